```python
import math
import jax
import jax.numpy as jnp
from jax import lax
import numpy as np


D_MODEL = 1024
BATCH = 8
SEQ = 2048
DEPTH = 2

GRID_W = 64
SSD_WIDTH = D_MODEL
SSD_HEAD_DIM = 64
SSD_HEADS = SSD_WIDTH // SSD_HEAD_DIM
SSD_GROUPS = 2
SSD_STATE = 128
SSD_CONV = 5
SSD_CHUNK = 128
SSD_XBC = SSD_WIDTH + 2 * SSD_GROUPS * SSD_STATE
SSD_IN = SSD_WIDTH + SSD_XBC + 2 * SSD_HEADS
DIFF_WIDTH = D_MODEL // 2
DIFF_HEAD_DIM = 64
DIFF_HEADS = DIFF_WIDTH // (2 * DIFF_HEAD_DIM)
DIFF_IN = 4 * DIFF_WIDTH
Q_BLOCK = 128
NA_WIDTH = D_MODEL // 2
NA_HEAD_DIM = 64
NA_HEADS = NA_WIDTH // NA_HEAD_DIM
NA_KH_MAX = 8
NA_KW = 16
NA_IN = 4 * NA_WIDTH

MIX_WIDTH = SSD_WIDTH + DIFF_WIDTH + NA_WIDTH
IN_WIDTH = SSD_IN + DIFF_IN + NA_IN
ROPE_THETA = 10000.0
EPS = 1e-6

kernel_name = 'hybrid_ssd_diffattn_natten_encoder'


def _rmsnorm(x, w):
    xf = x.astype(jnp.float32)
    y = xf * lax.rsqrt(jnp.mean(xf * xf, axis=-1, keepdims=True) + EPS)
    return (y * w.astype(jnp.float32)).astype(x.dtype)


def _rope(x, cos, sin):
    half = x.shape[-1] // 2
    x1, x2 = x[..., :half], x[..., half:]
    cos = cos.astype(x.dtype)
    sin = sin.astype(x.dtype)
    return jnp.concatenate([x1 * cos - x2 * sin, x2 * cos + x1 * sin], axis=-1)


def _dwconv_centred(x, w, b):
    k = w.shape[0]
    y = lax.conv_general_dilated(
        x, w[:, None, :].astype(x.dtype), window_strides=(1,),
        padding=[(k // 2, k // 2)], dimension_numbers=('NWC', 'WIO', 'NWC'),
        feature_group_count=x.shape[-1])
    return y + b.astype(x.dtype)


def _segsum(a):
    t = a.shape[-1]
    cs = jnp.cumsum(a, axis=-1)
    seg = cs[..., :, None] - cs[..., None, :]
    mask = jnp.tril(jnp.ones((t, t), dtype=bool))
    return jnp.where(mask, seg, -jnp.inf)


def _ssd_chunked(x, a, bm, cm):
    b, L, h, p = x.shape
    nc = L // SSD_CHUNK
    x = x.reshape(b, nc, SSD_CHUNK, h, p)
    bm = bm.reshape(b, nc, SSD_CHUNK, h, -1)
    cm = cm.reshape(b, nc, SSD_CHUNK, h, -1)
    a = a.reshape(b, nc, SSD_CHUNK, h).transpose(0, 3, 1, 2)
    a_cs = jnp.cumsum(a, axis=-1)
    lmat = jnp.exp(_segsum(a))
    scores = jnp.einsum('bclhn,bcshn->bhcls', cm, bm) * lmat
    y_diag = jnp.einsum('bhcls,bcshp->bclhp', scores, x)
    decay_states = jnp.exp(a_cs[..., -1:] - a_cs).transpose(0, 2, 3, 1)
    states = jnp.einsum('bclhn,bclhp->bchpn', bm * decay_states[..., None], x)
    states = jnp.concatenate([jnp.zeros_like(states[:, :1]), states], axis=1)
    decay_chunk = jnp.exp(_segsum(jnp.pad(a_cs[..., -1], ((0, 0), (0, 0), (1, 0)))))
    entering = jnp.einsum('bhzc,bchpn->bzhpn', decay_chunk, states)[:, :-1]
    state_decay_out = jnp.exp(a_cs).transpose(0, 2, 3, 1)[..., None]
    y_off = jnp.einsum('bclhn,bchpn->bclhp', cm, entering) * state_decay_out
    return (y_diag + y_off).reshape(b, L, h, p)


def _ssd_branch(u, conv_w, conv_b, a_log, dt_bias, d_skip, norm_w):
    b, L, _ = u.shape
    z = u[..., :SSD_WIDTH]
    xbc = u[..., SSD_WIDTH:SSD_WIDTH + SSD_XBC]
    dt = u[..., SSD_WIDTH + SSD_XBC:]
    xbc = jax.nn.silu(_dwconv_centred(xbc, conv_w, conv_b)).astype(jnp.float32)
    xs = xbc[..., :SSD_WIDTH].reshape(b, L, SSD_HEADS, SSD_HEAD_DIM)
    gs = SSD_GROUPS * SSD_STATE
    rep = SSD_HEADS // SSD_GROUPS
    bm = jnp.repeat(xbc[..., SSD_WIDTH:SSD_WIDTH + gs].reshape(b, L, SSD_GROUPS, SSD_STATE), rep, axis=2)
    cm = jnp.repeat(xbc[..., SSD_WIDTH + gs:].reshape(b, L, SSD_GROUPS, SSD_STATE), rep, axis=2)
    dt = jax.nn.softplus(dt.reshape(b, L, 2, SSD_HEADS).astype(jnp.float32) + dt_bias.astype(jnp.float32))
    a = -jnp.exp(a_log.astype(jnp.float32))
    dta = dt * a
    flip = lambda t: jnp.flip(t, axis=1)
    y_f = _ssd_chunked(xs * dt[:, :, 0, :, None], dta[:, :, 0], bm, cm)
    y_b = flip(_ssd_chunked(flip(xs * dt[:, :, 1, :, None]), flip(dta[:, :, 1]), flip(bm), flip(cm)))
    dsk = d_skip.astype(jnp.float32)
    y = y_f + y_b + (dsk[0] + dsk[1])[:, None] * xs
    y = y.reshape(b, L, SSD_WIDTH) * jax.nn.silu(z.astype(jnp.float32))
    yg = y.reshape(b, L, SSD_GROUPS, SSD_WIDTH // SSD_GROUPS)
    yg = yg * lax.rsqrt(jnp.mean(yg * yg, axis=-1, keepdims=True) + EPS)
    y = yg.reshape(b, L, SSD_WIDTH) * norm_w.astype(jnp.float32)
    return y.astype(u.dtype)


def _diff_branch(u, qk_norm_w, lam, subln_w, lam_init, cos, sin):
    b, L, _ = u.shape
    h, d = DIFF_HEADS, DIFF_HEAD_DIM
    q, k, v, g = jnp.split(u, 4, axis=-1)
    q = q.reshape(b, L, h, 2, d).transpose(0, 2, 3, 1, 4)
    k = k.reshape(b, L, h, 2, d).transpose(0, 2, 3, 1, 4)
    q = _rope(_rmsnorm(q, qk_norm_w[0]), cos, sin) * (d ** -0.5)
    k = _rope(_rmsnorm(k, qk_norm_w[1]), cos, sin)
    v = v.reshape(b, L, h, 2 * d).transpose(0, 2, 1, 3)
    lf = lam.astype(jnp.float32)
    lam_full = jnp.exp(jnp.sum(lf[0] * lf[1])) - jnp.exp(jnp.sum(lf[2] * lf[3])) + lam_init
    nb = L // Q_BLOCK
    qb = q.reshape(b, h, 2, nb, Q_BLOCK, d).transpose(3, 0, 1, 2, 4, 5)

    def one_block(qblk):
        s = jnp.einsum('bhcqd,bhckd->bhcqk', qblk, k).astype(jnp.float32)
        p = jax.nn.softmax(s, axis=-1)
        attn = p[:, :, 0] - lam_full * p[:, :, 1]
        return jnp.einsum('bhqk,bhkd->bhqd', attn.astype(v.dtype), v)

    o = lax.map(one_block, qb)
    o = o.transpose(1, 0, 3, 2, 4).reshape(b, L, h, 2 * d)
    o = _rmsnorm(o, subln_w) * (1.0 - lam_init)
    return o.reshape(b, L, DIFF_WIDTH) * jax.nn.silu(g)


def _na_branch(u, qk_norm_w, rpb):
    b, L, _ = u.shape
    h, d = NA_HEADS, NA_HEAD_DIM
    rows = L // GRID_W
    kh = min(NA_KH_MAX, rows)
    q, k, v, g = jnp.split(u, 4, axis=-1)
    to_grid = lambda t: t.reshape(b, rows, GRID_W, h, d).transpose(0, 3, 1, 2, 4)
    qg = to_grid(_rmsnorm(q.reshape(b, L, h, d), qk_norm_w[0]) * (d ** -0.5))
    kg = to_grid(_rmsnorm(k.reshape(b, L, h, d), qk_norm_w[1]))
    vg = to_grid(v)
    r = jnp.arange(rows)
    row_start = jnp.clip(r - kh // 2, 0, rows - kh)
    row_idx = row_start[:, None] + jnp.arange(kh)
    kb = kg[:, :, row_idx]
    vb = vg[:, :, row_idx]
    s = jnp.einsum('bhrqd,bhrjkd->bhrqjk', qg, kb).astype(jnp.float32)
    c = jnp.arange(GRID_W)
    col_start = jnp.clip(c - NA_KW // 2, 0, GRID_W - NA_KW)
    col_ok = (c[None, :] >= col_start[:, None]) & (c[None, :] < col_start[:, None] + NA_KW)
    dr_i = row_idx - r[:, None] + (NA_KH_MAX - 1)
    dc_i = jnp.clip(c[None, :] - c[:, None] + (NA_KW - 1), 0, 2 * NA_KW - 2)
    bias = rpb.astype(jnp.float32)[:, dr_i[:, None, :, None], dc_i[None, :, None, :]]
    s = jnp.where(col_ok[:, None, :], s + bias[None], -jnp.inf)
    p = jax.nn.softmax(s.reshape(s.shape[:4] + (kh * GRID_W,)), axis=-1).reshape(s.shape)
    o = jnp.einsum('bhrqjk,bhrjkd->bhrqd', p.astype(vb.dtype), vb)
    o = o.transpose(0, 2, 3, 1, 4).reshape(b, L, NA_WIDTH)
    return o * jax.nn.silu(g)


def setup_inputs(seed: int = 0) -> dict:
    key = jax.random.key(seed)
    ks = jax.random.split(key, 20)
    f32 = jnp.float32
    x = jax.random.normal(ks[0], (BATCH, SEQ, D_MODEL), f32)
    norm_w = 1.0 + 0.02 * jax.random.normal(ks[1], (DEPTH, D_MODEL), f32)
    w_in = jax.random.normal(ks[2], (DEPTH, D_MODEL, IN_WIDTH), f32) * (D_MODEL ** -0.5)
    conv_w = jax.random.normal(ks[3], (DEPTH, SSD_CONV, SSD_XBC), f32) * (SSD_CONV ** -0.5)
    conv_b = 0.01 * jax.random.normal(ks[4], (DEPTH, SSD_XBC), f32)
    a_log = jnp.log(jax.random.uniform(ks[5], (DEPTH, 2, SSD_HEADS), f32, 1.0, 16.0))
    dt0 = jnp.exp(jax.random.uniform(ks[6], (DEPTH, 2, SSD_HEADS), f32, math.log(1e-3), math.log(1e-1)))
    dt_bias = dt0 + jnp.log(-jnp.expm1(-dt0))
    d_skip = 1.0 + 0.1 * jax.random.normal(ks[7], (DEPTH, 2, SSD_HEADS), f32)
    ssd_norm_w = 1.0 + 0.02 * jax.random.normal(ks[8], (DEPTH, SSD_WIDTH), f32)
    diff_qk_norm = 1.0 + 0.02 * jax.random.normal(ks[9], (DEPTH, 2, DIFF_HEAD_DIM), f32)
    diff_lambda = 0.1 * jax.random.normal(ks[10], (DEPTH, 4, DIFF_HEAD_DIM), f32)
    diff_subln = 1.0 + 0.02 * jax.random.normal(ks[11], (DEPTH, 2 * DIFF_HEAD_DIM), f32)
    na_qk_norm = 1.0 + 0.02 * jax.random.normal(ks[12], (DEPTH, 2, NA_HEAD_DIM), f32)
    na_rpb = 0.02 * jax.random.normal(ks[13], (DEPTH, NA_HEADS, 2 * NA_KH_MAX - 1, 2 * NA_KW - 1), f32)
    w_out = jax.random.normal(ks[14], (DEPTH, MIX_WIDTH, D_MODEL), f32) * (MIX_WIDTH ** -0.5)
    return {'x': x, 'norm_w': norm_w, 'w_in': w_in, 'conv_w': conv_w, 'conv_b': conv_b,
            'a_log': a_log, 'dt_bias': dt_bias, 'd_skip': d_skip, 'ssd_norm_w': ssd_norm_w,
            'diff_qk_norm': diff_qk_norm, 'diff_lambda': diff_lambda, 'diff_subln': diff_subln,
            'na_qk_norm': na_qk_norm, 'na_rpb': na_rpb, 'w_out': w_out}


def reference(x, norm_w, w_in, conv_w, conv_b, a_log, dt_bias, d_skip, ssd_norm_w,
              diff_qk_norm, diff_lambda, diff_subln, na_qk_norm, na_rpb, w_out):
    L = x.shape[1]
    inv_freq = ROPE_THETA ** (-jnp.arange(0, DIFF_HEAD_DIM, 2, dtype=jnp.float32) / DIFF_HEAD_DIM)
    ang = jnp.arange(L, dtype=jnp.float32)[:, None] * inv_freq[None, :]
    cos, sin = jnp.cos(ang), jnp.sin(ang)
    for i in range(DEPTH):
        lam_init = 0.8 - 0.6 * math.exp(-0.3 * i)
        hdn = _rmsnorm(x, norm_w[i])
        u = jnp.einsum('bld,de->ble', hdn, w_in[i].astype(hdn.dtype))
        u_ssd = u[..., :SSD_IN]
        u_diff = u[..., SSD_IN:SSD_IN + DIFF_IN]
        u_na = u[..., SSD_IN + DIFF_IN:]
        y_ssd = _ssd_branch(u_ssd, conv_w[i], conv_b[i], a_log[i], dt_bias[i], d_skip[i], ssd_norm_w[i])
        y_diff = _diff_branch(u_diff, diff_qk_norm[i], diff_lambda[i], diff_subln[i], lam_init, cos, sin)
        y_na = _na_branch(u_na, na_qk_norm[i], na_rpb[i])
        y = jnp.concatenate([y_ssd, y_diff, y_na], axis=-1)
        x = x + jnp.einsum('ble,ed->bld', y, w_out[i].astype(y.dtype))
    return x
```

```python
import functools
import math

import numpy as np
import jax
import jax.numpy as jnp
from jax import lax
from jax.experimental import pallas as pl
from jax.experimental.pallas import tpu as pltpu

F32 = jnp.float32
BF16 = jnp.bfloat16

D_MODEL = 1024
SEQ = 2048
GRID_W = 64
ROWS = SEQ // GRID_W
SSD_WIDTH = 1024
SSD_HEAD_DIM = 64
SSD_HEADS = 16
SSD_GROUPS = 2
SSD_STATE = 128
SSD_CONV = 5
SSD_CHUNK = 128
SSD_XBC = SSD_WIDTH + 2 * SSD_GROUPS * SSD_STATE
GROUP_HEADS = SSD_HEADS // SSD_GROUPS
GROUP_WIDTH = SSD_WIDTH // SSD_GROUPS
N_CHUNKS = SEQ // SSD_CHUNK
DIFF_WIDTH = 512
DIFF_HEAD_DIM = 64
DIFF_HEADS = 4
NA_WIDTH = 512
NA_HEAD_DIM = 64
NA_HEADS = 8
NA_KH = 8
NA_KW = 16
MIX_WIDTH = SSD_WIDTH + DIFF_WIDTH + NA_WIDTH
ROPE_THETA = 10000.0
EPS = 1e-6
LANES = 128
HALO = 8

U_MAIN = SSD_WIDTH + SSD_XBC + 4 * DIFF_WIDTH + 4 * NA_WIDTH
COL_Z = 0
COL_X = SSD_WIDTH
COL_B = COL_X + SSD_WIDTH
COL_C = COL_B + SSD_GROUPS * SSD_STATE
COL_DIFF = COL_C + SSD_GROUPS * SSD_STATE
COL_NA = COL_DIFF + 4 * DIFF_WIDTH
DT_PAD = LANES

NA_QROWS = 4
NA_BAND = 12
NA_QBLK = NA_QROWS * GRID_W
NA_KBLK = NA_BAND * GRID_W
NA_NBLK = ROWS // NA_QROWS
NEG = -1e30

VMEM_LIMIT = 56 * 1024 * 1024


def _silu(v):
    return v * (1.0 / (1.0 + jnp.exp(-v)))


def _dot(a, b):
    return jnp.dot(a, b, preferred_element_type=F32)


def _dot_nt(a, b):
    return lax.dot_general(a, b, (((1,), (1,)), ((), ())), preferred_element_type=F32)


def _dot_tn(a, b):
    return lax.dot_general(a, b, (((0,), (0,)), ((), ())), preferred_element_type=F32)


IN_TM = 1024
IN_TN = 1664


def _inproj_kernel(x_ref, nw_ref, w_ref, wdt_ref, u_ref, dt_ref, h_scr):
    @pl.when(pl.program_id(1) == 0)
    def _():
        x = x_ref[...]
        ms = jnp.mean(x * x, axis=-1, keepdims=True)
        h = (x * lax.rsqrt(ms + EPS) * nw_ref[...]).astype(BF16)
        h_scr[...] = h
        dt_ref[...] = _dot(h, wdt_ref[...])

    u_ref[...] = _dot(h_scr[...], w_ref[...]).astype(u_ref.dtype)


def _inproj(x2, norm_w, w_main, w_dt):
    m = x2.shape[0]
    return pl.pallas_call(
        _inproj_kernel,
        grid=(m // IN_TM, U_MAIN // IN_TN),
        in_specs=[
            pl.BlockSpec((IN_TM, D_MODEL), lambda i, j: (i, 0)),
            pl.BlockSpec((1, D_MODEL), lambda i, j: (0, 0)),
            pl.BlockSpec((D_MODEL, IN_TN), lambda i, j: (0, j)),
            pl.BlockSpec((D_MODEL, DT_PAD), lambda i, j: (0, 0)),
        ],
        out_specs=[
            pl.BlockSpec((IN_TM, IN_TN), lambda i, j: (i, j)),
            pl.BlockSpec((IN_TM, DT_PAD), lambda i, j: (i, 0)),
        ],
        out_shape=[
            jax.ShapeDtypeStruct((m, U_MAIN), BF16),
            jax.ShapeDtypeStruct((m, DT_PAD), F32),
        ],
        scratch_shapes=[pltpu.VMEM((IN_TM, D_MODEL), BF16)],
        compiler_params=pltpu.CompilerParams(
            dimension_semantics=("arbitrary", "arbitrary"), vmem_limit_bytes=VMEM_LIMIT),
        name="inproj",
    )(x2, norm_w, w_main, w_dt)


Q_CSF, Q_ECSB, Q_WF, Q_WB, Q_DF, Q_DB, Q_DTF, Q_DTB, Q_EF, Q_EB = range(10)
N_QUANT = 10


def _softplus(v):
    return jnp.maximum(v, 0.0) + jnp.log1p(jnp.exp(-jnp.abs(v)))


def _chunk_scan(a, lane, reverse):
    n = a.shape[-1]
    out = a
    k = 1
    while k < SSD_CHUNK:
        if reverse:
            out = out + jnp.where(lane < SSD_CHUNK - k, pltpu.roll(out, n - k, axis=1), 0.0)
        else:
            out = out + jnp.where(lane >= k, pltpu.roll(out, k, axis=1), 0.0)
        k *= 2
    return out


def _expand_heads(col_block):
    rows = col_block.shape[0]
    left = lax.broadcasted_iota(jnp.int32, (rows, LANES), 1) < SSD_HEAD_DIM
    pieces = []
    for p in range(GROUP_HEADS // 2):
        even = jnp.broadcast_to(col_block[:, 2 * p:2 * p + 1], (rows, LANES))
        odd = jnp.broadcast_to(col_block[:, 2 * p + 1:2 * p + 2], (rows, LANES))
        pieces.append(jnp.where(left, even, odd))
    return jnp.concatenate(pieces, axis=1)


def _ssd_kernel(z_ref, x_ref, b_ref, c_ref, dtf_ref, dtb_ref, cwx_ref, cwb_ref, cwc_ref,
                cbx_ref, cbb_ref, cbc_ref, hp_ref, dsk_ref, nw_ref, o_ref,
                pad_scr, xs_scr, bm_scr, cm_scr, y_scr, hm_scr, tm_scr, sf_scr, sb_scr):
    L = SEQ
    W = GROUP_WIDTH
    NS = SSD_STATE
    CW = W + 2 * NS

    zeros_halo = jnp.zeros((HALO, CW), F32)
    pad_scr[0:HALO, :] = zeros_halo
    pad_scr[HALO + L:HALO + L + HALO, :] = zeros_halo

    def stage(c, carry):
        r = pl.multiple_of(c * SSD_CHUNK, SSD_CHUNK)
        pad_scr[pl.ds(HALO + r, SSD_CHUNK), 0:W] = x_ref[0, pl.ds(r, SSD_CHUNK), :].astype(F32)
        pad_scr[pl.ds(HALO + r, SSD_CHUNK), W:W + NS] = b_ref[0, pl.ds(r, SSD_CHUNK), :].astype(F32)
        pad_scr[pl.ds(HALO + r, SSD_CHUNK), W + NS:CW] = c_ref[0, pl.ds(r, SSD_CHUNK), :].astype(F32)
        return carry

    lax.fori_loop(0, N_CHUNKS, stage, 0)

    cw = jnp.concatenate([cwx_ref[...], cwb_ref[...], cwc_ref[...]], axis=1)
    cb = jnp.concatenate([cbx_ref[...], cbb_ref[...], cbc_ref[...]], axis=1)

    def conv(c, carry):
        r = pl.multiple_of(c * SSD_CHUNK, SSD_CHUNK)
        win = pad_scr[pl.ds(r, SSD_CHUNK + 2 * HALO), :]
        acc = jnp.broadcast_to(cb, (SSD_CHUNK, CW))
        for k in range(SSD_CONV):
            off = HALO - SSD_CONV // 2 + k
            acc = acc + win[off:off + SSD_CHUNK, :] * cw[k:k + 1, :]
        act = _silu(acc)
        xs_scr[pl.ds(r, SSD_CHUNK), :] = act[:, 0:W]
        bm_scr[pl.ds(r, SSD_CHUNK), :] = act[:, W:W + NS].astype(BF16)
        cm_scr[pl.ds(r, SSD_CHUNK), :] = act[:, W + NS:CW].astype(BF16)
        return carry

    lax.fori_loop(0, N_CHUNKS, conv, 0)

    hp = hp_ref[0]
    a_f = -jnp.exp(hp[:, 0:1])
    a_b = -jnp.exp(hp[:, 1:2])
    dt_f = _softplus(dtf_ref[0] + hp[:, 2:3])
    dt_b = _softplus(dtb_ref[0] + hp[:, 3:4])
    da_f = dt_f * a_f
    da_b = dt_b * a_b
    lane = lax.broadcasted_iota(jnp.int32, (GROUP_HEADS, L), 1) % SSD_CHUNK
    cs_f = _chunk_scan(da_f, lane, False)
    rs_f = _chunk_scan(da_f, lane, True) - da_f
    cs_b = _chunk_scan(da_b, lane, False)
    ecs_b = cs_b - da_b
    rs_b = _chunk_scan(da_b, lane, True)
    quants = {
        Q_CSF: cs_f,
        Q_ECSB: ecs_b,
        Q_WF: jnp.exp(rs_f) * dt_f,
        Q_WB: jnp.exp(ecs_b) * dt_b,
        Q_DF: jnp.exp(cs_f),
        Q_DB: jnp.exp(rs_b),
        Q_DTF: dt_f,
        Q_DTB: dt_b,
        Q_EF: jnp.exp(cs_f + rs_f),
        Q_EB: jnp.exp(ecs_b + rs_b),
    }
    for qi in range(N_QUANT):
        hm_scr[qi * GROUP_HEADS:(qi + 1) * GROUP_HEADS, :] = quants[qi]
    hm_scr[N_QUANT * GROUP_HEADS:LANES, :] = jnp.zeros((LANES - N_QUANT * GROUP_HEADS, L), F32)

    def to_time_major(c, carry):
        r = pl.multiple_of(c * SSD_CHUNK, SSD_CHUNK)
        tm_scr[pl.ds(r, SSD_CHUNK), :] = hm_scr[:, pl.ds(r, SSD_CHUNK)].T
        return carry

    lax.fori_loop(0, N_CHUNKS, to_time_major, 0)

    def tm_cols(r, q):
        return tm_scr[pl.ds(r, SSD_CHUNK), q * GROUP_HEADS:(q + 1) * GROUP_HEADS]

    def chunk_decay_row(r, q):
        row = tm_scr[pl.ds(r, 8), q * GROUP_HEADS:(q + 1) * GROUP_HEADS]
        return _expand_heads(row)[0:1, :]

    sb_scr[...] = jnp.zeros((NS, W), F32)

    def bwd(i, carry):
        c = N_CHUNKS - 1 - i
        r = pl.multiple_of(c * SSD_CHUNK, SSD_CHUNK)
        xs_c = xs_scr[pl.ds(r, SSD_CHUNK), :]
        s_b = sb_scr[...]
        y_off = _dot(cm_scr[pl.ds(r, SSD_CHUNK), :], s_b.astype(BF16))
        y_scr[pl.ds(r, SSD_CHUNK), :] = y_off * _expand_heads(tm_cols(r, Q_DB))
        xw = (xs_c * _expand_heads(tm_cols(r, Q_WB))).astype(BF16)
        sb_scr[...] = s_b * chunk_decay_row(r, Q_EB) + _dot_tn(bm_scr[pl.ds(r, SSD_CHUNK), :], xw)
        return carry

    lax.fori_loop(0, N_CHUNKS, bwd, 0)

    sf_scr[...] = jnp.zeros((NS, W), F32)
    row_i = lax.broadcasted_iota(jnp.int32, (SSD_CHUNK, SSD_CHUNK), 0)
    col_i = lax.broadcasted_iota(jnp.int32, (SSD_CHUNK, SSD_CHUNK), 1)
    lower = col_i <= row_i
    upper = col_i >= row_i
    lane2 = lax.broadcasted_iota(jnp.int32, (SSD_CHUNK, LANES), 1)
    left = lane2 < SSD_HEAD_DIM
    dsk = dsk_ref[0]
    nw = nw_ref[0]

    def fwd(c, carry):
        r = pl.multiple_of(c * SSD_CHUNK, SSD_CHUNK)
        xs_c = xs_scr[pl.ds(r, SSD_CHUNK), :]
        bm_c = bm_scr[pl.ds(r, SSD_CHUNK), :]
        cm_c = cm_scr[pl.ds(r, SSD_CHUNK), :]
        g = _dot_nt(cm_c, bm_c)
        xs_b = xs_c.astype(BF16)
        csf_cols = tm_cols(r, Q_CSF)
        ecsb_cols = tm_cols(r, Q_ECSB)
        y_pairs = []
        for hp_i in range(GROUP_HEADS // 2):
            ms = []
            for h in (2 * hp_i, 2 * hp_i + 1):
                csf_row = hm_scr[Q_CSF * GROUP_HEADS + h:Q_CSF * GROUP_HEADS + h + 1, pl.ds(r, SSD_CHUNK)]
                ecsb_row = hm_scr[Q_ECSB * GROUP_HEADS + h:Q_ECSB * GROUP_HEADS + h + 1, pl.ds(r, SSD_CHUNK)]
                dtf_row = hm_scr[Q_DTF * GROUP_HEADS + h:Q_DTF * GROUP_HEADS + h + 1, pl.ds(r, SSD_CHUNK)]
                dtb_row = hm_scr[Q_DTB * GROUP_HEADS + h:Q_DTB * GROUP_HEADS + h + 1, pl.ds(r, SSD_CHUNK)]
                seg_f = jnp.where(lower, csf_cols[:, h:h + 1] - csf_row, NEG)
                seg_b = jnp.where(upper, ecsb_row - ecsb_cols[:, h:h + 1], NEG)
                p = jnp.exp(seg_f) * dtf_row + jnp.exp(seg_b) * dtb_row
                ms.append((g * p).astype(BF16))
            xp = xs_b[:, hp_i * LANES:(hp_i + 1) * LANES]
            zero = jnp.zeros_like(xp)
            rhs = jnp.concatenate([jnp.where(left, xp, zero), jnp.where(left, zero, xp)], axis=0)
            y_pairs.append(_dot(jnp.concatenate(ms, axis=1), rhs))
        y_diag = jnp.concatenate(y_pairs, axis=1)
        s_f = sf_scr[...]
        y_off = _dot(cm_c, s_f.astype(BF16)) * _expand_heads(tm_cols(r, Q_DF))
        xw = (xs_c * _expand_heads(tm_cols(r, Q_WF))).astype(BF16)
        sf_scr[...] = s_f * chunk_decay_row(r, Q_EF) + _dot_tn(bm_c, xw)
        y = y_scr[pl.ds(r, SSD_CHUNK), :] + y_diag + y_off + dsk * xs_c
        y = y * _silu(z_ref[0, pl.ds(r, SSD_CHUNK), :].astype(F32))
        y = y * lax.rsqrt(jnp.mean(y * y, axis=-1, keepdims=True) + EPS) * nw
        o_ref[0, pl.ds(r, SSD_CHUNK), :] = y.astype(o_ref.dtype)
        return carry

    lax.fori_loop(0, N_CHUNKS, fwd, 0)


def _ssd(u3, dt_t, conv_w, conv_b, head_params, dsk_row, norm_w):
    b = u3.shape[0]
    L = SEQ
    W = GROUP_WIDTH
    NS = SSD_STATE
    G = SSD_GROUPS
    xblk = COL_X // W
    bblk = COL_B // NS
    cblk = COL_C // NS
    in_specs = [
        pl.BlockSpec((1, L, W), lambda i, g: (i, 0, g)),
        pl.BlockSpec((1, L, W), lambda i, g: (i, 0, xblk + g)),
        pl.BlockSpec((1, L, NS), lambda i, g: (i, 0, bblk + g)),
        pl.BlockSpec((1, L, NS), lambda i, g: (i, 0, cblk + g)),
        pl.BlockSpec((1, GROUP_HEADS, L), lambda i, g: (i, g, 0)),
        pl.BlockSpec((1, GROUP_HEADS, L), lambda i, g: (i, G + g, 0)),
        pl.BlockSpec((SSD_CONV, W), lambda i, g: (0, g)),
        pl.BlockSpec((SSD_CONV, NS), lambda i, g: (0, SSD_WIDTH // NS + g)),
        pl.BlockSpec((SSD_CONV, NS), lambda i, g: (0, SSD_WIDTH // NS + G + g)),
        pl.BlockSpec((1, W), lambda i, g: (0, g)),
        pl.BlockSpec((1, NS), lambda i, g: (0, SSD_WIDTH // NS + g)),
        pl.BlockSpec((1, NS), lambda i, g: (0, SSD_WIDTH // NS + G + g)),
        pl.BlockSpec((1, GROUP_HEADS, 8), lambda i, g: (g, 0, 0)),
        pl.BlockSpec((1, 1, W), lambda i, g: (g, 0, 0)),
        pl.BlockSpec((1, 1, W), lambda i, g: (g, 0, 0)),
    ]
    return pl.pallas_call(
        _ssd_kernel,
        grid=(b, G),
        in_specs=in_specs,
        out_specs=pl.BlockSpec((1, L, W), lambda i, g: (i, 0, g)),
        out_shape=jax.ShapeDtypeStruct((b, L, SSD_WIDTH), BF16),
        scratch_shapes=[
            pltpu.VMEM((L + 2 * HALO, W + 2 * NS), F32),
            pltpu.VMEM((L, W), F32),
            pltpu.VMEM((L, NS), BF16),
            pltpu.VMEM((L, NS), BF16),
            pltpu.VMEM((L, W), F32),
            pltpu.VMEM((LANES, L), F32),
            pltpu.VMEM((L, LANES), F32),
            pltpu.VMEM((NS, W), F32),
            pltpu.VMEM((NS, W), F32),
        ],
        compiler_params=pltpu.CompilerParams(
            dimension_semantics=("arbitrary", "arbitrary"), vmem_limit_bytes=VMEM_LIMIT),
        name="ssd",
    )(u3, u3, u3, u3, dt_t, dt_t, conv_w, conv_w, conv_w, conv_b, conv_b, conv_b,
      head_params, dsk_row, norm_w)


def _group_mean_sq(v, seg):
    sq = v * v
    hi = sq.astype(BF16)
    lo = (sq - hi.astype(F32)).astype(BF16)
    return (_dot(hi, seg) + _dot(lo, seg)) * (1.0 / 64.0)


def _seg_matrix():
    r = lax.broadcasted_iota(jnp.int32, (LANES, LANES), 0) // 64
    c = lax.broadcasted_iota(jnp.int32, (LANES, LANES), 1) // 64
    return jnp.where(r == c, 1.0, 0.0).astype(BF16)


DIFF_TQ = 256


def _diff_kernel(lam_init, q_ref, k_ref, v_ref, g_ref, cos_ref, sin_ref, qkw_ref, lam_ref,
                 sub_ref, o_ref, q0_scr, q1_scr, k_scr):
    L = SEQ
    seg = _seg_matrix()
    lane = lax.broadcasted_iota(jnp.int32, (DIFF_TQ, LANES), 1)
    first_half = (lane % 64) < 32
    comp0 = lane < 64

    def norm_rope(ref, r, w_row):
        v = ref[0, pl.ds(r, DIFF_TQ), :].astype(F32)
        v = v * lax.rsqrt(_group_mean_sq(v, seg) + EPS) * w_row
        swapped = jnp.where(first_half, pltpu.roll(v, LANES - 32, axis=1), pltpu.roll(v, 32, axis=1))
        return v * cos_ref[pl.ds(r, DIFF_TQ), :] + swapped * sin_ref[pl.ds(r, DIFF_TQ), :]

    def prep(i, carry):
        r = pl.multiple_of(i * DIFF_TQ, DIFF_TQ)
        qn = norm_rope(q_ref, r, qkw_ref[0:1, :]) * (DIFF_HEAD_DIM ** -0.5)
        q0_scr[pl.ds(r, DIFF_TQ), :] = jnp.where(comp0, qn, 0.0).astype(BF16)
        q1_scr[pl.ds(r, DIFF_TQ), :] = jnp.where(comp0, 0.0, qn).astype(BF16)
        k_scr[pl.ds(r, DIFF_TQ), :] = norm_rope(k_ref, r, qkw_ref[1:2, :]).astype(BF16)
        return carry

    lax.fori_loop(0, L // DIFF_TQ, prep, 0)

    lam = lam_ref[...]
    lam_full = (jnp.exp(jnp.sum(lam[0:1] * lam[1:2], axis=-1, keepdims=True))
                - jnp.exp(jnp.sum(lam[2:3] * lam[3:4], axis=-1, keepdims=True)) + lam_init)

    def block(i, carry):
        r = pl.multiple_of(i * DIFF_TQ, DIFF_TQ)
        kk = k_scr[...]
        s0 = _dot_nt(q0_scr[pl.ds(r, DIFF_TQ), :], kk)
        s1 = _dot_nt(q1_scr[pl.ds(r, DIFF_TQ), :], kk)
        e0 = jnp.exp(s0 - jnp.max(s0, axis=-1, keepdims=True))
        e1 = jnp.exp(s1 - jnp.max(s1, axis=-1, keepdims=True))
        inv0 = 1.0 / jnp.sum(e0, axis=-1, keepdims=True)
        inv1 = lam_full / jnp.sum(e1, axis=-1, keepdims=True)
        attn = (e0 * inv0 - e1 * inv1).astype(BF16)
        o = _dot(attn, v_ref[0])
        o = o * lax.rsqrt(jnp.mean(o * o, axis=-1, keepdims=True) + EPS) * sub_ref[...]
        o = o * (1.0 - lam_init) * _silu(g_ref[0, pl.ds(r, DIFF_TQ), :].astype(F32))
        o_ref[0, pl.ds(r, DIFF_TQ), :] = o.astype(o_ref.dtype)
        return carry

    lax.fori_loop(0, L // DIFF_TQ, block, 0)


def _diff(u3, cos_t, sin_t, qk_w, lam, subln_w, lam_init):
    b = u3.shape[0]
    L = SEQ
    base = COL_DIFF // LANES
    nb = DIFF_WIDTH // LANES

    def spec(part):
        return pl.BlockSpec((1, L, LANES), lambda i, h: (i, 0, base + part * nb + h))

    return pl.pallas_call(
        functools.partial(_diff_kernel, lam_init),
        grid=(b, DIFF_HEADS),
        in_specs=[
            spec(0), spec(1), spec(2), spec(3),
            pl.BlockSpec((L, LANES), lambda i, h: (0, 0)),
            pl.BlockSpec((L, LANES), lambda i, h: (0, 0)),
            pl.BlockSpec((2, LANES), lambda i, h: (0, 0)),
            pl.BlockSpec((4, DIFF_HEAD_DIM), lambda i, h: (0, 0)),
            pl.BlockSpec((1, LANES), lambda i, h: (0, 0)),
        ],
        out_specs=pl.BlockSpec((1, L, LANES), lambda i, h: (i, 0, h)),
        out_shape=jax.ShapeDtypeStruct((b, L, DIFF_WIDTH), BF16),
        scratch_shapes=[pltpu.VMEM((L, LANES), BF16)] * 3,
        compiler_params=pltpu.CompilerParams(
            dimension_semantics=("arbitrary", "arbitrary"), vmem_limit_bytes=VMEM_LIMIT),
        name="diff_attn",
    )(u3, u3, u3, u3, cos_t, sin_t, qk_w, lam, subln_w)


def _na_block_geometry(qb):
    r0 = qb * NA_QROWS
    kb = min(max(r0 - NA_KH // 2, 0), ROWS - NA_BAND)
    return r0, kb


def _na_bias_type(qb):
    return 0 if qb == 0 else (2 if qb == NA_NBLK - 1 else 1)


def _na_kernel(q_ref, k_ref, v_ref, g_ref, qkw_ref, bias_ref, o_ref, q_scr, k_scr):
    L = SEQ
    seg = _seg_matrix()
    head0_blk = lax.broadcasted_iota(jnp.int32, (NA_QBLK, LANES), 1) < 64

    def norm(ref, r, w_row):
        v = ref[0, pl.ds(r, NA_QBLK), :].astype(F32)
        return v * lax.rsqrt(_group_mean_sq(v, seg) + EPS) * w_row

    def prep(i, carry):
        r = pl.multiple_of(i * NA_QBLK, NA_QBLK)
        qn = norm(q_ref, r, qkw_ref[0:1, :]) * (NA_HEAD_DIM ** -0.5)
        q_scr[0, pl.ds(r, NA_QBLK), :] = jnp.where(head0_blk, qn, 0.0).astype(BF16)
        q_scr[1, pl.ds(r, NA_QBLK), :] = jnp.where(head0_blk, 0.0, qn).astype(BF16)
        k_scr[pl.ds(r, NA_QBLK), :] = norm(k_ref, r, qkw_ref[1:2, :]).astype(BF16)
        return carry

    lax.fori_loop(0, L // NA_QBLK, prep, 0)

    for qb in range(NA_NBLK):
        r0, kb = _na_block_geometry(qb)
        t = _na_bias_type(qb)
        qs = slice(r0 * GRID_W, r0 * GRID_W + NA_QBLK)
        ks = slice(kb * GRID_W, kb * GRID_W + NA_KBLK)
        kk = k_scr[ks, :]
        vv = v_ref[0, ks, :]
        outs = []
        for h in range(2):
            s = _dot_nt(q_scr[h, qs, :], kk) + bias_ref[h, t]
            e = jnp.exp(s - jnp.max(s, axis=-1, keepdims=True))
            p = (e * (1.0 / jnp.sum(e, axis=-1, keepdims=True))).astype(BF16)
            outs.append(_dot(p, vv))
        o = jnp.where(head0_blk, outs[0], outs[1])
        o = o * _silu(g_ref[0, qs, :].astype(F32))
        o_ref[0, qs, :] = o.astype(o_ref.dtype)


def _na(u3, qk_w, bias):
    b = u3.shape[0]
    L = SEQ
    base = COL_NA // LANES
    nb = NA_WIDTH // LANES

    def spec(part):
        return pl.BlockSpec((1, L, LANES), lambda hp, i: (i, 0, base + part * nb + hp))

    return pl.pallas_call(
        _na_kernel,
        grid=(NA_HEADS // 2, b),
        in_specs=[
            spec(0), spec(1), spec(2), spec(3),
            pl.BlockSpec((2, LANES), lambda hp, i: (0, 0)),
            pl.BlockSpec((2, 3, NA_QBLK, NA_KBLK), lambda hp, i: (hp, 0, 0, 0)),
        ],
        out_specs=pl.BlockSpec((1, L, LANES), lambda hp, i: (i, 0, hp)),
        out_shape=jax.ShapeDtypeStruct((b, L, NA_WIDTH), BF16),
        scratch_shapes=[pltpu.VMEM((2, L, LANES), BF16), pltpu.VMEM((L, LANES), BF16)],
        compiler_params=pltpu.CompilerParams(
            dimension_semantics=("arbitrary", "arbitrary"), vmem_limit_bytes=VMEM_LIMIT),
        name="na_attn",
    )(u3, u3, u3, u3, qk_w, bias)


def _na_bias(rpb):
    c = np.arange(GRID_W)
    col_start = np.clip(c - NA_KW // 2, 0, GRID_W - NA_KW)
    col_ok = (c[None, :] >= col_start[:, None]) & (c[None, :] < col_start[:, None] + NA_KW)
    dc_i = np.clip(c[None, :] - c[:, None] + (NA_KW - 1), 0, 2 * NA_KW - 2)
    tiles = jnp.where(col_ok[None, None], rpb.astype(F32)[:, :, dc_i], NEG)
    blocks = []
    for qb in (0, 1, NA_NBLK - 1):
        r0, kb = _na_block_geometry(qb)
        r = r0 + np.arange(NA_QROWS)
        rs = np.clip(r - NA_KH // 2, 0, ROWS - NA_KH)
        rk = kb + np.arange(NA_BAND)
        valid = (rk[None, :] >= rs[:, None]) & (rk[None, :] < rs[:, None] + NA_KH)
        dr = np.clip(rk[None, :] - r[:, None] + (NA_KH - 1), 0, 2 * NA_KH - 2)
        blk = jnp.where(valid[None, :, :, None, None], tiles[:, dr], NEG)
        blocks.append(blk.transpose(0, 1, 3, 2, 4).reshape(NA_HEADS, NA_QBLK, NA_KBLK))
    return jnp.stack(blocks, axis=1)


OUT_TM = 512


def _outproj_kernel(x_ref, ys_ref, yd_ref, yn_ref, w_ref, o_ref):
    acc = _dot(ys_ref[...], w_ref[0:SSD_WIDTH, :])
    acc = acc + _dot(yd_ref[...], w_ref[SSD_WIDTH:SSD_WIDTH + DIFF_WIDTH, :])
    acc = acc + _dot(yn_ref[...], w_ref[SSD_WIDTH + DIFF_WIDTH:MIX_WIDTH, :])
    o_ref[...] = x_ref[...] + acc


def _outproj(x2, y_ssd, y_diff, y_na, w_out):
    m = x2.shape[0]
    return pl.pallas_call(
        _outproj_kernel,
        grid=(m // OUT_TM,),
        in_specs=[
            pl.BlockSpec((OUT_TM, D_MODEL), lambda i: (i, 0)),
            pl.BlockSpec((OUT_TM, SSD_WIDTH), lambda i: (i, 0)),
            pl.BlockSpec((OUT_TM, DIFF_WIDTH), lambda i: (i, 0)),
            pl.BlockSpec((OUT_TM, NA_WIDTH), lambda i: (i, 0)),
            pl.BlockSpec((MIX_WIDTH, D_MODEL), lambda i: (0, 0)),
        ],
        out_specs=pl.BlockSpec((OUT_TM, D_MODEL), lambda i: (i, 0)),
        out_shape=jax.ShapeDtypeStruct((m, D_MODEL), F32),
        compiler_params=pltpu.CompilerParams(
            dimension_semantics=("arbitrary",), vmem_limit_bytes=VMEM_LIMIT),
        name="outproj",
    )(x2, y_ssd, y_diff, y_na, w_out)


def _rope_tables():
    inv_freq = ROPE_THETA ** (-jnp.arange(0, DIFF_HEAD_DIM, 2, dtype=F32) / DIFF_HEAD_DIM)
    ang = jnp.arange(SEQ, dtype=F32)[:, None] * inv_freq[None, :]
    cos, sin = jnp.cos(ang), jnp.sin(ang)
    cos_t = jnp.concatenate([cos, cos, cos, cos], axis=1)
    sin_t = jnp.concatenate([-sin, sin, -sin, sin], axis=1)
    return cos_t, sin_t


def kernel(x, norm_w, w_in, conv_w, conv_b, a_log, dt_bias, d_skip, ssd_norm_w, diff_qk_norm,
           diff_lambda, diff_subln, na_qk_norm, na_rpb, w_out):
    b, L, d = x.shape
    assert (L, d) == (SEQ, D_MODEL)
    depth = w_in.shape[0]
    cos_t, sin_t = _rope_tables()
    dt_lo = SSD_WIDTH + SSD_XBC
    dt_hi = dt_lo + 2 * SSD_HEADS
    x2 = x.reshape(b * L, d)
    for i in range(depth):
        lam_init = 0.8 - 0.6 * math.exp(-0.3 * i)
        w_main = jnp.concatenate([w_in[i][:, :dt_lo], w_in[i][:, dt_hi:]], axis=1).astype(BF16)
        w_dt = jnp.pad(w_in[i][:, dt_lo:dt_hi], ((0, 0), (0, DT_PAD - 2 * SSD_HEADS))).astype(BF16)
        u, dt = _inproj(x2, norm_w[i][None, :], w_main, w_dt)
        u3 = u.reshape(b, L, U_MAIN)
        dt_t = jnp.swapaxes(dt.reshape(b, L, DT_PAD)[:, :, :2 * SSD_HEADS], 1, 2)

        hp = jnp.stack([a_log[i][0], a_log[i][1], dt_bias[i][0], dt_bias[i][1]], axis=-1)
        hp = jnp.pad(hp, ((0, 0), (0, 4))).reshape(SSD_GROUPS, GROUP_HEADS, 8)
        dsk_row = jnp.repeat(d_skip[i][0] + d_skip[i][1], SSD_HEAD_DIM).reshape(SSD_GROUPS, 1, GROUP_WIDTH)
        y_ssd = _ssd(u3, dt_t, conv_w[i], conv_b[i][None, :], hp, dsk_row,
                     ssd_norm_w[i].reshape(SSD_GROUPS, 1, GROUP_WIDTH))

        qk_w = jnp.concatenate([diff_qk_norm[i], diff_qk_norm[i]], axis=1)
        y_diff = _diff(u3, cos_t, sin_t, qk_w, diff_lambda[i], diff_subln[i][None, :], lam_init)

        na_w = jnp.concatenate([na_qk_norm[i], na_qk_norm[i]], axis=1)
        y_na = _na(u3, na_w, _na_bias(na_rpb[i]))

        x2 = _outproj(x2, y_ssd.reshape(b * L, SSD_WIDTH), y_diff.reshape(b * L, DIFF_WIDTH),
                      y_na.reshape(b * L, NA_WIDTH), w_out[i].astype(BF16))
    return x2.reshape(b, L, d)
```

```python
import functools
import math

import numpy as np
import jax
import jax.numpy as jnp
from jax import lax
from jax.experimental import pallas as pl
from jax.experimental.pallas import tpu as pltpu

F32 = jnp.float32
BF16 = jnp.bfloat16

D_MODEL = 1024
SEQ = 2048
GRID_W = 64
ROWS = SEQ // GRID_W
SSD_WIDTH = 1024
SSD_HEAD_DIM = 64
SSD_HEADS = 16
SSD_GROUPS = 2
SSD_STATE = 128
SSD_CONV = 5
SSD_CHUNK = 128
SSD_XBC = SSD_WIDTH + 2 * SSD_GROUPS * SSD_STATE
GROUP_HEADS = SSD_HEADS // SSD_GROUPS
GROUP_WIDTH = SSD_WIDTH // SSD_GROUPS
N_CHUNKS = SEQ // SSD_CHUNK
DIFF_WIDTH = 512
DIFF_HEAD_DIM = 64
DIFF_HEADS = 4
NA_WIDTH = 512
NA_HEAD_DIM = 64
NA_HEADS = 8
NA_KH = 8
NA_KW = 16
MIX_WIDTH = SSD_WIDTH + DIFF_WIDTH + NA_WIDTH
ROPE_THETA = 10000.0
EPS = 1e-6
LANES = 128
HALO = 8

U_MAIN = SSD_WIDTH + SSD_XBC + 4 * DIFF_WIDTH + 4 * NA_WIDTH
COL_Z = 0
COL_X = SSD_WIDTH
COL_B = COL_X + SSD_WIDTH
COL_C = COL_B + SSD_GROUPS * SSD_STATE
COL_DIFF = COL_C + SSD_GROUPS * SSD_STATE
COL_NA = COL_DIFF + 4 * DIFF_WIDTH
DT_PAD = LANES

NA_QROWS = 4
NA_BAND = 12
NA_QBLK = NA_QROWS * GRID_W
NA_KBLK = NA_BAND * GRID_W
NA_NBLK = ROWS // NA_QROWS
NEG = -1e30
LOG2E = math.log2(math.e)
ONES_ROWS = 16

VMEM_LIMIT = 56 * 1024 * 1024


def _silu(v):
    return v * (1.0 / (1.0 + jnp.exp(-v)))


def _dot(a, b):
    return jnp.dot(a, b, preferred_element_type=F32)


def _dot_nt(a, b):
    return lax.dot_general(a, b, (((1,), (1,)), ((), ())), preferred_element_type=F32)


def _dot_tn(a, b):
    return lax.dot_general(a, b, (((0,), (0,)), ((), ())), preferred_element_type=F32)


IN_TM = 1024
IN_TN = 1664


def _inproj_kernel(x_ref, nw_ref, w_ref, wdt_ref, u_ref, dt_ref, h_scr):
    @pl.when(pl.program_id(1) == 0)
    def _():
        x = x_ref[...]
        ms = jnp.mean(x * x, axis=-1, keepdims=True)
        h = (x * lax.rsqrt(ms + EPS) * nw_ref[...]).astype(BF16)
        h_scr[...] = h
        dt_ref[...] = _dot(h, wdt_ref[...])

    u_ref[...] = _dot(h_scr[...], w_ref[...]).astype(u_ref.dtype)


def _inproj(x2, norm_w, w_main, w_dt):
    m = x2.shape[0]
    return pl.pallas_call(
        _inproj_kernel,
        grid=(m // IN_TM, U_MAIN // IN_TN),
        in_specs=[
            pl.BlockSpec((IN_TM, D_MODEL), lambda i, j: (i, 0)),
            pl.BlockSpec((1, D_MODEL), lambda i, j: (0, 0)),
            pl.BlockSpec((D_MODEL, IN_TN), lambda i, j: (0, j)),
            pl.BlockSpec((D_MODEL, DT_PAD), lambda i, j: (0, 0)),
        ],
        out_specs=[
            pl.BlockSpec((IN_TM, IN_TN), lambda i, j: (i, j)),
            pl.BlockSpec((IN_TM, DT_PAD), lambda i, j: (i, 0)),
        ],
        out_shape=[
            jax.ShapeDtypeStruct((m, U_MAIN), BF16),
            jax.ShapeDtypeStruct((m, DT_PAD), F32),
        ],
        scratch_shapes=[pltpu.VMEM((IN_TM, D_MODEL), BF16)],
        compiler_params=pltpu.CompilerParams(
            dimension_semantics=("arbitrary", "arbitrary"), vmem_limit_bytes=VMEM_LIMIT),
        name="inproj",
    )(x2, norm_w, w_main, w_dt)


Q_CSF, Q_ECSB, Q_WF, Q_WB, Q_DF, Q_DB, Q_DTF, Q_DTB, Q_EF, Q_EB = range(10)
N_QUANT = 10


def _softplus(v):
    return jnp.maximum(v, 0.0) + jnp.log1p(jnp.exp(-jnp.abs(v)))


def _chunk_scan(a, lane, reverse):
    n = a.shape[-1]
    out = a
    k = 1
    while k < SSD_CHUNK:
        if reverse:
            out = out + jnp.where(lane < SSD_CHUNK - k, pltpu.roll(out, n - k, axis=1), 0.0)
        else:
            out = out + jnp.where(lane >= k, pltpu.roll(out, k, axis=1), 0.0)
        k *= 2
    return out


def _expand_heads(col_block):
    rows = col_block.shape[0]
    left = lax.broadcasted_iota(jnp.int32, (rows, LANES), 1) < SSD_HEAD_DIM
    pieces = []
    for p in range(GROUP_HEADS // 2):
        even = jnp.broadcast_to(col_block[:, 2 * p:2 * p + 1], (rows, LANES))
        odd = jnp.broadcast_to(col_block[:, 2 * p + 1:2 * p + 2], (rows, LANES))
        pieces.append(jnp.where(left, even, odd))
    return jnp.concatenate(pieces, axis=1)


def _ssd_kernel(z_ref, x_ref, b_ref, c_ref, dtf_ref, dtb_ref, cwx_ref, cwb_ref, cwc_ref,
                cbx_ref, cbb_ref, cbc_ref, hp_ref, dsk_ref, nw_ref, o_ref,
                pad_scr, xs_scr, bm_scr, cm_scr, y_scr, hm_scr, tm_scr, sf_scr, sb_scr):
    L = SEQ
    W = GROUP_WIDTH
    NS = SSD_STATE
    CW = W + 2 * NS

    zeros_halo = jnp.zeros((HALO, CW), F32)
    pad_scr[0:HALO, :] = zeros_halo
    pad_scr[HALO + L:HALO + L + HALO, :] = zeros_halo

    def stage(c, carry):
        r = pl.multiple_of(c * SSD_CHUNK, SSD_CHUNK)
        pad_scr[pl.ds(HALO + r, SSD_CHUNK), 0:W] = x_ref[0, pl.ds(r, SSD_CHUNK), :].astype(F32)
        pad_scr[pl.ds(HALO + r, SSD_CHUNK), W:W + NS] = b_ref[0, pl.ds(r, SSD_CHUNK), :].astype(F32)
        pad_scr[pl.ds(HALO + r, SSD_CHUNK), W + NS:CW] = c_ref[0, pl.ds(r, SSD_CHUNK), :].astype(F32)
        return carry

    lax.fori_loop(0, N_CHUNKS, stage, 0)

    cw = jnp.concatenate([cwx_ref[...], cwb_ref[...], cwc_ref[...]], axis=1)
    cb = jnp.concatenate([cbx_ref[...], cbb_ref[...], cbc_ref[...]], axis=1)

    def conv(c, carry):
        r = pl.multiple_of(c * SSD_CHUNK, SSD_CHUNK)
        win = pad_scr[pl.ds(r, SSD_CHUNK + 2 * HALO), :]
        acc = jnp.broadcast_to(cb, (SSD_CHUNK, CW))
        for k in range(SSD_CONV):
            off = HALO - SSD_CONV // 2 + k
            acc = acc + win[off:off + SSD_CHUNK, :] * cw[k:k + 1, :]
        act = _silu(acc)
        xs_scr[pl.ds(r, SSD_CHUNK), :] = act[:, 0:W]
        bm_scr[pl.ds(r, SSD_CHUNK), :] = act[:, W:W + NS].astype(BF16)
        cm_scr[pl.ds(r, SSD_CHUNK), :] = act[:, W + NS:CW].astype(BF16)
        return carry

    lax.fori_loop(0, N_CHUNKS, conv, 0)

    hp = hp_ref[0]
    a_f = -jnp.exp(hp[:, 0:1])
    a_b = -jnp.exp(hp[:, 1:2])
    dt_f = _softplus(dtf_ref[0] + hp[:, 2:3])
    dt_b = _softplus(dtb_ref[0] + hp[:, 3:4])
    da_f = dt_f * a_f
    da_b = dt_b * a_b
    lane = lax.broadcasted_iota(jnp.int32, (GROUP_HEADS, L), 1) % SSD_CHUNK
    cs_f = _chunk_scan(da_f, lane, False)
    rs_f = _chunk_scan(da_f, lane, True) - da_f
    cs_b = _chunk_scan(da_b, lane, False)
    ecs_b = cs_b - da_b
    rs_b = _chunk_scan(da_b, lane, True)
    quants = {
        Q_CSF: cs_f,
        Q_ECSB: ecs_b,
        Q_WF: jnp.exp(rs_f) * dt_f,
        Q_WB: jnp.exp(ecs_b) * dt_b,
        Q_DF: jnp.exp(cs_f),
        Q_DB: jnp.exp(rs_b),
        Q_DTF: dt_f,
        Q_DTB: dt_b,
        Q_EF: jnp.exp(cs_f + rs_f),
        Q_EB: jnp.exp(ecs_b + rs_b),
    }
    for qi in range(N_QUANT):
        hm_scr[qi * GROUP_HEADS:(qi + 1) * GROUP_HEADS, :] = quants[qi]
    hm_scr[N_QUANT * GROUP_HEADS:LANES, :] = jnp.zeros((LANES - N_QUANT * GROUP_HEADS, L), F32)

    def to_time_major(c, carry):
        r = pl.multiple_of(c * SSD_CHUNK, SSD_CHUNK)
        tm_scr[pl.ds(r, SSD_CHUNK), :] = hm_scr[:, pl.ds(r, SSD_CHUNK)].T
        return carry

    lax.fori_loop(0, N_CHUNKS, to_time_major, 0)

    def tm_cols(r, q):
        return tm_scr[pl.ds(r, SSD_CHUNK), q * GROUP_HEADS:(q + 1) * GROUP_HEADS]

    def chunk_decay_row(r, q):
        row = tm_scr[pl.ds(r, 8), q * GROUP_HEADS:(q + 1) * GROUP_HEADS]
        return _expand_heads(row)[0:1, :]

    sb_scr[...] = jnp.zeros((NS, W), F32)

    def bwd(i, carry):
        c = N_CHUNKS - 1 - i
        r = pl.multiple_of(c * SSD_CHUNK, SSD_CHUNK)
        xs_c = xs_scr[pl.ds(r, SSD_CHUNK), :]
        s_b = sb_scr[...]
        y_off = _dot(cm_scr[pl.ds(r, SSD_CHUNK), :], s_b.astype(BF16))
        y_scr[pl.ds(r, SSD_CHUNK), :] = y_off * _expand_heads(tm_cols(r, Q_DB))
        xw = (xs_c * _expand_heads(tm_cols(r, Q_WB))).astype(BF16)
        sb_scr[...] = s_b * chunk_decay_row(r, Q_EB) + _dot_tn(bm_scr[pl.ds(r, SSD_CHUNK), :], xw)
        return carry

    lax.fori_loop(0, N_CHUNKS, bwd, 0)

    sf_scr[...] = jnp.zeros((NS, W), F32)
    row_i = lax.broadcasted_iota(jnp.int32, (SSD_CHUNK, SSD_CHUNK), 0)
    col_i = lax.broadcasted_iota(jnp.int32, (SSD_CHUNK, SSD_CHUNK), 1)
    lower = col_i <= row_i
    upper = col_i >= row_i
    lane2 = lax.broadcasted_iota(jnp.int32, (SSD_CHUNK, LANES), 1)
    left = lane2 < SSD_HEAD_DIM
    dsk = dsk_ref[0]
    nw = nw_ref[0]

    def fwd(c, carry):
        r = pl.multiple_of(c * SSD_CHUNK, SSD_CHUNK)
        xs_c = xs_scr[pl.ds(r, SSD_CHUNK), :]
        bm_c = bm_scr[pl.ds(r, SSD_CHUNK), :]
        cm_c = cm_scr[pl.ds(r, SSD_CHUNK), :]
        g = _dot_nt(cm_c, bm_c)
        xs_b = xs_c.astype(BF16)
        csf_cols = tm_cols(r, Q_CSF)
        ecsb_cols = tm_cols(r, Q_ECSB)
        y_pairs = []
        for hp_i in range(GROUP_HEADS // 2):
            ms = []
            for h in (2 * hp_i, 2 * hp_i + 1):
                csf_row = hm_scr[Q_CSF * GROUP_HEADS + h:Q_CSF * GROUP_HEADS + h + 1, pl.ds(r, SSD_CHUNK)]
                ecsb_row = hm_scr[Q_ECSB * GROUP_HEADS + h:Q_ECSB * GROUP_HEADS + h + 1, pl.ds(r, SSD_CHUNK)]
                dtf_row = hm_scr[Q_DTF * GROUP_HEADS + h:Q_DTF * GROUP_HEADS + h + 1, pl.ds(r, SSD_CHUNK)]
                dtb_row = hm_scr[Q_DTB * GROUP_HEADS + h:Q_DTB * GROUP_HEADS + h + 1, pl.ds(r, SSD_CHUNK)]
                seg_f = jnp.where(lower, csf_cols[:, h:h + 1] - csf_row, NEG)
                seg_b = jnp.where(upper, ecsb_row - ecsb_cols[:, h:h + 1], NEG)
                p = jnp.exp(seg_f) * dtf_row + jnp.exp(seg_b) * dtb_row
                ms.append((g * p).astype(BF16))
            xp = xs_b[:, hp_i * LANES:(hp_i + 1) * LANES]
            zero = jnp.zeros_like(xp)
            rhs = jnp.concatenate([jnp.where(left, xp, zero), jnp.where(left, zero, xp)], axis=0)
            y_pairs.append(_dot(jnp.concatenate(ms, axis=1), rhs))
        y_diag = jnp.concatenate(y_pairs, axis=1)
        s_f = sf_scr[...]
        y_off = _dot(cm_c, s_f.astype(BF16)) * _expand_heads(tm_cols(r, Q_DF))
        xw = (xs_c * _expand_heads(tm_cols(r, Q_WF))).astype(BF16)
        sf_scr[...] = s_f * chunk_decay_row(r, Q_EF) + _dot_tn(bm_c, xw)
        y = y_scr[pl.ds(r, SSD_CHUNK), :] + y_diag + y_off + dsk * xs_c
        y = y * _silu(z_ref[0, pl.ds(r, SSD_CHUNK), :].astype(F32))
        y = y * lax.rsqrt(jnp.mean(y * y, axis=-1, keepdims=True) + EPS) * nw
        o_ref[0, pl.ds(r, SSD_CHUNK), :] = y.astype(o_ref.dtype)
        return carry

    lax.fori_loop(0, N_CHUNKS, fwd, 0)


def _ssd(u3, dt_t, conv_w, conv_b, head_params, dsk_row, norm_w):
    b = u3.shape[0]
    L = SEQ
    W = GROUP_WIDTH
    NS = SSD_STATE
    G = SSD_GROUPS
    xblk = COL_X // W
    bblk = COL_B // NS
    cblk = COL_C // NS
    in_specs = [
        pl.BlockSpec((1, L, W), lambda i, g: (i, 0, g)),
        pl.BlockSpec((1, L, W), lambda i, g: (i, 0, xblk + g)),
        pl.BlockSpec((1, L, NS), lambda i, g: (i, 0, bblk + g)),
        pl.BlockSpec((1, L, NS), lambda i, g: (i, 0, cblk + g)),
        pl.BlockSpec((1, GROUP_HEADS, L), lambda i, g: (i, g, 0)),
        pl.BlockSpec((1, GROUP_HEADS, L), lambda i, g: (i, G + g, 0)),
        pl.BlockSpec((SSD_CONV, W), lambda i, g: (0, g)),
        pl.BlockSpec((SSD_CONV, NS), lambda i, g: (0, SSD_WIDTH // NS + g)),
        pl.BlockSpec((SSD_CONV, NS), lambda i, g: (0, SSD_WIDTH // NS + G + g)),
        pl.BlockSpec((1, W), lambda i, g: (0, g)),
        pl.BlockSpec((1, NS), lambda i, g: (0, SSD_WIDTH // NS + g)),
        pl.BlockSpec((1, NS), lambda i, g: (0, SSD_WIDTH // NS + G + g)),
        pl.BlockSpec((1, GROUP_HEADS, 8), lambda i, g: (g, 0, 0)),
        pl.BlockSpec((1, 1, W), lambda i, g: (g, 0, 0)),
        pl.BlockSpec((1, 1, W), lambda i, g: (g, 0, 0)),
    ]
    return pl.pallas_call(
        _ssd_kernel,
        grid=(b, G),
        in_specs=in_specs,
        out_specs=pl.BlockSpec((1, L, W), lambda i, g: (i, 0, g)),
        out_shape=jax.ShapeDtypeStruct((b, L, SSD_WIDTH), BF16),
        scratch_shapes=[
            pltpu.VMEM((L + 2 * HALO, W + 2 * NS), F32),
            pltpu.VMEM((L, W), F32),
            pltpu.VMEM((L, NS), BF16),
            pltpu.VMEM((L, NS), BF16),
            pltpu.VMEM((L, W), F32),
            pltpu.VMEM((LANES, L), F32),
            pltpu.VMEM((L, LANES), F32),
            pltpu.VMEM((NS, W), F32),
            pltpu.VMEM((NS, W), F32),
        ],
        compiler_params=pltpu.CompilerParams(
            dimension_semantics=("arbitrary", "arbitrary"), vmem_limit_bytes=VMEM_LIMIT),
        name="ssd",
    )(u3, u3, u3, u3, dt_t, dt_t, conv_w, conv_w, conv_w, conv_b, conv_b, conv_b,
      head_params, dsk_row, norm_w)


def _group_mean_sq(v, seg):
    sq = v * v
    hi = sq.astype(BF16)
    lo = (sq - hi.astype(F32)).astype(BF16)
    return (_dot(hi, seg) + _dot(lo, seg)) * (1.0 / 64.0)


def _seg_matrix():
    r = lax.broadcasted_iota(jnp.int32, (LANES, LANES), 0) // 64
    c = lax.broadcasted_iota(jnp.int32, (LANES, LANES), 1) // 64
    return jnp.where(r == c, 1.0, 0.0).astype(BF16)


DIFF_TQ = 256


def _diff_kernel(lam_init, q_ref, k_ref, v_ref, g_ref, cos_ref, sin_ref, qkw_ref, lam_ref,
                 sub_ref, o_ref, q_scr, k_scr, vt_scr, sa_scr, sb_scr):
    L = SEQ
    seg = _seg_matrix()
    lane = lax.broadcasted_iota(jnp.int32, (DIFF_TQ, LANES), 1)
    first_half = (lane % 64) < 32
    comp0 = lane < 64

    def norm_rope(ref, r, w_row):
        v = ref[0, pl.ds(r, DIFF_TQ), :].astype(F32)
        v = v * lax.rsqrt(_group_mean_sq(v, seg) + EPS) * w_row
        swapped = jnp.where(first_half, pltpu.roll(v, LANES - 32, axis=1), pltpu.roll(v, 32, axis=1))
        return v * cos_ref[pl.ds(r, DIFF_TQ), :] + swapped * sin_ref[pl.ds(r, DIFF_TQ), :]

    q_scale = (DIFF_HEAD_DIM ** -0.5) * LOG2E
    vt_scr[LANES:LANES + ONES_ROWS, :] = jnp.ones((ONES_ROWS, L), BF16)

    def prep(i, carry):
        r = pl.multiple_of(i * DIFF_TQ, DIFF_TQ)
        qn = norm_rope(q_ref, r, qkw_ref[0:1, :]) * q_scale
        q_scr[pl.ds(r, DIFF_TQ), :] = qn.astype(BF16)
        kn = norm_rope(k_ref, r, qkw_ref[1:2, :])
        k_scr[0, pl.ds(r, DIFF_TQ), :] = jnp.where(comp0, kn, 0.0).astype(BF16)
        k_scr[1, pl.ds(r, DIFF_TQ), :] = jnp.where(comp0, 0.0, kn).astype(BF16)
        vt_scr[0:LANES, pl.ds(r, DIFF_TQ)] = v_ref[0, pl.ds(r, DIFF_TQ), :].astype(F32).T.astype(BF16)
        return carry

    lax.fori_loop(0, L // DIFF_TQ, prep, 0, unroll=2)

    lam = lam_ref[...]
    lam_full = (jnp.exp(jnp.sum(lam[0:1] * lam[1:2], axis=-1, keepdims=True))
                - jnp.exp(jnp.sum(lam[2:3] * lam[3:4], axis=-1, keepdims=True)) + lam_init)

    def scores(blk, s_ref):
        r = pl.multiple_of(blk * DIFF_TQ, DIFF_TQ)
        qb = q_scr[pl.ds(r, DIFF_TQ), :]
        for c in range(2):
            s_ref[c] = _dot_nt(k_scr[c], qb)

    def finish(blk, s_ref):
        r = pl.multiple_of(blk * DIFF_TQ, DIFF_TQ)
        parts = []
        for c in range(2):
            s = s_ref[c]
            e = jnp.exp2(s - jnp.max(s, axis=0, keepdims=True)).astype(BF16)
            acc = _dot(vt_scr[...], e)
            parts.append(acc[0:LANES, :] * (1.0 / acc[LANES:LANES + 1, :]))
        ot = parts[0] - lam_full * parts[1]
        o = ot.T
        o = o * lax.rsqrt(jnp.mean(o * o, axis=-1, keepdims=True) + EPS) * sub_ref[...]
        o = o * (1.0 - lam_init) * _silu(g_ref[0, pl.ds(r, DIFF_TQ), :].astype(F32))
        o_ref[0, pl.ds(r, DIFF_TQ), :] = o.astype(o_ref.dtype)

    n_blk = L // DIFF_TQ
    scores(0, sa_scr)

    def pair(i, carry):
        scores(2 * i + 1, sb_scr)
        finish(2 * i, sa_scr)
        scores(2 * i + 2, sa_scr)
        finish(2 * i + 1, sb_scr)
        return carry

    lax.fori_loop(0, n_blk // 2 - 1, pair, 0)
    scores(n_blk - 1, sb_scr)
    finish(n_blk - 2, sa_scr)
    finish(n_blk - 1, sb_scr)


def _diff(u3, cos_t, sin_t, qk_w, lam, subln_w, lam_init):
    b = u3.shape[0]
    L = SEQ
    base = COL_DIFF // LANES
    nb = DIFF_WIDTH // LANES

    def spec(part):
        return pl.BlockSpec((1, L, LANES), lambda i, h: (i, 0, base + part * nb + h))

    return pl.pallas_call(
        functools.partial(_diff_kernel, lam_init),
        grid=(b, DIFF_HEADS),
        in_specs=[
            spec(0), spec(1), spec(2), spec(3),
            pl.BlockSpec((L, LANES), lambda i, h: (0, 0)),
            pl.BlockSpec((L, LANES), lambda i, h: (0, 0)),
            pl.BlockSpec((2, LANES), lambda i, h: (0, 0)),
            pl.BlockSpec((4, DIFF_HEAD_DIM), lambda i, h: (0, 0)),
            pl.BlockSpec((1, LANES), lambda i, h: (0, 0)),
        ],
        out_specs=pl.BlockSpec((1, L, LANES), lambda i, h: (i, 0, h)),
        out_shape=jax.ShapeDtypeStruct((b, L, DIFF_WIDTH), BF16),
        scratch_shapes=[
            pltpu.VMEM((L, LANES), BF16),
            pltpu.VMEM((2, L, LANES), BF16),
            pltpu.VMEM((LANES + ONES_ROWS, L), BF16),
            pltpu.VMEM((2, L, DIFF_TQ), F32),
            pltpu.VMEM((2, L, DIFF_TQ), F32),
        ],
        compiler_params=pltpu.CompilerParams(
            dimension_semantics=("arbitrary", "arbitrary"), vmem_limit_bytes=VMEM_LIMIT),
        name="diff_attn",
    )(u3, u3, u3, u3, cos_t, sin_t, qk_w, lam, subln_w)


def _na_block_geometry(qb):
    r0 = qb * NA_QROWS
    kb = min(max(r0 - NA_KH // 2, 0), ROWS - NA_BAND)
    return r0, kb


def _na_bias_type(qb):
    return 0 if qb == 0 else (2 if qb == NA_NBLK - 1 else 1)


def _na_kernel(q_ref, k_ref, v_ref, g_ref, qkw_ref, bias_ref, o_ref, q_scr, k_scr, vt_scr, s_scr):
    L = SEQ
    seg = _seg_matrix()
    head0_blk = lax.broadcasted_iota(jnp.int32, (NA_QBLK, LANES), 1) < 64
    q_scale = (NA_HEAD_DIM ** -0.5) * LOG2E
    vt_scr[LANES:LANES + ONES_ROWS, :] = jnp.ones((ONES_ROWS, L), BF16)

    def norm(ref, r, w_row):
        v = ref[0, pl.ds(r, NA_QBLK), :].astype(F32)
        return v * lax.rsqrt(_group_mean_sq(v, seg) + EPS) * w_row

    def prep(i, carry):
        r = pl.multiple_of(i * NA_QBLK, NA_QBLK)
        q_scr[pl.ds(r, NA_QBLK), :] = (norm(q_ref, r, qkw_ref[0:1, :]) * q_scale).astype(BF16)
        kn = norm(k_ref, r, qkw_ref[1:2, :])
        k_scr[0, pl.ds(r, NA_QBLK), :] = jnp.where(head0_blk, kn, 0.0).astype(BF16)
        k_scr[1, pl.ds(r, NA_QBLK), :] = jnp.where(head0_blk, 0.0, kn).astype(BF16)
        vt_scr[0:LANES, pl.ds(r, NA_QBLK)] = v_ref[0, pl.ds(r, NA_QBLK), :].astype(F32).T.astype(BF16)
        return carry

    lax.fori_loop(0, L // NA_QBLK, prep, 0, unroll=2)

    def slices(qb):
        r0, kb = _na_block_geometry(qb)
        return (slice(r0 * GRID_W, r0 * GRID_W + NA_QBLK), slice(kb * GRID_W, kb * GRID_W + NA_KBLK))

    def scores(qb):
        qs, ks = slices(qb)
        q_blk = q_scr[qs, :]
        for h in range(2):
            s_scr[qb % 2, h] = _dot_nt(k_scr[h, ks, :], q_blk)

    def finish(qb):
        qs, ks = slices(qb)
        t = _na_bias_type(qb)
        vt = vt_scr[:, ks]
        outs = []
        for h in range(2):
            s = s_scr[qb % 2, h] + bias_ref[h, t]
            e = jnp.exp2(s - jnp.max(s, axis=0, keepdims=True)).astype(BF16)
            acc = _dot(vt, e)
            outs.append(acc[h * NA_HEAD_DIM:(h + 1) * NA_HEAD_DIM, :] * (1.0 / acc[LANES:LANES + 1, :]))
        o = jnp.concatenate(outs, axis=0).T
        o = o * _silu(g_ref[0, qs, :].astype(F32))
        o_ref[0, qs, :] = o.astype(o_ref.dtype)

    scores(0)
    for qb in range(NA_NBLK):
        if qb + 1 < NA_NBLK:
            scores(qb + 1)
        finish(qb)


def _na(u3, qk_w, bias):
    b = u3.shape[0]
    L = SEQ
    base = COL_NA // LANES
    nb = NA_WIDTH // LANES

    def spec(part):
        return pl.BlockSpec((1, L, LANES), lambda hp, i: (i, 0, base + part * nb + hp))

    return pl.pallas_call(
        _na_kernel,
        grid=(NA_HEADS // 2, b),
        in_specs=[
            spec(0), spec(1), spec(2), spec(3),
            pl.BlockSpec((2, LANES), lambda hp, i: (0, 0)),
            pl.BlockSpec((2, 3, NA_KBLK, NA_QBLK), lambda hp, i: (hp, 0, 0, 0)),
        ],
        out_specs=pl.BlockSpec((1, L, LANES), lambda hp, i: (i, 0, hp)),
        out_shape=jax.ShapeDtypeStruct((b, L, NA_WIDTH), BF16),
        scratch_shapes=[
            pltpu.VMEM((L, LANES), BF16),
            pltpu.VMEM((2, L, LANES), BF16),
            pltpu.VMEM((LANES + ONES_ROWS, L), BF16),
            pltpu.VMEM((2, 2, NA_KBLK, NA_QBLK), F32),
        ],
        compiler_params=pltpu.CompilerParams(
            dimension_semantics=("arbitrary", "arbitrary"), vmem_limit_bytes=VMEM_LIMIT),
        name="na_attn",
    )(u3, u3, u3, u3, qk_w, bias)


def _na_bias(rpb):
    c = np.arange(GRID_W)
    col_start = np.clip(c - NA_KW // 2, 0, GRID_W - NA_KW)
    col_ok = (c[None, :] >= col_start[:, None]) & (c[None, :] < col_start[:, None] + NA_KW)
    dc_i = np.clip(c[None, :] - c[:, None] + (NA_KW - 1), 0, 2 * NA_KW - 2)
    tiles = jnp.where(col_ok.T[None, None], rpb.astype(F32)[:, :, dc_i.T] * LOG2E, NEG)
    blocks = []
    for qb in (0, 1, NA_NBLK - 1):
        r0, kb = _na_block_geometry(qb)
        r = r0 + np.arange(NA_QROWS)
        rs = np.clip(r - NA_KH // 2, 0, ROWS - NA_KH)
        rk = kb + np.arange(NA_BAND)
        valid = (rk[:, None] >= rs[None, :]) & (rk[:, None] < rs[None, :] + NA_KH)
        dr = np.clip(rk[:, None] - r[None, :] + (NA_KH - 1), 0, 2 * NA_KH - 2)
        blk = jnp.where(valid[None, :, :, None, None], tiles[:, dr], NEG)
        blocks.append(blk.transpose(0, 1, 3, 2, 4).reshape(NA_HEADS, NA_KBLK, NA_QBLK))
    return jnp.stack(blocks, axis=1)


OUT_TM = 512


def _outproj_kernel(x_ref, ys_ref, yd_ref, yn_ref, w_ref, o_ref):
    acc = _dot(ys_ref[...], w_ref[0:SSD_WIDTH, :])
    acc = acc + _dot(yd_ref[...], w_ref[SSD_WIDTH:SSD_WIDTH + DIFF_WIDTH, :])
    acc = acc + _dot(yn_ref[...], w_ref[SSD_WIDTH + DIFF_WIDTH:MIX_WIDTH, :])
    o_ref[...] = x_ref[...] + acc


def _outproj(x2, y_ssd, y_diff, y_na, w_out):
    m = x2.shape[0]
    return pl.pallas_call(
        _outproj_kernel,
        grid=(m // OUT_TM,),
        in_specs=[
            pl.BlockSpec((OUT_TM, D_MODEL), lambda i: (i, 0)),
            pl.BlockSpec((OUT_TM, SSD_WIDTH), lambda i: (i, 0)),
            pl.BlockSpec((OUT_TM, DIFF_WIDTH), lambda i: (i, 0)),
            pl.BlockSpec((OUT_TM, NA_WIDTH), lambda i: (i, 0)),
            pl.BlockSpec((MIX_WIDTH, D_MODEL), lambda i: (0, 0)),
        ],
        out_specs=pl.BlockSpec((OUT_TM, D_MODEL), lambda i: (i, 0)),
        out_shape=jax.ShapeDtypeStruct((m, D_MODEL), F32),
        compiler_params=pltpu.CompilerParams(
            dimension_semantics=("arbitrary",), vmem_limit_bytes=VMEM_LIMIT),
        name="outproj",
    )(x2, y_ssd, y_diff, y_na, w_out)


def _rope_tables():
    inv_freq = ROPE_THETA ** (-jnp.arange(0, DIFF_HEAD_DIM, 2, dtype=F32) / DIFF_HEAD_DIM)
    ang = jnp.arange(SEQ, dtype=F32)[:, None] * inv_freq[None, :]
    cos, sin = jnp.cos(ang), jnp.sin(ang)
    cos_t = jnp.concatenate([cos, cos, cos, cos], axis=1)
    sin_t = jnp.concatenate([-sin, sin, -sin, sin], axis=1)
    return cos_t, sin_t


def kernel(x, norm_w, w_in, conv_w, conv_b, a_log, dt_bias, d_skip, ssd_norm_w, diff_qk_norm,
           diff_lambda, diff_subln, na_qk_norm, na_rpb, w_out):
    b, L, d = x.shape
    assert (L, d) == (SEQ, D_MODEL)
    depth = w_in.shape[0]
    cos_t, sin_t = _rope_tables()
    dt_lo = SSD_WIDTH + SSD_XBC
    dt_hi = dt_lo + 2 * SSD_HEADS
    x2 = x.reshape(b * L, d)
    for i in range(depth):
        lam_init = 0.8 - 0.6 * math.exp(-0.3 * i)
        w_main = jnp.concatenate([w_in[i][:, :dt_lo], w_in[i][:, dt_hi:]], axis=1).astype(BF16)
        w_dt = jnp.pad(w_in[i][:, dt_lo:dt_hi], ((0, 0), (0, DT_PAD - 2 * SSD_HEADS))).astype(BF16)
        u, dt = _inproj(x2, norm_w[i][None, :], w_main, w_dt)
        u3 = u.reshape(b, L, U_MAIN)
        dt_t = jnp.swapaxes(dt.reshape(b, L, DT_PAD)[:, :, :2 * SSD_HEADS], 1, 2)

        hp = jnp.stack([a_log[i][0], a_log[i][1], dt_bias[i][0], dt_bias[i][1]], axis=-1)
        hp = jnp.pad(hp, ((0, 0), (0, 4))).reshape(SSD_GROUPS, GROUP_HEADS, 8)
        dsk_row = jnp.repeat(d_skip[i][0] + d_skip[i][1], SSD_HEAD_DIM).reshape(SSD_GROUPS, 1, GROUP_WIDTH)
        y_ssd = _ssd(u3, dt_t, conv_w[i], conv_b[i][None, :], hp, dsk_row,
                     ssd_norm_w[i].reshape(SSD_GROUPS, 1, GROUP_WIDTH))

        qk_w = jnp.concatenate([diff_qk_norm[i], diff_qk_norm[i]], axis=1)
        y_diff = _diff(u3, cos_t, sin_t, qk_w, diff_lambda[i], diff_subln[i][None, :], lam_init)

        na_w = jnp.concatenate([na_qk_norm[i], na_qk_norm[i]], axis=1)
        y_na = _na(u3, na_w, _na_bias(na_rpb[i]))

        x2 = _outproj(x2, y_ssd.reshape(b * L, SSD_WIDTH), y_diff.reshape(b * L, DIFF_WIDTH),
                      y_na.reshape(b * L, NA_WIDTH), w_out[i].astype(BF16))
    return x2.reshape(b, L, d)
```

```python
import functools
import math

import numpy as np
import jax
import jax.numpy as jnp
from jax import lax
from jax.experimental import pallas as pl
from jax.experimental.pallas import tpu as pltpu

F32 = jnp.float32
BF16 = jnp.bfloat16

D_MODEL = 1024
SEQ = 2048
GRID_W = 64
ROWS = SEQ // GRID_W
SSD_WIDTH = 1024
SSD_HEAD_DIM = 64
SSD_HEADS = 16
SSD_GROUPS = 2
SSD_STATE = 128
SSD_CONV = 5
SSD_CHUNK = 128
SSD_XBC = SSD_WIDTH + 2 * SSD_GROUPS * SSD_STATE
GROUP_HEADS = SSD_HEADS // SSD_GROUPS
GROUP_WIDTH = SSD_WIDTH // SSD_GROUPS
N_CHUNKS = SEQ // SSD_CHUNK
DIFF_WIDTH = 512
DIFF_HEAD_DIM = 64
DIFF_HEADS = 4
NA_WIDTH = 512
NA_HEAD_DIM = 64
NA_HEADS = 8
NA_KH = 8
NA_KW = 16
MIX_WIDTH = SSD_WIDTH + DIFF_WIDTH + NA_WIDTH
ROPE_THETA = 10000.0
EPS = 1e-6
LANES = 128
CONV_HALO = 16
CONV_WIN = SSD_CHUNK + 2 * CONV_HALO

U_MAIN = SSD_WIDTH + SSD_XBC + 4 * DIFF_WIDTH + 4 * NA_WIDTH
COL_Z = 0
COL_X = SSD_WIDTH
COL_B = COL_X + SSD_WIDTH
COL_C = COL_B + SSD_GROUPS * SSD_STATE
COL_DIFF = COL_C + SSD_GROUPS * SSD_STATE
COL_NA = COL_DIFF + 4 * DIFF_WIDTH
DT_PAD = LANES

NA_QROWS = 4
NA_BAND = 12
NA_QBLK = NA_QROWS * GRID_W
NA_KBLK = NA_BAND * GRID_W
NA_NBLK = ROWS // NA_QROWS
NEG = -1e30
LOG2E = math.log2(math.e)
ONES_ROWS = 16

VMEM_LIMIT = 56 * 1024 * 1024


def _silu(v):
    return v * (1.0 / (1.0 + jnp.exp(-v)))


def _dot(a, b):
    return jnp.dot(a, b, preferred_element_type=F32)


def _dot_nt(a, b):
    return lax.dot_general(a, b, (((1,), (1,)), ((), ())), preferred_element_type=F32)


def _dot_tn(a, b):
    return lax.dot_general(a, b, (((0,), (0,)), ((), ())), preferred_element_type=F32)


IN_TM = 1024
IN_TN = 1664


def _inproj_kernel(x_ref, nw_ref, w_ref, wdt_ref, u_ref, dt_ref, h_scr):
    @pl.when(pl.program_id(1) == 0)
    def _():
        x = x_ref[...]
        ms = jnp.mean(x * x, axis=-1, keepdims=True)
        h = (x * lax.rsqrt(ms + EPS) * nw_ref[...]).astype(BF16)
        h_scr[...] = h
        dt_ref[...] = _dot(h, wdt_ref[...])

    u_ref[...] = _dot(h_scr[...], w_ref[...]).astype(u_ref.dtype)


def _inproj(x2, norm_w, w_main, w_dt):
    m = x2.shape[0]
    return pl.pallas_call(
        _inproj_kernel,
        grid=(m // IN_TM, U_MAIN // IN_TN),
        in_specs=[
            pl.BlockSpec((IN_TM, D_MODEL), lambda i, j: (i, 0)),
            pl.BlockSpec((1, D_MODEL), lambda i, j: (0, 0)),
            pl.BlockSpec((D_MODEL, IN_TN), lambda i, j: (0, j)),
            pl.BlockSpec((D_MODEL, DT_PAD), lambda i, j: (0, 0)),
        ],
        out_specs=[
            pl.BlockSpec((IN_TM, IN_TN), lambda i, j: (i, j)),
            pl.BlockSpec((IN_TM, DT_PAD), lambda i, j: (i, 0)),
        ],
        out_shape=[
            jax.ShapeDtypeStruct((m, U_MAIN), BF16),
            jax.ShapeDtypeStruct((m, DT_PAD), F32),
        ],
        scratch_shapes=[pltpu.VMEM((IN_TM, D_MODEL), BF16)],
        compiler_params=pltpu.CompilerParams(
            dimension_semantics=("arbitrary", "arbitrary"), vmem_limit_bytes=VMEM_LIMIT),
        name="inproj",
    )(x2, norm_w, w_main, w_dt)


R_WF, R_DF = 0, 1
R_WB, R_DB = 4, 5
R_COLF, R_COLB = 8, 9
R_EF, R_EB = 10, 11
R_ROWF, R_ROWB = 14, 15


def _softplus(v):
    return jnp.maximum(v, 0.0) + jnp.log1p(jnp.exp(-jnp.abs(v)))


def _chunk_scan(a, lane, reverse):
    n = a.shape[-1]
    out = a
    k = 1
    while k < SSD_CHUNK:
        if reverse:
            out = out + jnp.where(lane < SSD_CHUNK - k, pltpu.roll(out, n - k, axis=1), 0.0)
        else:
            out = out + jnp.where(lane >= k, pltpu.roll(out, k, axis=1), 0.0)
        k *= 2
    return out


def _ssd_kernel(z_ref, x_ref, b_ref, c_ref, dtf_ref, dtb_ref, cwx_ref, cwb_ref, cwc_ref,
                cbx_ref, cbb_ref, cbc_ref, hp_ref, dsk_ref, nw_ref, o_ref,
                shift_scr, xs_scr, bt_scr, cm_scr, y_scr, hm_scr, tm_scr, dec_scr, sf_scr, sb_scr):
    L = SEQ
    W = GROUP_WIDTH
    NS = SSD_STATE

    tap_t = lax.broadcasted_iota(jnp.int32, (SSD_CONV * SSD_CHUNK, CONV_WIN), 0)
    tap_j = lax.broadcasted_iota(jnp.int32, (SSD_CONV * SSD_CHUNK, CONV_WIN), 1)
    tap_src = tap_t % SSD_CHUNK + tap_t // SSD_CHUNK - SSD_CONV // 2
    for variant in range(3):
        shift_scr[variant] = jnp.where(tap_j == tap_src + variant * CONV_HALO, 1.0, 0.0).astype(BF16)

    def conv(c, carry):
        r = pl.multiple_of(c * SSD_CHUNK, SSD_CHUNK)
        w0 = pl.multiple_of(jnp.clip(r - CONV_HALO, 0, L - CONV_WIN), CONV_HALO)
        variant = jnp.where(c == 0, 0, jnp.where(c == N_CHUNKS - 1, 2, 1))
        shift = shift_scr[variant]
        rows = pl.ds(r, SSD_CHUNK)

        def taps(win, cw, cb):
            sh = _dot(shift, win)
            acc = cb + sh[0:SSD_CHUNK, :] * cw[0:1, :]
            for k in range(1, SSD_CONV):
                acc = acc + sh[k * SSD_CHUNK:(k + 1) * SSD_CHUNK, :] * cw[k:k + 1, :]
            return _silu(acc)

        xs_scr[rows, :] = taps(x_ref[0, pl.ds(w0, CONV_WIN), :], cwx_ref[...], cbx_ref[...])
        win_bc = jnp.concatenate([b_ref[0, pl.ds(w0, CONV_WIN), :], c_ref[0, pl.ds(w0, CONV_WIN), :]], axis=1)
        act_bc = taps(win_bc, jnp.concatenate([cwb_ref[...], cwc_ref[...]], axis=1),
                      jnp.concatenate([cbb_ref[...], cbc_ref[...]], axis=1))
        bt_scr[:, rows] = act_bc[:, 0:NS].T.astype(BF16)
        cm_scr[rows, :] = act_bc[:, NS:2 * NS].astype(BF16)
        return carry

    lax.fori_loop(0, N_CHUNKS, conv, 0, unroll=2)

    hp = hp_ref[0]
    a_f = -jnp.exp(hp[:, 0:1])
    a_b = -jnp.exp(hp[:, 1:2])
    dt_f = _softplus(dtf_ref[0] + hp[:, 2:3])
    dt_b = _softplus(dtb_ref[0] + hp[:, 3:4])
    da_f = dt_f * a_f
    da_b = dt_b * a_b
    lane = lax.broadcasted_iota(jnp.int32, (GROUP_HEADS, L), 1) % SSD_CHUNK
    cs_f = _chunk_scan(da_f, lane, False)
    rs_f = _chunk_scan(da_f, lane, True) - da_f
    cs_b = _chunk_scan(da_b, lane, False)
    ecs_b = cs_b - da_b
    rs_b = _chunk_scan(da_b, lane, True)

    def put(row, v):
        hm_scr[row * GROUP_HEADS:(row + 1) * GROUP_HEADS, :] = v

    def put_split(row_hi, v):
        hi = v.astype(BF16).astype(F32)
        put(row_hi, hi)
        put(row_hi + 2, (v - hi).astype(BF16).astype(F32))

    put_split(R_WF, jnp.exp(rs_f) * dt_f)
    put_split(R_DF, jnp.exp(cs_f))
    put_split(R_WB, jnp.exp(ecs_b) * dt_b)
    put_split(R_DB, jnp.exp(rs_b))
    put_split(R_EF, jnp.exp(cs_f + rs_f))
    put_split(R_EB, jnp.exp(ecs_b + rs_b))
    put(R_COLF, cs_f * LOG2E)
    put(R_COLB, ecs_b * LOG2E)
    put(R_ROWF, cs_f * LOG2E - jnp.log2(dt_f))
    put(R_ROWB, ecs_b * LOG2E + jnp.log2(dt_b))

    def to_time_major(c, carry):
        r = pl.multiple_of(c * SSD_CHUNK, SSD_CHUNK)
        tm_scr[pl.ds(r, SSD_CHUNK), :] = hm_scr[:, pl.ds(r, SSD_CHUNK)].T
        return carry

    lax.fori_loop(0, N_CHUNKS, to_time_major, 0)

    e_row = lax.broadcasted_iota(jnp.int32, (4 * GROUP_HEADS, 2 * W), 0) % (2 * GROUP_HEADS)
    e_col = lax.broadcasted_iota(jnp.int32, (4 * GROUP_HEADS, 2 * W), 1) // SSD_HEAD_DIM
    spread = jnp.where(e_row == e_col, 1.0, 0.0).astype(BF16)

    def expand(rows, first_row):
        cols = slice(first_row * GROUP_HEADS, (first_row + 4) * GROUP_HEADS)
        return _dot(tm_scr[rows, :][:, cols].astype(BF16), spread)

    dec_scr[...] = expand(pl.ds(0, N_CHUNKS, stride=SSD_CHUNK), R_EF)

    row_i = lax.broadcasted_iota(jnp.int32, (SSD_CHUNK, SSD_CHUNK), 0)
    col_i = lax.broadcasted_iota(jnp.int32, (SSD_CHUNK, SSD_CHUNK), 1)
    lower = col_i <= row_i
    upper = col_i >= row_i
    left = lax.broadcasted_iota(jnp.int32, (SSD_CHUNK, LANES), 1) < SSD_HEAD_DIM
    dsk = dsk_ref[0]
    nw = nw_ref[0]

    def hm_row(row, h, r):
        return hm_scr[row * GROUP_HEADS + h:row * GROUP_HEADS + h + 1, pl.ds(r, SSD_CHUNK)]

    def forward_part(c):
        r = pl.multiple_of(c * SSD_CHUNK, SSD_CHUNK)
        rows = pl.ds(r, SSD_CHUNK)
        xs_c = xs_scr[rows, :]
        bt_c = bt_scr[:, rows]
        cm_c = cm_scr[rows, :]
        g = _dot(cm_c, bt_c)
        xs_b = xs_c.astype(BF16)
        colf = tm_scr[rows, R_COLF * GROUP_HEADS:(R_COLF + 1) * GROUP_HEADS]
        colb = tm_scr[rows, R_COLB * GROUP_HEADS:(R_COLB + 1) * GROUP_HEADS]
        y_pairs = []
        for hp_i in range(GROUP_HEADS // 2):
            ms = []
            for h in (2 * hp_i, 2 * hp_i + 1):
                seg_f = jnp.where(lower, colf[:, h:h + 1] - hm_row(R_ROWF, h, r), NEG)
                seg_b = jnp.where(upper, hm_row(R_ROWB, h, r) - colb[:, h:h + 1], NEG)
                ms.append((g * (jnp.exp2(seg_f) + jnp.exp2(seg_b))).astype(BF16))
            xp = xs_b[:, hp_i * LANES:(hp_i + 1) * LANES]
            zero = jnp.zeros_like(xp)
            rhs = jnp.concatenate([jnp.where(left, xp, zero), jnp.where(left, zero, xp)], axis=0)
            y_pairs.append(_dot(jnp.concatenate(ms, axis=1), rhs))
        y_diag = jnp.concatenate(y_pairs, axis=1)
        ex = expand(rows, R_WF)
        s_f = sf_scr[...]
        y_off = _dot(cm_c, s_f.astype(BF16)) * ex[:, W:2 * W]
        xw = (xs_c * ex[:, 0:W]).astype(BF16)
        sf_scr[...] = s_f * dec_scr[pl.ds(c, 1), 0:W] + _dot(bt_c, xw)
        return y_diag + y_off

    def backward_part(c):
        r = pl.multiple_of(c * SSD_CHUNK, SSD_CHUNK)
        rows = pl.ds(r, SSD_CHUNK)
        ex = expand(rows, R_WB)
        s_b = sb_scr[...]
        y_off = _dot(cm_scr[rows, :], s_b.astype(BF16)) * ex[:, W:2 * W]
        xw = (xs_scr[rows, :] * ex[:, 0:W]).astype(BF16)
        sb_scr[...] = s_b * dec_scr[pl.ds(c, 1), W:2 * W] + _dot(bt_scr[:, rows], xw)
        return y_off

    def finalize(c, y):
        r = pl.multiple_of(c * SSD_CHUNK, SSD_CHUNK)
        rows = pl.ds(r, SSD_CHUNK)
        y = y + y_scr[rows, :] + dsk * xs_scr[rows, :]
        y = y * _silu(z_ref[0, rows, :].astype(F32))
        y = y * lax.rsqrt(jnp.mean(y * y, axis=-1, keepdims=True) + EPS) * nw
        o_ref[0, rows, :] = y.astype(o_ref.dtype)

    sf_scr[...] = jnp.zeros((NS, W), F32)
    sb_scr[...] = jnp.zeros((NS, W), F32)
    half = N_CHUNKS // 2

    def first_half(i, carry):
        cb = N_CHUNKS - 1 - i
        y_scr[pl.ds(pl.multiple_of(i * SSD_CHUNK, SSD_CHUNK), SSD_CHUNK), :] = forward_part(i)
        y_scr[pl.ds(pl.multiple_of(cb * SSD_CHUNK, SSD_CHUNK), SSD_CHUNK), :] = backward_part(cb)
        return carry

    def second_half(i, carry):
        cb = N_CHUNKS - 1 - i
        finalize(i, forward_part(i))
        finalize(cb, backward_part(cb))
        return carry

    lax.fori_loop(0, half, first_half, 0, unroll=2)
    lax.fori_loop(half, N_CHUNKS, second_half, 0, unroll=2)


def _ssd(u3, dt_t, conv_w, conv_b, head_params, dsk_row, norm_w):
    b = u3.shape[0]
    L = SEQ
    W = GROUP_WIDTH
    NS = SSD_STATE
    G = SSD_GROUPS
    xblk = COL_X // W
    bblk = COL_B // NS
    cblk = COL_C // NS
    in_specs = [
        pl.BlockSpec((1, L, W), lambda i, g: (i, 0, g)),
        pl.BlockSpec((1, L, W), lambda i, g: (i, 0, xblk + g)),
        pl.BlockSpec((1, L, NS), lambda i, g: (i, 0, bblk + g)),
        pl.BlockSpec((1, L, NS), lambda i, g: (i, 0, cblk + g)),
        pl.BlockSpec((1, GROUP_HEADS, L), lambda i, g: (i, g, 0)),
        pl.BlockSpec((1, GROUP_HEADS, L), lambda i, g: (i, G + g, 0)),
        pl.BlockSpec((SSD_CONV, W), lambda i, g: (0, g)),
        pl.BlockSpec((SSD_CONV, NS), lambda i, g: (0, SSD_WIDTH // NS + g)),
        pl.BlockSpec((SSD_CONV, NS), lambda i, g: (0, SSD_WIDTH // NS + G + g)),
        pl.BlockSpec((1, W), lambda i, g: (0, g)),
        pl.BlockSpec((1, NS), lambda i, g: (0, SSD_WIDTH // NS + g)),
        pl.BlockSpec((1, NS), lambda i, g: (0, SSD_WIDTH // NS + G + g)),
        pl.BlockSpec((1, GROUP_HEADS, 8), lambda i, g: (g, 0, 0)),
        pl.BlockSpec((1, 1, W), lambda i, g: (g, 0, 0)),
        pl.BlockSpec((1, 1, W), lambda i, g: (g, 0, 0)),
    ]
    return pl.pallas_call(
        _ssd_kernel,
        grid=(b, G),
        in_specs=in_specs,
        out_specs=pl.BlockSpec((1, L, W), lambda i, g: (i, 0, g)),
        out_shape=jax.ShapeDtypeStruct((b, L, SSD_WIDTH), BF16),
        scratch_shapes=[
            pltpu.VMEM((3, SSD_CONV * SSD_CHUNK, CONV_WIN), BF16),
            pltpu.VMEM((L, W), F32),
            pltpu.VMEM((NS, L), BF16),
            pltpu.VMEM((L, NS), BF16),
            pltpu.VMEM((L, W), F32),
            pltpu.VMEM((LANES, L), F32),
            pltpu.VMEM((L, LANES), F32),
            pltpu.VMEM((N_CHUNKS, 2 * W), F32),
            pltpu.VMEM((NS, W), F32),
            pltpu.VMEM((NS, W), F32),
        ],
        compiler_params=pltpu.CompilerParams(
            dimension_semantics=("arbitrary", "arbitrary"), vmem_limit_bytes=VMEM_LIMIT),
        name="ssd",
    )(u3, u3, u3, u3, dt_t, dt_t, conv_w, conv_w, conv_w, conv_b, conv_b, conv_b,
      head_params, dsk_row, norm_w)


def _group_mean_sq(v, seg):
    sq = v * v
    hi = sq.astype(BF16)
    lo = (sq - hi.astype(F32)).astype(BF16)
    return (_dot(hi, seg) + _dot(lo, seg)) * (1.0 / 64.0)


def _seg_matrix():
    r = lax.broadcasted_iota(jnp.int32, (LANES, LANES), 0) // 64
    c = lax.broadcasted_iota(jnp.int32, (LANES, LANES), 1) // 64
    return jnp.where(r == c, 1.0, 0.0).astype(BF16)


DIFF_TQ = 256


def _diff_kernel(lam_init, q_ref, k_ref, v_ref, g_ref, cos_ref, sin_ref, qkw_ref, lam_ref,
                 sub_ref, o_ref, q_scr, k_scr, vt_scr, sa_scr, sb_scr):
    L = SEQ
    seg = _seg_matrix()
    lane = lax.broadcasted_iota(jnp.int32, (DIFF_TQ, LANES), 1)
    first_half = (lane % 64) < 32
    comp0 = lane < 64

    def norm_rope(ref, r, w_row):
        v = ref[0, pl.ds(r, DIFF_TQ), :].astype(F32)
        v = v * lax.rsqrt(_group_mean_sq(v, seg) + EPS) * w_row
        swapped = jnp.where(first_half, pltpu.roll(v, LANES - 32, axis=1), pltpu.roll(v, 32, axis=1))
        return v * cos_ref[pl.ds(r, DIFF_TQ), :] + swapped * sin_ref[pl.ds(r, DIFF_TQ), :]

    q_scale = (DIFF_HEAD_DIM ** -0.5) * LOG2E
    vt_scr[LANES:LANES + ONES_ROWS, :] = jnp.ones((ONES_ROWS, L), BF16)

    def prep(i, carry):
        r = pl.multiple_of(i * DIFF_TQ, DIFF_TQ)
        qn = norm_rope(q_ref, r, qkw_ref[0:1, :]) * q_scale
        q_scr[pl.ds(r, DIFF_TQ), :] = qn.astype(BF16)
        kn = norm_rope(k_ref, r, qkw_ref[1:2, :])
        k_scr[0, pl.ds(r, DIFF_TQ), :] = jnp.where(comp0, kn, 0.0).astype(BF16)
        k_scr[1, pl.ds(r, DIFF_TQ), :] = jnp.where(comp0, 0.0, kn).astype(BF16)
        vt_scr[0:LANES, pl.ds(r, DIFF_TQ)] = v_ref[0, pl.ds(r, DIFF_TQ), :].astype(F32).T.astype(BF16)
        return carry

    lax.fori_loop(0, L // DIFF_TQ, prep, 0, unroll=2)

    lam = lam_ref[...]
    lam_full = (jnp.exp(jnp.sum(lam[0:1] * lam[1:2], axis=-1, keepdims=True))
                - jnp.exp(jnp.sum(lam[2:3] * lam[3:4], axis=-1, keepdims=True)) + lam_init)

    def scores(blk, s_ref):
        r = pl.multiple_of(blk * DIFF_TQ, DIFF_TQ)
        qb = q_scr[pl.ds(r, DIFF_TQ), :]
        for c in range(2):
            s_ref[c] = _dot_nt(k_scr[c], qb)

    def finish(blk, s_ref):
        r = pl.multiple_of(blk * DIFF_TQ, DIFF_TQ)
        parts = []
        for c in range(2):
            s = s_ref[c]
            e = jnp.exp2(s - jnp.max(s, axis=0, keepdims=True)).astype(BF16)
            acc = _dot(vt_scr[...], e)
            parts.append(acc[0:LANES, :] * (1.0 / acc[LANES:LANES + 1, :]))
        ot = parts[0] - lam_full * parts[1]
        o = ot.T
        o = o * lax.rsqrt(jnp.mean(o * o, axis=-1, keepdims=True) + EPS) * sub_ref[...]
        o = o * (1.0 - lam_init) * _silu(g_ref[0, pl.ds(r, DIFF_TQ), :].astype(F32))
        o_ref[0, pl.ds(r, DIFF_TQ), :] = o.astype(o_ref.dtype)

    n_blk = L // DIFF_TQ
    scores(0, sa_scr)

    def pair(i, carry):
        scores(2 * i + 1, sb_scr)
        finish(2 * i, sa_scr)
        scores(2 * i + 2, sa_scr)
        finish(2 * i + 1, sb_scr)
        return carry

    lax.fori_loop(0, n_blk // 2 - 1, pair, 0)
    scores(n_blk - 1, sb_scr)
    finish(n_blk - 2, sa_scr)
    finish(n_blk - 1, sb_scr)


def _diff(u3, cos_t, sin_t, qk_w, lam, subln_w, lam_init):
    b = u3.shape[0]
    L = SEQ
    base = COL_DIFF // LANES
    nb = DIFF_WIDTH // LANES

    def spec(part):
        return pl.BlockSpec((1, L, LANES), lambda i, h: (i, 0, base + part * nb + h))

    return pl.pallas_call(
        functools.partial(_diff_kernel, lam_init),
        grid=(b, DIFF_HEADS),
        in_specs=[
            spec(0), spec(1), spec(2), spec(3),
            pl.BlockSpec((L, LANES), lambda i, h: (0, 0)),
            pl.BlockSpec((L, LANES), lambda i, h: (0, 0)),
            pl.BlockSpec((2, LANES), lambda i, h: (0, 0)),
            pl.BlockSpec((4, DIFF_HEAD_DIM), lambda i, h: (0, 0)),
            pl.BlockSpec((1, LANES), lambda i, h: (0, 0)),
        ],
        out_specs=pl.BlockSpec((1, L, LANES), lambda i, h: (i, 0, h)),
        out_shape=jax.ShapeDtypeStruct((b, L, DIFF_WIDTH), BF16),
        scratch_shapes=[
            pltpu.VMEM((L, LANES), BF16),
            pltpu.VMEM((2, L, LANES), BF16),
            pltpu.VMEM((LANES + ONES_ROWS, L), BF16),
            pltpu.VMEM((2, L, DIFF_TQ), F32),
            pltpu.VMEM((2, L, DIFF_TQ), F32),
        ],
        compiler_params=pltpu.CompilerParams(
            dimension_semantics=("arbitrary", "arbitrary"), vmem_limit_bytes=VMEM_LIMIT),
        name="diff_attn",
    )(u3, u3, u3, u3, cos_t, sin_t, qk_w, lam, subln_w)


def _na_block_geometry(qb):
    r0 = qb * NA_QROWS
    kb = min(max(r0 - NA_KH // 2, 0), ROWS - NA_BAND)
    return r0, kb


def _na_bias_type(qb):
    return 0 if qb == 0 else (2 if qb == NA_NBLK - 1 else 1)


def _na_kernel(q_ref, k_ref, v_ref, g_ref, qkw_ref, bias_ref, o_ref, q_scr, k_scr, vt_scr, s_scr):
    L = SEQ
    seg = _seg_matrix()
    head0_blk = lax.broadcasted_iota(jnp.int32, (NA_QBLK, LANES), 1) < 64
    q_scale = (NA_HEAD_DIM ** -0.5) * LOG2E
    vt_scr[LANES:LANES + ONES_ROWS, :] = jnp.ones((ONES_ROWS, L), BF16)

    def norm(ref, r, w_row):
        v = ref[0, pl.ds(r, NA_QBLK), :].astype(F32)
        return v * lax.rsqrt(_group_mean_sq(v, seg) + EPS) * w_row

    def prep(i, carry):
        r = pl.multiple_of(i * NA_QBLK, NA_QBLK)
        q_scr[pl.ds(r, NA_QBLK), :] = (norm(q_ref, r, qkw_ref[0:1, :]) * q_scale).astype(BF16)
        kn = norm(k_ref, r, qkw_ref[1:2, :])
        k_scr[0, pl.ds(r, NA_QBLK), :] = jnp.where(head0_blk, kn, 0.0).astype(BF16)
        k_scr[1, pl.ds(r, NA_QBLK), :] = jnp.where(head0_blk, 0.0, kn).astype(BF16)
        vt_scr[0:LANES, pl.ds(r, NA_QBLK)] = v_ref[0, pl.ds(r, NA_QBLK), :].astype(F32).T.astype(BF16)
        return carry

    lax.fori_loop(0, L // NA_QBLK, prep, 0, unroll=2)

    def slices(qb):
        r0, kb = _na_block_geometry(qb)
        return (slice(r0 * GRID_W, r0 * GRID_W + NA_QBLK), slice(kb * GRID_W, kb * GRID_W + NA_KBLK))

    def scores(qb):
        qs, ks = slices(qb)
        q_blk = q_scr[qs, :]
        for h in range(2):
            s_scr[qb % 2, h] = _dot_nt(k_scr[h, ks, :], q_blk)

    def finish(qb):
        qs, ks = slices(qb)
        t = _na_bias_type(qb)
        vt = vt_scr[:, ks]
        outs = []
        for h in range(2):
            s = s_scr[qb % 2, h] + bias_ref[h, t]
            e = jnp.exp2(s - jnp.max(s, axis=0, keepdims=True)).astype(BF16)
            acc = _dot(vt, e)
            outs.append(acc[h * NA_HEAD_DIM:(h + 1) * NA_HEAD_DIM, :] * (1.0 / acc[LANES:LANES + 1, :]))
        o = jnp.concatenate(outs, axis=0).T
        o = o * _silu(g_ref[0, qs, :].astype(F32))
        o_ref[0, qs, :] = o.astype(o_ref.dtype)

    scores(0)
    for qb in range(NA_NBLK):
        if qb + 1 < NA_NBLK:
            scores(qb + 1)
        finish(qb)


def _na(u3, qk_w, bias):
    b = u3.shape[0]
    L = SEQ
    base = COL_NA // LANES
    nb = NA_WIDTH // LANES

    def spec(part):
        return pl.BlockSpec((1, L, LANES), lambda hp, i: (i, 0, base + part * nb + hp))

    return pl.pallas_call(
        _na_kernel,
        grid=(NA_HEADS // 2, b),
        in_specs=[
            spec(0), spec(1), spec(2), spec(3),
            pl.BlockSpec((2, LANES), lambda hp, i: (0, 0)),
            pl.BlockSpec((2, 3, NA_KBLK, NA_QBLK), lambda hp, i: (hp, 0, 0, 0)),
        ],
        out_specs=pl.BlockSpec((1, L, LANES), lambda hp, i: (i, 0, hp)),
        out_shape=jax.ShapeDtypeStruct((b, L, NA_WIDTH), BF16),
        scratch_shapes=[
            pltpu.VMEM((L, LANES), BF16),
            pltpu.VMEM((2, L, LANES), BF16),
            pltpu.VMEM((LANES + ONES_ROWS, L), BF16),
            pltpu.VMEM((2, 2, NA_KBLK, NA_QBLK), F32),
        ],
        compiler_params=pltpu.CompilerParams(
            dimension_semantics=("arbitrary", "arbitrary"), vmem_limit_bytes=VMEM_LIMIT),
        name="na_attn",
    )(u3, u3, u3, u3, qk_w, bias)


def _na_bias(rpb):
    c = np.arange(GRID_W)
    col_start = np.clip(c - NA_KW // 2, 0, GRID_W - NA_KW)
    col_ok = (c[None, :] >= col_start[:, None]) & (c[None, :] < col_start[:, None] + NA_KW)
    lead = rpb.shape[:-2]
    pad = GRID_W - NA_KW
    rev = jnp.pad(rpb.astype(F32)[..., ::-1] * LOG2E, [(0, 0)] * (rpb.ndim - 1) + [(pad, pad)])
    tiles = jnp.stack([rev[..., GRID_W - 1 - k:2 * GRID_W - 1 - k] for k in range(GRID_W)], axis=-2)
    tiles = jnp.where(col_ok.T, tiles, NEG)
    masked = jnp.full(lead + (GRID_W, GRID_W), NEG, F32)
    blocks = []
    for qb in (0, 1, NA_NBLK - 1):
        r0, kb = _na_block_geometry(qb)
        band_rows = []
        for j in range(NA_BAND):
            row = []
            for i in range(NA_QROWS):
                r, rk = r0 + i, kb + j
                rs = min(max(r - NA_KH // 2, 0), ROWS - NA_KH)
                row.append(tiles[..., rk - r + NA_KH - 1, :, :] if rs <= rk < rs + NA_KH else masked)
            band_rows.append(jnp.concatenate(row, axis=-1))
        blocks.append(jnp.concatenate(band_rows, axis=-2))
    return jnp.stack(blocks, axis=-3)


OUT_TM = 512


def _outproj_kernel(x_ref, ys_ref, yd_ref, yn_ref, w_ref, o_ref):
    acc = _dot(ys_ref[...], w_ref[0:SSD_WIDTH, :])
    acc = acc + _dot(yd_ref[...], w_ref[SSD_WIDTH:SSD_WIDTH + DIFF_WIDTH, :])
    acc = acc + _dot(yn_ref[...], w_ref[SSD_WIDTH + DIFF_WIDTH:MIX_WIDTH, :])
    o_ref[...] = x_ref[...] + acc


def _outproj(x2, y_ssd, y_diff, y_na, w_out):
    m = x2.shape[0]
    return pl.pallas_call(
        _outproj_kernel,
        grid=(m // OUT_TM,),
        in_specs=[
            pl.BlockSpec((OUT_TM, D_MODEL), lambda i: (i, 0)),
            pl.BlockSpec((OUT_TM, SSD_WIDTH), lambda i: (i, 0)),
            pl.BlockSpec((OUT_TM, DIFF_WIDTH), lambda i: (i, 0)),
            pl.BlockSpec((OUT_TM, NA_WIDTH), lambda i: (i, 0)),
            pl.BlockSpec((MIX_WIDTH, D_MODEL), lambda i: (0, 0)),
        ],
        out_specs=pl.BlockSpec((OUT_TM, D_MODEL), lambda i: (i, 0)),
        out_shape=jax.ShapeDtypeStruct((m, D_MODEL), F32),
        compiler_params=pltpu.CompilerParams(
            dimension_semantics=("arbitrary",), vmem_limit_bytes=VMEM_LIMIT),
        name="outproj",
    )(x2, y_ssd, y_diff, y_na, w_out)


def _rope_tables():
    inv_freq = ROPE_THETA ** (-jnp.arange(0, DIFF_HEAD_DIM, 2, dtype=F32) / DIFF_HEAD_DIM)
    ang = jnp.arange(SEQ, dtype=F32)[:, None] * inv_freq[None, :]
    cos, sin = jnp.cos(ang), jnp.sin(ang)
    cos_t = jnp.concatenate([cos, cos, cos, cos], axis=1)
    sin_t = jnp.concatenate([-sin, sin, -sin, sin], axis=1)
    return cos_t, sin_t


def kernel(x, norm_w, w_in, conv_w, conv_b, a_log, dt_bias, d_skip, ssd_norm_w, diff_qk_norm,
           diff_lambda, diff_subln, na_qk_norm, na_rpb, w_out):
    b, L, d = x.shape
    assert (L, d) == (SEQ, D_MODEL)
    depth = w_in.shape[0]
    cos_t, sin_t = _rope_tables()
    na_bias = _na_bias(na_rpb)
    dt_lo = SSD_WIDTH + SSD_XBC
    dt_hi = dt_lo + 2 * SSD_HEADS
    x2 = x.reshape(b * L, d)
    for i in range(depth):
        lam_init = 0.8 - 0.6 * math.exp(-0.3 * i)
        w_main = jnp.concatenate([w_in[i][:, :dt_lo], w_in[i][:, dt_hi:]], axis=1).astype(BF16)
        w_dt = jnp.pad(w_in[i][:, dt_lo:dt_hi], ((0, 0), (0, DT_PAD - 2 * SSD_HEADS))).astype(BF16)
        u, dt = _inproj(x2, norm_w[i][None, :], w_main, w_dt)
        u3 = u.reshape(b, L, U_MAIN)
        dt_t = jnp.swapaxes(dt.reshape(b, L, DT_PAD)[:, :, :2 * SSD_HEADS], 1, 2)

        hp = jnp.stack([a_log[i][0], a_log[i][1], dt_bias[i][0], dt_bias[i][1]], axis=-1)
        hp = jnp.pad(hp, ((0, 0), (0, 4))).reshape(SSD_GROUPS, GROUP_HEADS, 8)
        dsk_row = jnp.repeat(d_skip[i][0] + d_skip[i][1], SSD_HEAD_DIM).reshape(SSD_GROUPS, 1, GROUP_WIDTH)
        y_ssd = _ssd(u3, dt_t, conv_w[i], conv_b[i][None, :], hp, dsk_row,
                     ssd_norm_w[i].reshape(SSD_GROUPS, 1, GROUP_WIDTH))

        qk_w = jnp.concatenate([diff_qk_norm[i], diff_qk_norm[i]], axis=1)
        y_diff = _diff(u3, cos_t, sin_t, qk_w, diff_lambda[i], diff_subln[i][None, :], lam_init)

        na_w = jnp.concatenate([na_qk_norm[i], na_qk_norm[i]], axis=1)
        y_na = _na(u3, na_w, na_bias[i])

        x2 = _outproj(x2, y_ssd.reshape(b * L, SSD_WIDTH), y_diff.reshape(b * L, DIFF_WIDTH),
                      y_na.reshape(b * L, NA_WIDTH), w_out[i].astype(BF16))
    return x2.reshape(b, L, d)
```

```python
import functools
import math

import numpy as np
import jax
import jax.numpy as jnp
from jax import lax
from jax.experimental import pallas as pl
from jax.experimental.pallas import tpu as pltpu

F32 = jnp.float32
BF16 = jnp.bfloat16

D_MODEL = 1024
SEQ = 2048
GRID_W = 64
ROWS = SEQ // GRID_W
SSD_WIDTH = 1024
SSD_HEAD_DIM = 64
SSD_HEADS = 16
SSD_GROUPS = 2
SSD_STATE = 128
SSD_CONV = 5
SSD_CHUNK = 128
SSD_XBC = SSD_WIDTH + 2 * SSD_GROUPS * SSD_STATE
GROUP_HEADS = SSD_HEADS // SSD_GROUPS
GROUP_WIDTH = SSD_WIDTH // SSD_GROUPS
N_CHUNKS = SEQ // SSD_CHUNK
DIFF_WIDTH = 512
DIFF_HEAD_DIM = 64
DIFF_HEADS = 4
NA_WIDTH = 512
NA_HEAD_DIM = 64
NA_HEADS = 8
NA_KH = 8
NA_KW = 16
MIX_WIDTH = SSD_WIDTH + DIFF_WIDTH + NA_WIDTH
ROPE_THETA = 10000.0
EPS = 1e-6
LANES = 128
CONV_HALO = 16
CONV_WIN = SSD_CHUNK + 2 * CONV_HALO

U_MAIN = SSD_WIDTH + SSD_XBC + 4 * DIFF_WIDTH + 4 * NA_WIDTH
COL_Z = 0
COL_X = SSD_WIDTH
COL_B = COL_X + SSD_WIDTH
COL_C = COL_B + SSD_GROUPS * SSD_STATE
COL_DIFF = COL_C + SSD_GROUPS * SSD_STATE
COL_NA = COL_DIFF + 4 * DIFF_WIDTH
DT_PAD = LANES

NA_QROWS = 4
NA_BAND = 12
NA_QBLK = NA_QROWS * GRID_W
NA_KBLK = NA_BAND * GRID_W
NA_NBLK = ROWS // NA_QROWS
NEG = -1e30
LOG2E = math.log2(math.e)
ONES_ROWS = 16

VMEM_LIMIT = 56 * 1024 * 1024


def _silu(v):
    return v * (1.0 / (1.0 + jnp.exp(-v)))


def _dot(a, b):
    return jnp.dot(a, b, preferred_element_type=F32)


def _dot_nt(a, b):
    return lax.dot_general(a, b, (((1,), (1,)), ((), ())), preferred_element_type=F32)


def _dot_tn(a, b):
    return lax.dot_general(a, b, (((0,), (0,)), ((), ())), preferred_element_type=F32)


PACK_ROWS = 256


def _pack_kernel(w_ref, main_ref, dt_ref):
    dt_lo = SSD_WIDTH + SSD_XBC
    dt_hi = dt_lo + 2 * SSD_HEADS
    main_ref[0, :, 0:dt_lo] = w_ref[0, :, 0:dt_lo].astype(BF16)
    main_ref[0, :, dt_lo:U_MAIN] = w_ref[0, :, dt_hi:dt_hi + U_MAIN - dt_lo].astype(BF16)
    dt_ref[...] = jnp.zeros(dt_ref.shape, BF16)
    dt_ref[0, :, 0:2 * SSD_HEADS] = w_ref[0, :, dt_lo:dt_hi].astype(BF16)


def _pack_weights(w_in):
    depth, d, n = w_in.shape
    return pl.pallas_call(
        _pack_kernel,
        grid=(depth, d // PACK_ROWS),
        in_specs=[pl.BlockSpec((1, PACK_ROWS, n), lambda l, i: (l, i, 0))],
        out_specs=[
            pl.BlockSpec((1, PACK_ROWS, U_MAIN), lambda l, i: (l, i, 0)),
            pl.BlockSpec((1, PACK_ROWS, DT_PAD), lambda l, i: (l, i, 0)),
        ],
        out_shape=[
            jax.ShapeDtypeStruct((depth, d, U_MAIN), BF16),
            jax.ShapeDtypeStruct((depth, d, DT_PAD), BF16),
        ],
        compiler_params=pltpu.CompilerParams(
            dimension_semantics=("arbitrary", "arbitrary"), vmem_limit_bytes=VMEM_LIMIT),
        name="pack_weights",
    )(w_in)


IN_TM = 1024
IN_TN = 3328


def _inproj_kernel(x_ref, nw_ref, w_ref, wdt_ref, u_ref, dt_ref, h_scr):
    @pl.when(pl.program_id(1) == 0)
    def _():
        x = x_ref[...]
        ms = jnp.mean(x * x, axis=-1, keepdims=True)
        h = (x * lax.rsqrt(ms + EPS) * nw_ref[...]).astype(BF16)
        h_scr[...] = h
        dt_ref[...] = _dot(h, wdt_ref[...])

    u_ref[...] = _dot(h_scr[...], w_ref[...]).astype(u_ref.dtype)


def _inproj(x2, norm_w, w_main, w_dt):
    m = x2.shape[0]
    return pl.pallas_call(
        _inproj_kernel,
        grid=(m // IN_TM, U_MAIN // IN_TN),
        in_specs=[
            pl.BlockSpec((IN_TM, D_MODEL), lambda i, j: (i, 0)),
            pl.BlockSpec((1, D_MODEL), lambda i, j: (0, 0)),
            pl.BlockSpec((D_MODEL, IN_TN), lambda i, j: (0, j)),
            pl.BlockSpec((D_MODEL, DT_PAD), lambda i, j: (0, 0)),
        ],
        out_specs=[
            pl.BlockSpec((IN_TM, IN_TN), lambda i, j: (i, j)),
            pl.BlockSpec((IN_TM, DT_PAD), lambda i, j: (i, 0)),
        ],
        out_shape=[
            jax.ShapeDtypeStruct((m, U_MAIN), BF16),
            jax.ShapeDtypeStruct((m, DT_PAD), F32),
        ],
        scratch_shapes=[pltpu.VMEM((IN_TM, D_MODEL), BF16)],
        compiler_params=pltpu.CompilerParams(
            dimension_semantics=("arbitrary", "arbitrary"), vmem_limit_bytes=VMEM_LIMIT),
        name="inproj",
    )(x2, norm_w, w_main, w_dt)


R_WF, R_DF = 0, 1
R_WB, R_DB = 4, 5
R_COLF, R_COLB = 8, 9
R_EF, R_EB = 10, 11
R_ROWF, R_ROWB = 14, 15


def _softplus(v):
    return jnp.maximum(v, 0.0) + jnp.log1p(jnp.exp(-jnp.abs(v)))


def _chunk_scan(a, lane, reverse):
    n = a.shape[-1]
    out = a
    k = 1
    while k < SSD_CHUNK:
        if reverse:
            out = out + jnp.where(lane < SSD_CHUNK - k, pltpu.roll(out, n - k, axis=1), 0.0)
        else:
            out = out + jnp.where(lane >= k, pltpu.roll(out, k, axis=1), 0.0)
        k *= 2
    return out


def _ssd_kernel(z_ref, x_ref, b_ref, c_ref, dtf_ref, dtb_ref, cwx_ref, cwb_ref, cwc_ref,
                cbx_ref, cbb_ref, cbc_ref, hp_ref, dsk_ref, nw_ref, o_ref,
                shift_scr, xs_scr, bt_scr, cm_scr, y_scr, hm_scr, tm_scr, dec_scr, sf_scr, sb_scr):
    L = SEQ
    W = GROUP_WIDTH
    NS = SSD_STATE

    side = (SSD_CONV - 1) * SSD_CHUNK
    tap_t = lax.broadcasted_iota(jnp.int32, (side, CONV_WIN), 0)
    tap_j = lax.broadcasted_iota(jnp.int32, (side, CONV_WIN), 1)
    tap_k = tap_t // SSD_CHUNK
    tap_src = tap_t % SSD_CHUNK + jnp.where(tap_k >= SSD_CONV // 2, tap_k + 1, tap_k) - SSD_CONV // 2
    for variant in range(3):
        shift_scr[variant] = jnp.where(tap_j == tap_src + variant * CONV_HALO, 1.0, 0.0).astype(BF16)
    side_taps = [k for k in range(SSD_CONV) if k != SSD_CONV // 2]

    def conv(c, carry):
        r = pl.multiple_of(c * SSD_CHUNK, SSD_CHUNK)
        w0 = pl.multiple_of(jnp.clip(r - CONV_HALO, 0, L - CONV_WIN), CONV_HALO)
        variant = jnp.where(c == 0, 0, jnp.where(c == N_CHUNKS - 1, 2, 1))
        shift = shift_scr[variant]
        rows = pl.ds(r, SSD_CHUNK)

        def taps(win, centre, cw, cb):
            sh = _dot(shift, win)
            acc = cb + centre.astype(F32) * cw[SSD_CONV // 2:SSD_CONV // 2 + 1, :]
            for slot, k in enumerate(side_taps):
                acc = acc + sh[slot * SSD_CHUNK:(slot + 1) * SSD_CHUNK, :] * cw[k:k + 1, :]
            return _silu(acc)

        xs_scr[rows, :] = taps(x_ref[0, pl.ds(w0, CONV_WIN), :], x_ref[0, rows, :], cwx_ref[...], cbx_ref[...])
        win_bc = jnp.concatenate([b_ref[0, pl.ds(w0, CONV_WIN), :], c_ref[0, pl.ds(w0, CONV_WIN), :]], axis=1)
        mid_bc = jnp.concatenate([b_ref[0, rows, :], c_ref[0, rows, :]], axis=1)
        act_bc = taps(win_bc, mid_bc, jnp.concatenate([cwb_ref[...], cwc_ref[...]], axis=1),
                      jnp.concatenate([cbb_ref[...], cbc_ref[...]], axis=1))
        bt_scr[:, rows] = act_bc[:, 0:NS].T.astype(BF16)
        cm_scr[rows, :] = act_bc[:, NS:2 * NS].astype(BF16)
        return carry

    lax.fori_loop(0, N_CHUNKS, conv, 0, unroll=2)

    hp = hp_ref[0]
    a_f = -jnp.exp(hp[:, 0:1])
    a_b = -jnp.exp(hp[:, 1:2])
    dt_f = _softplus(dtf_ref[0] + hp[:, 2:3])
    dt_b = _softplus(dtb_ref[0] + hp[:, 3:4])
    da_f = dt_f * a_f
    da_b = dt_b * a_b
    lane = lax.broadcasted_iota(jnp.int32, (GROUP_HEADS, L), 1) % SSD_CHUNK
    cs_f = _chunk_scan(da_f, lane, False)
    rs_f = _chunk_scan(da_f, lane, True) - da_f
    cs_b = _chunk_scan(da_b, lane, False)
    ecs_b = cs_b - da_b
    rs_b = _chunk_scan(da_b, lane, True)

    def put(row, v):
        hm_scr[row * GROUP_HEADS:(row + 1) * GROUP_HEADS, :] = v

    def put_split(row_hi, v):
        hi = v.astype(BF16).astype(F32)
        put(row_hi, hi)
        put(row_hi + 2, (v - hi).astype(BF16).astype(F32))

    put_split(R_WF, jnp.exp(rs_f) * dt_f)
    put_split(R_DF, jnp.exp(cs_f))
    put_split(R_WB, jnp.exp(ecs_b) * dt_b)
    put_split(R_DB, jnp.exp(rs_b))
    put_split(R_EF, jnp.exp(cs_f + rs_f))
    put_split(R_EB, jnp.exp(ecs_b + rs_b))
    put(R_COLF, cs_f * LOG2E)
    put(R_COLB, ecs_b * LOG2E)
    put(R_ROWF, cs_f * LOG2E - jnp.log2(dt_f))
    put(R_ROWB, ecs_b * LOG2E + jnp.log2(dt_b))

    def to_time_major(c, carry):
        r = pl.multiple_of(c * SSD_CHUNK, SSD_CHUNK)
        tm_scr[pl.ds(r, SSD_CHUNK), :] = hm_scr[:, pl.ds(r, SSD_CHUNK)].T
        return carry

    lax.fori_loop(0, N_CHUNKS, to_time_major, 0)

    e_row = lax.broadcasted_iota(jnp.int32, (4 * GROUP_HEADS, 2 * W), 0) % (2 * GROUP_HEADS)
    e_col = lax.broadcasted_iota(jnp.int32, (4 * GROUP_HEADS, 2 * W), 1) // SSD_HEAD_DIM
    spread = jnp.where(e_row == e_col, 1.0, 0.0).astype(BF16)

    def expand(rows, first_row):
        cols = slice(first_row * GROUP_HEADS, (first_row + 4) * GROUP_HEADS)
        return _dot(tm_scr[rows, :][:, cols].astype(BF16), spread)

    dec_scr[...] = expand(pl.ds(0, N_CHUNKS, stride=SSD_CHUNK), R_EF)

    row_i = lax.broadcasted_iota(jnp.int32, (SSD_CHUNK, SSD_CHUNK), 0)
    col_i = lax.broadcasted_iota(jnp.int32, (SSD_CHUNK, SSD_CHUNK), 1)
    lower = col_i <= row_i
    upper = col_i >= row_i
    left = lax.broadcasted_iota(jnp.int32, (SSD_CHUNK, LANES), 1) < SSD_HEAD_DIM
    dsk = dsk_ref[0]
    nw = nw_ref[0]

    def hm_row(row, h, r):
        return hm_scr[row * GROUP_HEADS + h:row * GROUP_HEADS + h + 1, pl.ds(r, SSD_CHUNK)]

    def forward_part(c):
        r = pl.multiple_of(c * SSD_CHUNK, SSD_CHUNK)
        rows = pl.ds(r, SSD_CHUNK)
        xs_c = xs_scr[rows, :]
        bt_c = bt_scr[:, rows]
        cm_c = cm_scr[rows, :]
        g = _dot(cm_c, bt_c)
        xs_b = xs_c.astype(BF16)
        colf = tm_scr[rows, R_COLF * GROUP_HEADS:(R_COLF + 1) * GROUP_HEADS]
        colb = tm_scr[rows, R_COLB * GROUP_HEADS:(R_COLB + 1) * GROUP_HEADS]
        y_pairs = []
        for hp_i in range(GROUP_HEADS // 2):
            ms = []
            for h in (2 * hp_i, 2 * hp_i + 1):
                seg_f = jnp.where(lower, colf[:, h:h + 1] - hm_row(R_ROWF, h, r), NEG)
                seg_b = jnp.where(upper, hm_row(R_ROWB, h, r) - colb[:, h:h + 1], NEG)
                ms.append((g * (jnp.exp2(seg_f) + jnp.exp2(seg_b))).astype(BF16))
            xp = xs_b[:, hp_i * LANES:(hp_i + 1) * LANES]
            zero = jnp.zeros_like(xp)
            rhs = jnp.concatenate([jnp.where(left, xp, zero), jnp.where(left, zero, xp)], axis=0)
            y_pairs.append(_dot(jnp.concatenate(ms, axis=1), rhs))
        y_diag = jnp.concatenate(y_pairs, axis=1)
        ex = expand(rows, R_WF)
        s_f = sf_scr[...]
        y_off = _dot(cm_c, s_f.astype(BF16)) * ex[:, W:2 * W]
        xw = (xs_c * ex[:, 0:W]).astype(BF16)
        sf_scr[...] = s_f * dec_scr[pl.ds(c, 1), 0:W] + _dot(bt_c, xw)
        return y_diag + y_off

    def backward_part(c):
        r = pl.multiple_of(c * SSD_CHUNK, SSD_CHUNK)
        rows = pl.ds(r, SSD_CHUNK)
        ex = expand(rows, R_WB)
        s_b = sb_scr[...]
        y_off = _dot(cm_scr[rows, :], s_b.astype(BF16)) * ex[:, W:2 * W]
        xw = (xs_scr[rows, :] * ex[:, 0:W]).astype(BF16)
        sb_scr[...] = s_b * dec_scr[pl.ds(c, 1), W:2 * W] + _dot(bt_scr[:, rows], xw)
        return y_off

    def finalize(c, y):
        r = pl.multiple_of(c * SSD_CHUNK, SSD_CHUNK)
        rows = pl.ds(r, SSD_CHUNK)
        y = y + y_scr[rows, :] + dsk * xs_scr[rows, :]
        y = y * _silu(z_ref[0, rows, :].astype(F32))
        y = y * lax.rsqrt(jnp.mean(y * y, axis=-1, keepdims=True) + EPS) * nw
        o_ref[0, rows, :] = y.astype(o_ref.dtype)

    sf_scr[...] = jnp.zeros((NS, W), F32)
    sb_scr[...] = jnp.zeros((NS, W), F32)
    half = N_CHUNKS // 2

    def first_half(i, carry):
        cb = N_CHUNKS - 1 - i
        y_scr[pl.ds(pl.multiple_of(i * SSD_CHUNK, SSD_CHUNK), SSD_CHUNK), :] = forward_part(i)
        y_scr[pl.ds(pl.multiple_of(cb * SSD_CHUNK, SSD_CHUNK), SSD_CHUNK), :] = backward_part(cb)
        return carry

    def second_half(i, carry):
        cb = N_CHUNKS - 1 - i
        finalize(i, forward_part(i))
        finalize(cb, backward_part(cb))
        return carry

    lax.fori_loop(0, half, first_half, 0, unroll=2)
    lax.fori_loop(half, N_CHUNKS, second_half, 0, unroll=2)


def _ssd(u3, dt_t, conv_w, conv_b, head_params, dsk_row, norm_w):
    b = u3.shape[0]
    L = SEQ
    W = GROUP_WIDTH
    NS = SSD_STATE
    G = SSD_GROUPS
    xblk = COL_X // W
    bblk = COL_B // NS
    cblk = COL_C // NS
    in_specs = [
        pl.BlockSpec((1, L, W), lambda i, g: (i, 0, g)),
        pl.BlockSpec((1, L, W), lambda i, g: (i, 0, xblk + g)),
        pl.BlockSpec((1, L, NS), lambda i, g: (i, 0, bblk + g)),
        pl.BlockSpec((1, L, NS), lambda i, g: (i, 0, cblk + g)),
        pl.BlockSpec((1, GROUP_HEADS, L), lambda i, g: (i, g, 0)),
        pl.BlockSpec((1, GROUP_HEADS, L), lambda i, g: (i, G + g, 0)),
        pl.BlockSpec((SSD_CONV, W), lambda i, g: (0, g)),
        pl.BlockSpec((SSD_CONV, NS), lambda i, g: (0, SSD_WIDTH // NS + g)),
        pl.BlockSpec((SSD_CONV, NS), lambda i, g: (0, SSD_WIDTH // NS + G + g)),
        pl.BlockSpec((1, W), lambda i, g: (0, g)),
        pl.BlockSpec((1, NS), lambda i, g: (0, SSD_WIDTH // NS + g)),
        pl.BlockSpec((1, NS), lambda i, g: (0, SSD_WIDTH // NS + G + g)),
        pl.BlockSpec((1, GROUP_HEADS, 8), lambda i, g: (g, 0, 0)),
        pl.BlockSpec((1, 1, W), lambda i, g: (g, 0, 0)),
        pl.BlockSpec((1, 1, W), lambda i, g: (g, 0, 0)),
    ]
    return pl.pallas_call(
        _ssd_kernel,
        grid=(b, G),
        in_specs=in_specs,
        out_specs=pl.BlockSpec((1, L, W), lambda i, g: (i, 0, g)),
        out_shape=jax.ShapeDtypeStruct((b, L, SSD_WIDTH), BF16),
        scratch_shapes=[
            pltpu.VMEM((3, (SSD_CONV - 1) * SSD_CHUNK, CONV_WIN), BF16),
            pltpu.VMEM((L, W), F32),
            pltpu.VMEM((NS, L), BF16),
            pltpu.VMEM((L, NS), BF16),
            pltpu.VMEM((L, W), F32),
            pltpu.VMEM((LANES, L), F32),
            pltpu.VMEM((L, LANES), F32),
            pltpu.VMEM((N_CHUNKS, 2 * W), F32),
            pltpu.VMEM((NS, W), F32),
            pltpu.VMEM((NS, W), F32),
        ],
        compiler_params=pltpu.CompilerParams(
            dimension_semantics=("arbitrary", "arbitrary"), vmem_limit_bytes=VMEM_LIMIT),
        name="ssd",
    )(u3, u3, u3, u3, dt_t, dt_t, conv_w, conv_w, conv_w, conv_b, conv_b, conv_b,
      head_params, dsk_row, norm_w)


def _group_mean_sq(v, seg):
    sq = v * v
    hi = sq.astype(BF16)
    lo = (sq - hi.astype(F32)).astype(BF16)
    return (_dot(hi, seg) + _dot(lo, seg)) * (1.0 / 64.0)


def _seg_matrix():
    r = lax.broadcasted_iota(jnp.int32, (LANES, LANES), 0) // 64
    c = lax.broadcasted_iota(jnp.int32, (LANES, LANES), 1) // 64
    return jnp.where(r == c, 1.0, 0.0).astype(BF16)


DIFF_TQ = 256
DIFF_HEADS_PER_STEP = 2


def _diff_kernel(lam_init, q_ref, k_ref, v_ref, g_ref, qc_ref, qs_ref, kc_ref, ks_ref, lam_ref,
                 sub_ref, o_ref, q_scr, k_scr, vt_scr, sa_scr, sb_scr):
    L = SEQ
    seg = _seg_matrix()
    comp0 = lax.broadcasted_iota(jnp.int32, (DIFF_TQ, LANES), 1) < 64
    p_row = lax.broadcasted_iota(jnp.int32, (LANES, LANES), 0)
    p_col = lax.broadcasted_iota(jnp.int32, (LANES, LANES), 1)
    swap = jnp.where((p_row ^ 32) == p_col, 1.0, 0.0).astype(BF16)

    def norm_rope(ref, r, lanes, cw_ref, sw_ref):
        vb = ref[0, pl.ds(r, DIFF_TQ), lanes]
        v = vb.astype(F32)
        rinv = lax.rsqrt(_group_mean_sq(v, seg) + EPS)
        return rinv * (v * cw_ref[pl.ds(r, DIFF_TQ), :] + _dot(vb, swap) * sw_ref[pl.ds(r, DIFF_TQ), :])

    n_blk = L // DIFF_TQ
    n_items = DIFF_HEADS_PER_STEP * n_blk

    def item(j):
        hd = j // n_blk
        r = pl.multiple_of((j % n_blk) * DIFF_TQ, DIFF_TQ)
        return hd, r, pl.ds(pl.multiple_of(hd * LANES, LANES), LANES)

    for hd in range(DIFF_HEADS_PER_STEP):
        vt_scr[hd, LANES:LANES + ONES_ROWS, :] = jnp.ones((ONES_ROWS, L), BF16)

    def prep(j, carry):
        hd, r, lanes = item(j)
        q_scr[hd, pl.ds(r, DIFF_TQ), :] = norm_rope(q_ref, r, lanes, qc_ref, qs_ref).astype(BF16)
        kn = norm_rope(k_ref, r, lanes, kc_ref, ks_ref)
        k_scr[hd, 0, pl.ds(r, DIFF_TQ), :] = jnp.where(comp0, kn, 0.0).astype(BF16)
        k_scr[hd, 1, pl.ds(r, DIFF_TQ), :] = jnp.where(comp0, 0.0, kn).astype(BF16)
        vt_scr[hd, 0:LANES, pl.ds(r, DIFF_TQ)] = v_ref[0, pl.ds(r, DIFF_TQ), lanes].astype(F32).T.astype(BF16)
        return carry

    lax.fori_loop(0, n_items, prep, 0, unroll=2)

    lam = lam_ref[...]
    lam_full = (jnp.exp(jnp.sum(lam[0:1] * lam[1:2], axis=-1, keepdims=True))
                - jnp.exp(jnp.sum(lam[2:3] * lam[3:4], axis=-1, keepdims=True)) + lam_init)

    def scores(j, s_ref):
        hd, r, _ = item(j)
        qb = q_scr[hd, pl.ds(r, DIFF_TQ), :]
        for c in range(2):
            s_ref[c] = _dot_nt(k_scr[hd, c], qb)

    def finish(j, s_ref):
        hd, r, lanes = item(j)
        parts = []
        for c in range(2):
            s = s_ref[c]
            e = jnp.exp2(s - jnp.max(s, axis=0, keepdims=True)).astype(BF16)
            acc = _dot(vt_scr[hd], e)
            parts.append(acc[0:LANES, :] * (1.0 / acc[LANES:LANES + 1, :]))
        ot = parts[0] - lam_full * parts[1]
        o = ot.T
        o = o * lax.rsqrt(jnp.mean(o * o, axis=-1, keepdims=True) + EPS) * sub_ref[...]
        o = o * (1.0 - lam_init) * _silu(g_ref[0, pl.ds(r, DIFF_TQ), lanes].astype(F32))
        o_ref[0, pl.ds(r, DIFF_TQ), lanes] = o.astype(o_ref.dtype)

    scores(0, sa_scr)

    def pair(i, carry):
        scores(2 * i + 1, sb_scr)
        finish(2 * i, sa_scr)
        scores(2 * i + 2, sa_scr)
        finish(2 * i + 1, sb_scr)
        return carry

    lax.fori_loop(0, n_items // 2 - 1, pair, 0)
    scores(n_items - 1, sb_scr)
    finish(n_items - 2, sa_scr)
    finish(n_items - 1, sb_scr)


def _diff(u3, rope_tabs, lam, subln_w, lam_init):
    b = u3.shape[0]
    L = SEQ
    step_w = DIFF_HEADS_PER_STEP * LANES
    base = COL_DIFF // step_w
    nb = DIFF_WIDTH // step_w

    def spec(part):
        return pl.BlockSpec((1, L, step_w), lambda i, h: (i, 0, base + part * nb + h))

    return pl.pallas_call(
        functools.partial(_diff_kernel, lam_init),
        grid=(b, DIFF_HEADS // DIFF_HEADS_PER_STEP),
        in_specs=[
            spec(0), spec(1), spec(2), spec(3),
            pl.BlockSpec((L, LANES), lambda i, h: (0, 0)),
            pl.BlockSpec((L, LANES), lambda i, h: (0, 0)),
            pl.BlockSpec((L, LANES), lambda i, h: (0, 0)),
            pl.BlockSpec((L, LANES), lambda i, h: (0, 0)),
            pl.BlockSpec((4, DIFF_HEAD_DIM), lambda i, h: (0, 0)),
            pl.BlockSpec((1, LANES), lambda i, h: (0, 0)),
        ],
        out_specs=pl.BlockSpec((1, L, step_w), lambda i, h: (i, 0, h)),
        out_shape=jax.ShapeDtypeStruct((b, L, DIFF_WIDTH), BF16),
        scratch_shapes=[
            pltpu.VMEM((DIFF_HEADS_PER_STEP, L, LANES), BF16),
            pltpu.VMEM((DIFF_HEADS_PER_STEP, 2, L, LANES), BF16),
            pltpu.VMEM((DIFF_HEADS_PER_STEP, LANES + ONES_ROWS, L), BF16),
            pltpu.VMEM((2, L, DIFF_TQ), F32),
            pltpu.VMEM((2, L, DIFF_TQ), F32),
        ],
        compiler_params=pltpu.CompilerParams(
            dimension_semantics=("arbitrary", "arbitrary"), vmem_limit_bytes=VMEM_LIMIT),
        name="diff_attn",
    )(u3, u3, u3, u3, *rope_tabs, lam, subln_w)


def _na_block_geometry(qb):
    r0 = qb * NA_QROWS
    kb = min(max(r0 - NA_KH // 2, 0), ROWS - NA_BAND)
    return r0, kb


def _na_bias_type(qb):
    return 0 if qb == 0 else (2 if qb == NA_NBLK - 1 else 1)


def _na_kernel(q_ref, k_ref, v_ref, g_ref, qkw_ref, bias_ref, o_ref, q_scr, k_scr, vt_scr, s_scr):
    L = SEQ
    seg = _seg_matrix()
    head0_blk = lax.broadcasted_iota(jnp.int32, (NA_QBLK, LANES), 1) < 64
    q_scale = (NA_HEAD_DIM ** -0.5) * LOG2E
    vt_scr[LANES:LANES + ONES_ROWS, :] = jnp.ones((ONES_ROWS, L), BF16)

    def norm(ref, r, w_row):
        v = ref[0, pl.ds(r, NA_QBLK), :].astype(F32)
        return v * lax.rsqrt(_group_mean_sq(v, seg) + EPS) * w_row

    def prep(i, carry):
        r = pl.multiple_of(i * NA_QBLK, NA_QBLK)
        q_scr[pl.ds(r, NA_QBLK), :] = (norm(q_ref, r, qkw_ref[0:1, :]) * q_scale).astype(BF16)
        kn = norm(k_ref, r, qkw_ref[1:2, :])
        k_scr[0, pl.ds(r, NA_QBLK), :] = jnp.where(head0_blk, kn, 0.0).astype(BF16)
        k_scr[1, pl.ds(r, NA_QBLK), :] = jnp.where(head0_blk, 0.0, kn).astype(BF16)
        vt_scr[0:LANES, pl.ds(r, NA_QBLK)] = v_ref[0, pl.ds(r, NA_QBLK), :].astype(F32).T.astype(BF16)
        return carry

    lax.fori_loop(0, L // NA_QBLK, prep, 0, unroll=2)

    def slices(qb):
        r0, kb = _na_block_geometry(qb)
        return (slice(r0 * GRID_W, r0 * GRID_W + NA_QBLK), slice(kb * GRID_W, kb * GRID_W + NA_KBLK))

    def scores(qb):
        qs, ks = slices(qb)
        q_blk = q_scr[qs, :]
        for h in range(2):
            s_scr[qb % 2, h] = _dot_nt(k_scr[h, ks, :], q_blk)

    def finish(qb):
        qs, ks = slices(qb)
        t = _na_bias_type(qb)
        vt = vt_scr[:, ks]
        outs = []
        for h in range(2):
            s = s_scr[qb % 2, h] + bias_ref[h, t]
            e = jnp.exp2(s - jnp.max(s, axis=0, keepdims=True)).astype(BF16)
            acc = _dot(vt, e)
            outs.append(acc[h * NA_HEAD_DIM:(h + 1) * NA_HEAD_DIM, :] * (1.0 / acc[LANES:LANES + 1, :]))
        o = jnp.concatenate(outs, axis=0).T
        o = o * _silu(g_ref[0, qs, :].astype(F32))
        o_ref[0, qs, :] = o.astype(o_ref.dtype)

    scores(0)
    for qb in range(NA_NBLK):
        if qb + 1 < NA_NBLK:
            scores(qb + 1)
        finish(qb)


def _na(u3, qk_w, bias):
    b = u3.shape[0]
    L = SEQ
    base = COL_NA // LANES
    nb = NA_WIDTH // LANES

    def spec(part):
        return pl.BlockSpec((1, L, LANES), lambda hp, i: (i, 0, base + part * nb + hp))

    return pl.pallas_call(
        _na_kernel,
        grid=(NA_HEADS // 2, b),
        in_specs=[
            spec(0), spec(1), spec(2), spec(3),
            pl.BlockSpec((2, LANES), lambda hp, i: (0, 0)),
            pl.BlockSpec((2, 3, NA_KBLK, NA_QBLK), lambda hp, i: (hp, 0, 0, 0)),
        ],
        out_specs=pl.BlockSpec((1, L, LANES), lambda hp, i: (i, 0, hp)),
        out_shape=jax.ShapeDtypeStruct((b, L, NA_WIDTH), BF16),
        scratch_shapes=[
            pltpu.VMEM((L, LANES), BF16),
            pltpu.VMEM((2, L, LANES), BF16),
            pltpu.VMEM((LANES + ONES_ROWS, L), BF16),
            pltpu.VMEM((2, 2, NA_KBLK, NA_QBLK), F32),
        ],
        compiler_params=pltpu.CompilerParams(
            dimension_semantics=("arbitrary", "arbitrary"), vmem_limit_bytes=VMEM_LIMIT),
        name="na_attn",
    )(u3, u3, u3, u3, qk_w, bias)


def _na_bias(rpb):
    c = np.arange(GRID_W)
    col_start = np.clip(c - NA_KW // 2, 0, GRID_W - NA_KW)
    col_ok = (c[None, :] >= col_start[:, None]) & (c[None, :] < col_start[:, None] + NA_KW)
    lead = rpb.shape[:-2]
    pad = GRID_W - NA_KW
    rev = jnp.pad(rpb.astype(F32)[..., ::-1] * LOG2E, [(0, 0)] * (rpb.ndim - 1) + [(pad, pad)])
    tiles = jnp.stack([rev[..., GRID_W - 1 - k:2 * GRID_W - 1 - k] for k in range(GRID_W)], axis=-2)
    tiles = jnp.where(col_ok.T, tiles, NEG)
    masked = jnp.full(lead + (GRID_W, GRID_W), NEG, F32)
    blocks = []
    for qb in (0, 1, NA_NBLK - 1):
        r0, kb = _na_block_geometry(qb)
        band_rows = []
        for j in range(NA_BAND):
            row = []
            for i in range(NA_QROWS):
                r, rk = r0 + i, kb + j
                rs = min(max(r - NA_KH // 2, 0), ROWS - NA_KH)
                row.append(tiles[..., rk - r + NA_KH - 1, :, :] if rs <= rk < rs + NA_KH else masked)
            band_rows.append(jnp.concatenate(row, axis=-1))
        blocks.append(jnp.concatenate(band_rows, axis=-2))
    return jnp.stack(blocks, axis=-3)


OUT_TM = 512


def _outproj_kernel(x_ref, ys_ref, yd_ref, yn_ref, w_ref, o_ref):
    acc = _dot(ys_ref[...], w_ref[0:SSD_WIDTH, :])
    acc = acc + _dot(yd_ref[...], w_ref[SSD_WIDTH:SSD_WIDTH + DIFF_WIDTH, :])
    acc = acc + _dot(yn_ref[...], w_ref[SSD_WIDTH + DIFF_WIDTH:MIX_WIDTH, :])
    o_ref[...] = x_ref[...] + acc


def _outproj(x2, y_ssd, y_diff, y_na, w_out):
    m = x2.shape[0]
    return pl.pallas_call(
        _outproj_kernel,
        grid=(m // OUT_TM,),
        in_specs=[
            pl.BlockSpec((OUT_TM, D_MODEL), lambda i: (i, 0)),
            pl.BlockSpec((OUT_TM, SSD_WIDTH), lambda i: (i, 0)),
            pl.BlockSpec((OUT_TM, DIFF_WIDTH), lambda i: (i, 0)),
            pl.BlockSpec((OUT_TM, NA_WIDTH), lambda i: (i, 0)),
            pl.BlockSpec((MIX_WIDTH, D_MODEL), lambda i: (0, 0)),
        ],
        out_specs=pl.BlockSpec((OUT_TM, D_MODEL), lambda i: (i, 0)),
        out_shape=jax.ShapeDtypeStruct((m, D_MODEL), F32),
        compiler_params=pltpu.CompilerParams(
            dimension_semantics=("arbitrary",), vmem_limit_bytes=VMEM_LIMIT),
        name="outproj",
    )(x2, y_ssd, y_diff, y_na, w_out)


def _rope_tables():
    inv_freq = ROPE_THETA ** (-jnp.arange(0, DIFF_HEAD_DIM, 2, dtype=F32) / DIFF_HEAD_DIM)
    ang = jnp.arange(SEQ, dtype=F32)[:, None] * inv_freq[None, :]
    cos, sin = jnp.cos(ang), jnp.sin(ang)
    cos_t = jnp.concatenate([cos, cos, cos, cos], axis=1)
    sin_t = jnp.concatenate([-sin, sin, -sin, sin], axis=1)
    return cos_t, sin_t


def kernel(x, norm_w, w_in, conv_w, conv_b, a_log, dt_bias, d_skip, ssd_norm_w, diff_qk_norm,
           diff_lambda, diff_subln, na_qk_norm, na_rpb, w_out):
    b, L, d = x.shape
    assert (L, d) == (SEQ, D_MODEL)
    depth = w_in.shape[0]
    cos_t, sin_t = _rope_tables()
    na_bias = _na_bias(na_rpb)
    w_main, w_dt = _pack_weights(w_in)
    x2 = x.reshape(b * L, d)
    for i in range(depth):
        lam_init = 0.8 - 0.6 * math.exp(-0.3 * i)
        u, dt = _inproj(x2, norm_w[i][None, :], w_main[i], w_dt[i])
        u3 = u.reshape(b, L, U_MAIN)
        dt_t = jnp.swapaxes(dt.reshape(b, L, DT_PAD)[:, :, :2 * SSD_HEADS], 1, 2)

        hp = jnp.stack([a_log[i][0], a_log[i][1], dt_bias[i][0], dt_bias[i][1]], axis=-1)
        hp = jnp.pad(hp, ((0, 0), (0, 4))).reshape(SSD_GROUPS, GROUP_HEADS, 8)
        dsk_row = jnp.repeat(d_skip[i][0] + d_skip[i][1], SSD_HEAD_DIM).reshape(SSD_GROUPS, 1, GROUP_WIDTH)
        y_ssd = _ssd(u3, dt_t, conv_w[i], conv_b[i][None, :], hp, dsk_row,
                     ssd_norm_w[i].reshape(SSD_GROUPS, 1, GROUP_WIDTH))

        half = DIFF_HEAD_DIM // 2
        qk_w = jnp.tile(diff_qk_norm[i], (1, 2))
        qk_w_sw = jnp.tile(jnp.concatenate([diff_qk_norm[i][:, half:], diff_qk_norm[i][:, :half]], axis=1), (1, 2))
        q_scale = (DIFF_HEAD_DIM ** -0.5) * LOG2E
        rope_tabs = (cos_t * (qk_w[0:1] * q_scale), sin_t * (qk_w_sw[0:1] * q_scale),
                     cos_t * qk_w[1:2], sin_t * qk_w_sw[1:2])
        y_diff = _diff(u3, rope_tabs, diff_lambda[i], diff_subln[i][None, :], lam_init)

        na_w = jnp.concatenate([na_qk_norm[i], na_qk_norm[i]], axis=1)
        y_na = _na(u3, na_w, na_bias[i])

        x2 = _outproj(x2, y_ssd.reshape(b * L, SSD_WIDTH), y_diff.reshape(b * L, DIFF_WIDTH),
                      y_na.reshape(b * L, NA_WIDTH), w_out[i].astype(BF16))
    return x2.reshape(b, L, d)
```

```python
import functools
import math

import numpy as np
import jax
import jax.numpy as jnp
from jax import lax
from jax.experimental import pallas as pl
from jax.experimental.pallas import tpu as pltpu

F32 = jnp.float32
BF16 = jnp.bfloat16

D_MODEL = 1024
SEQ = 2048
GRID_W = 64
ROWS = SEQ // GRID_W
SSD_WIDTH = 1024
SSD_HEAD_DIM = 64
SSD_HEADS = 16
SSD_GROUPS = 2
SSD_STATE = 128
SSD_CONV = 5
SSD_CHUNK = 128
SSD_XBC = SSD_WIDTH + 2 * SSD_GROUPS * SSD_STATE
GROUP_HEADS = SSD_HEADS // SSD_GROUPS
GROUP_WIDTH = SSD_WIDTH // SSD_GROUPS
N_CHUNKS = SEQ // SSD_CHUNK
DIFF_WIDTH = 512
DIFF_HEAD_DIM = 64
DIFF_HEADS = 4
NA_WIDTH = 512
NA_HEAD_DIM = 64
NA_HEADS = 8
NA_KH = 8
NA_KW = 16
MIX_WIDTH = SSD_WIDTH + DIFF_WIDTH + NA_WIDTH
ROPE_THETA = 10000.0
EPS = 1e-6
LANES = 128
CONV_HALO = 16
CONV_WIN = SSD_CHUNK + 2 * CONV_HALO

U_MAIN = SSD_WIDTH + SSD_XBC + 4 * DIFF_WIDTH + 4 * NA_WIDTH
COL_Z = 0
COL_X = SSD_WIDTH
COL_B = COL_X + SSD_WIDTH
COL_C = COL_B + SSD_GROUPS * SSD_STATE
COL_DIFF = COL_C + SSD_GROUPS * SSD_STATE
COL_NA = COL_DIFF + 4 * DIFF_WIDTH
DT_PAD = LANES

NA_QROWS = 4
NA_BAND = 12
NA_QBLK = NA_QROWS * GRID_W
NA_KBLK = NA_BAND * GRID_W
NA_NBLK = ROWS // NA_QROWS
NEG = -1e30
LOG2E = math.log2(math.e)
ONES_ROWS = 16

VMEM_LIMIT = 56 * 1024 * 1024


def _silu(v):
    return v * (1.0 / (1.0 + jnp.exp(-v)))


def _dot(a, b):
    return jnp.dot(a, b, preferred_element_type=F32)


def _dot_nt(a, b):
    return lax.dot_general(a, b, (((1,), (1,)), ((), ())), preferred_element_type=F32)


def _dot_tn(a, b):
    return lax.dot_general(a, b, (((0,), (0,)), ((), ())), preferred_element_type=F32)


IN_TM = 1024
IN_TN = 3328


def _inproj_kernel(x_ref, nw_ref, w_ref, wdt_ref, u_ref, dt_ref, h_scr):
    @pl.when(pl.program_id(1) == 0)
    def _():
        x = x_ref[...]
        ms = jnp.mean(x * x, axis=-1, keepdims=True)
        h = (x * lax.rsqrt(ms + EPS) * nw_ref[...]).astype(BF16)
        h_scr[...] = h
        dt_ref[...] = _dot(h, wdt_ref[...])

    u_ref[...] = _dot(h_scr[...], w_ref[...]).astype(u_ref.dtype)


def _inproj(x2, norm_w, w_main, w_dt, layer):
    m = x2.shape[0]
    return pl.pallas_call(
        _inproj_kernel,
        grid=(m // IN_TM, U_MAIN // IN_TN),
        in_specs=[
            pl.BlockSpec((IN_TM, D_MODEL), lambda i, j: (i, 0)),
            pl.BlockSpec((1, D_MODEL), lambda i, j: (0, 0)),
            pl.BlockSpec((None, D_MODEL, IN_TN), lambda i, j: (layer, 0, j)),
            pl.BlockSpec((None, D_MODEL, DT_PAD), lambda i, j: (layer, 0, 0)),
        ],
        out_specs=[
            pl.BlockSpec((IN_TM, IN_TN), lambda i, j: (i, j)),
            pl.BlockSpec((IN_TM, DT_PAD), lambda i, j: (i, 0)),
        ],
        out_shape=[
            jax.ShapeDtypeStruct((m, U_MAIN), BF16),
            jax.ShapeDtypeStruct((m, DT_PAD), F32),
        ],
        scratch_shapes=[pltpu.VMEM((IN_TM, D_MODEL), BF16)],
        compiler_params=pltpu.CompilerParams(
            dimension_semantics=("arbitrary", "arbitrary"), vmem_limit_bytes=VMEM_LIMIT),
        name="inproj",
    )(x2, norm_w, w_main, w_dt)


R_WF, R_DF = 0, 1
R_WB, R_DB = 4, 5
R_COLF, R_COLB = 8, 9
R_EF, R_EB = 10, 11
R_ROWF, R_ROWB = 14, 15


def _softplus(v):
    return jnp.maximum(v, 0.0) + jnp.log1p(jnp.exp(-jnp.abs(v)))


def _chunk_scan(a, lane, reverse):
    n = a.shape[-1]
    out = a
    k = 1
    while k < SSD_CHUNK:
        if reverse:
            out = out + jnp.where(lane < SSD_CHUNK - k, pltpu.roll(out, n - k, axis=1), 0.0)
        else:
            out = out + jnp.where(lane >= k, pltpu.roll(out, k, axis=1), 0.0)
        k *= 2
    return out


def _ssd_kernel(z_ref, x_ref, b_ref, c_ref, dtf_ref, dtb_ref, cwx_ref, cwb_ref, cwc_ref,
                cbx_ref, cbb_ref, cbc_ref, hp_ref, dsk_ref, nw_ref, o_ref,
                shift_scr, xs_scr, bt_scr, cm_scr, y_scr, hm_scr, tm_scr, dec_scr, sf_scr, sb_scr):
    L = SEQ
    W = GROUP_WIDTH
    NS = SSD_STATE

    side = (SSD_CONV - 1) * SSD_CHUNK
    tap_t = lax.broadcasted_iota(jnp.int32, (side, CONV_WIN), 0)
    tap_j = lax.broadcasted_iota(jnp.int32, (side, CONV_WIN), 1)
    tap_k = tap_t // SSD_CHUNK
    tap_src = tap_t % SSD_CHUNK + jnp.where(tap_k >= SSD_CONV // 2, tap_k + 1, tap_k) - SSD_CONV // 2
    for variant in range(3):
        shift_scr[variant] = jnp.where(tap_j == tap_src + variant * CONV_HALO, 1.0, 0.0).astype(BF16)
    side_taps = [k for k in range(SSD_CONV) if k != SSD_CONV // 2]

    def conv(c, carry):
        r = pl.multiple_of(c * SSD_CHUNK, SSD_CHUNK)
        w0 = pl.multiple_of(jnp.clip(r - CONV_HALO, 0, L - CONV_WIN), CONV_HALO)
        variant = jnp.where(c == 0, 0, jnp.where(c == N_CHUNKS - 1, 2, 1))
        shift = shift_scr[variant]
        rows = pl.ds(r, SSD_CHUNK)

        def taps(win, centre, cw, cb):
            sh = _dot(shift, win)
            acc = cb + centre.astype(F32) * cw[SSD_CONV // 2:SSD_CONV // 2 + 1, :]
            for slot, k in enumerate(side_taps):
                acc = acc + sh[slot * SSD_CHUNK:(slot + 1) * SSD_CHUNK, :] * cw[k:k + 1, :]
            return _silu(acc)

        xs_scr[rows, :] = taps(x_ref[0, pl.ds(w0, CONV_WIN), :], x_ref[0, rows, :], cwx_ref[...], cbx_ref[...])
        win_bc = jnp.concatenate([b_ref[0, pl.ds(w0, CONV_WIN), :], c_ref[0, pl.ds(w0, CONV_WIN), :]], axis=1)
        mid_bc = jnp.concatenate([b_ref[0, rows, :], c_ref[0, rows, :]], axis=1)
        act_bc = taps(win_bc, mid_bc, jnp.concatenate([cwb_ref[...], cwc_ref[...]], axis=1),
                      jnp.concatenate([cbb_ref[...], cbc_ref[...]], axis=1))
        bt_scr[:, rows] = act_bc[:, 0:NS].T.astype(BF16)
        cm_scr[rows, :] = act_bc[:, NS:2 * NS].astype(BF16)
        return carry

    lax.fori_loop(0, N_CHUNKS, conv, 0, unroll=2)

    hp = hp_ref[0]
    a_f = -jnp.exp(hp[:, 0:1])
    a_b = -jnp.exp(hp[:, 1:2])
    dt_f = _softplus(dtf_ref[0] + hp[:, 2:3])
    dt_b = _softplus(dtb_ref[0] + hp[:, 3:4])
    da_f = dt_f * a_f
    da_b = dt_b * a_b
    lane = lax.broadcasted_iota(jnp.int32, (GROUP_HEADS, L), 1) % SSD_CHUNK
    cs_f = _chunk_scan(da_f, lane, False)
    rs_f = _chunk_scan(da_f, lane, True) - da_f
    cs_b = _chunk_scan(da_b, lane, False)
    ecs_b = cs_b - da_b
    rs_b = _chunk_scan(da_b, lane, True)

    def put(row, v):
        hm_scr[row * GROUP_HEADS:(row + 1) * GROUP_HEADS, :] = v

    def put_split(row_hi, v):
        hi = v.astype(BF16).astype(F32)
        put(row_hi, hi)
        put(row_hi + 2, (v - hi).astype(BF16).astype(F32))

    put_split(R_WF, jnp.exp(rs_f) * dt_f)
    put_split(R_DF, jnp.exp(cs_f))
    put_split(R_WB, jnp.exp(ecs_b) * dt_b)
    put_split(R_DB, jnp.exp(rs_b))
    put_split(R_EF, jnp.exp(cs_f + rs_f))
    put_split(R_EB, jnp.exp(ecs_b + rs_b))
    put(R_COLF, cs_f * LOG2E)
    put(R_COLB, ecs_b * LOG2E)
    put(R_ROWF, cs_f * LOG2E - jnp.log2(dt_f))
    put(R_ROWB, ecs_b * LOG2E + jnp.log2(dt_b))

    def to_time_major(c, carry):
        r = pl.multiple_of(c * SSD_CHUNK, SSD_CHUNK)
        tm_scr[pl.ds(r, SSD_CHUNK), :] = hm_scr[:, pl.ds(r, SSD_CHUNK)].T
        return carry

    lax.fori_loop(0, N_CHUNKS, to_time_major, 0)

    e_row = lax.broadcasted_iota(jnp.int32, (4 * GROUP_HEADS, 2 * W), 0) % (2 * GROUP_HEADS)
    e_col = lax.broadcasted_iota(jnp.int32, (4 * GROUP_HEADS, 2 * W), 1) // SSD_HEAD_DIM
    spread = jnp.where(e_row == e_col, 1.0, 0.0).astype(BF16)

    def expand(rows, first_row):
        cols = slice(first_row * GROUP_HEADS, (first_row + 4) * GROUP_HEADS)
        return _dot(tm_scr[rows, :][:, cols].astype(BF16), spread)

    dec_scr[...] = expand(pl.ds(0, N_CHUNKS, stride=SSD_CHUNK), R_EF)

    row_i = lax.broadcasted_iota(jnp.int32, (SSD_CHUNK, SSD_CHUNK), 0)
    col_i = lax.broadcasted_iota(jnp.int32, (SSD_CHUNK, SSD_CHUNK), 1)
    lower = col_i <= row_i
    upper = col_i >= row_i
    left = lax.broadcasted_iota(jnp.int32, (SSD_CHUNK, LANES), 1) < SSD_HEAD_DIM
    dsk = dsk_ref[0]
    nw = nw_ref[0]

    def hm_row(row, h, r):
        return hm_scr[row * GROUP_HEADS + h:row * GROUP_HEADS + h + 1, pl.ds(r, SSD_CHUNK)]

    def forward_part(c):
        r = pl.multiple_of(c * SSD_CHUNK, SSD_CHUNK)
        rows = pl.ds(r, SSD_CHUNK)
        xs_c = xs_scr[rows, :]
        bt_c = bt_scr[:, rows]
        cm_c = cm_scr[rows, :]
        g = _dot(cm_c, bt_c)
        xs_b = xs_c.astype(BF16)
        colf = tm_scr[rows, R_COLF * GROUP_HEADS:(R_COLF + 1) * GROUP_HEADS]
        colb = tm_scr[rows, R_COLB * GROUP_HEADS:(R_COLB + 1) * GROUP_HEADS]
        y_pairs = []
        for hp_i in range(GROUP_HEADS // 2):
            ms = []
            for h in (2 * hp_i, 2 * hp_i + 1):
                seg_f = jnp.where(lower, colf[:, h:h + 1] - hm_row(R_ROWF, h, r), NEG)
                seg_b = jnp.where(upper, hm_row(R_ROWB, h, r) - colb[:, h:h + 1], NEG)
                ms.append((g * (jnp.exp2(seg_f) + jnp.exp2(seg_b))).astype(BF16))
            xp = xs_b[:, hp_i * LANES:(hp_i + 1) * LANES]
            zero = jnp.zeros_like(xp)
            rhs = jnp.concatenate([jnp.where(left, xp, zero), jnp.where(left, zero, xp)], axis=0)
            y_pairs.append(_dot(jnp.concatenate(ms, axis=1), rhs))
        y_diag = jnp.concatenate(y_pairs, axis=1)
        ex = expand(rows, R_WF)
        s_f = sf_scr[...]
        y_off = _dot(cm_c, s_f.astype(BF16)) * ex[:, W:2 * W]
        xw = (xs_c * ex[:, 0:W]).astype(BF16)
        sf_scr[...] = s_f * dec_scr[pl.ds(c, 1), 0:W] + _dot(bt_c, xw)
        return y_diag + y_off

    def backward_part(c):
        r = pl.multiple_of(c * SSD_CHUNK, SSD_CHUNK)
        rows = pl.ds(r, SSD_CHUNK)
        ex = expand(rows, R_WB)
        s_b = sb_scr[...]
        y_off = _dot(cm_scr[rows, :], s_b.astype(BF16)) * ex[:, W:2 * W]
        xw = (xs_scr[rows, :] * ex[:, 0:W]).astype(BF16)
        sb_scr[...] = s_b * dec_scr[pl.ds(c, 1), W:2 * W] + _dot(bt_scr[:, rows], xw)
        return y_off

    def finalize(c, y):
        r = pl.multiple_of(c * SSD_CHUNK, SSD_CHUNK)
        rows = pl.ds(r, SSD_CHUNK)
        y = y + y_scr[rows, :] + dsk * xs_scr[rows, :]
        y = y * _silu(z_ref[0, rows, :].astype(F32))
        y = y * lax.rsqrt(jnp.mean(y * y, axis=-1, keepdims=True) + EPS) * nw
        o_ref[0, rows, :] = y.astype(o_ref.dtype)

    sf_scr[...] = jnp.zeros((NS, W), F32)
    sb_scr[...] = jnp.zeros((NS, W), F32)
    half = N_CHUNKS // 2

    def first_half(i, carry):
        cb = N_CHUNKS - 1 - i
        y_scr[pl.ds(pl.multiple_of(i * SSD_CHUNK, SSD_CHUNK), SSD_CHUNK), :] = forward_part(i)
        y_scr[pl.ds(pl.multiple_of(cb * SSD_CHUNK, SSD_CHUNK), SSD_CHUNK), :] = backward_part(cb)
        return carry

    def second_half(i, carry):
        cb = N_CHUNKS - 1 - i
        finalize(i, forward_part(i))
        finalize(cb, backward_part(cb))
        return carry

    lax.fori_loop(0, half, first_half, 0, unroll=2)
    lax.fori_loop(half, N_CHUNKS, second_half, 0, unroll=2)


def _ssd(u3, dt_t, conv_w, conv_b, head_params, dsk_row, norm_w):
    b = u3.shape[0]
    L = SEQ
    W = GROUP_WIDTH
    NS = SSD_STATE
    G = SSD_GROUPS
    xblk = COL_X // W
    bblk = COL_B // NS
    cblk = COL_C // NS
    in_specs = [
        pl.BlockSpec((1, L, W), lambda i, g: (i, 0, g)),
        pl.BlockSpec((1, L, W), lambda i, g: (i, 0, xblk + g)),
        pl.BlockSpec((1, L, NS), lambda i, g: (i, 0, bblk + g)),
        pl.BlockSpec((1, L, NS), lambda i, g: (i, 0, cblk + g)),
        pl.BlockSpec((1, GROUP_HEADS, L), lambda i, g: (i, g, 0)),
        pl.BlockSpec((1, GROUP_HEADS, L), lambda i, g: (i, G + g, 0)),
        pl.BlockSpec((SSD_CONV, W), lambda i, g: (0, g)),
        pl.BlockSpec((SSD_CONV, NS), lambda i, g: (0, SSD_WIDTH // NS + g)),
        pl.BlockSpec((SSD_CONV, NS), lambda i, g: (0, SSD_WIDTH // NS + G + g)),
        pl.BlockSpec((1, W), lambda i, g: (0, g)),
        pl.BlockSpec((1, NS), lambda i, g: (0, SSD_WIDTH // NS + g)),
        pl.BlockSpec((1, NS), lambda i, g: (0, SSD_WIDTH // NS + G + g)),
        pl.BlockSpec((1, GROUP_HEADS, 8), lambda i, g: (g, 0, 0)),
        pl.BlockSpec((1, 1, W), lambda i, g: (g, 0, 0)),
        pl.BlockSpec((1, 1, W), lambda i, g: (g, 0, 0)),
    ]
    return pl.pallas_call(
        _ssd_kernel,
        grid=(b, G),
        in_specs=in_specs,
        out_specs=pl.BlockSpec((1, L, W), lambda i, g: (i, 0, g)),
        out_shape=jax.ShapeDtypeStruct((b, L, SSD_WIDTH), BF16),
        scratch_shapes=[
            pltpu.VMEM((3, (SSD_CONV - 1) * SSD_CHUNK, CONV_WIN), BF16),
            pltpu.VMEM((L, W), F32),
            pltpu.VMEM((NS, L), BF16),
            pltpu.VMEM((L, NS), BF16),
            pltpu.VMEM((L, W), F32),
            pltpu.VMEM((LANES, L), F32),
            pltpu.VMEM((L, LANES), F32),
            pltpu.VMEM((N_CHUNKS, 2 * W), F32),
            pltpu.VMEM((NS, W), F32),
            pltpu.VMEM((NS, W), F32),
        ],
        compiler_params=pltpu.CompilerParams(
            dimension_semantics=("arbitrary", "arbitrary"), vmem_limit_bytes=VMEM_LIMIT),
        name="ssd",
    )(u3, u3, u3, u3, dt_t, dt_t, conv_w, conv_w, conv_w, conv_b, conv_b, conv_b,
      head_params, dsk_row, norm_w)


def _group_mean_sq(v, seg):
    sq = v * v
    hi = sq.astype(BF16)
    lo = (sq - hi.astype(F32)).astype(BF16)
    return (_dot(hi, seg) + _dot(lo, seg)) * (1.0 / 64.0)


def _seg_matrix():
    r = lax.broadcasted_iota(jnp.int32, (LANES, LANES), 0) // 64
    c = lax.broadcasted_iota(jnp.int32, (LANES, LANES), 1) // 64
    return jnp.where(r == c, 1.0, 0.0).astype(BF16)


DIFF_TQ = 256
DIFF_HEADS_PER_STEP = 2


def _diff_kernel(lam_init, q_ref, k_ref, v_ref, g_ref, qc_ref, qs_ref, kc_ref, ks_ref, lam_ref,
                 sub_ref, o_ref, q_scr, k_scr, vt_scr, sa_scr, sb_scr):
    L = SEQ
    seg = _seg_matrix()
    comp0 = lax.broadcasted_iota(jnp.int32, (DIFF_TQ, LANES), 1) < 64
    p_row = lax.broadcasted_iota(jnp.int32, (LANES, LANES), 0)
    p_col = lax.broadcasted_iota(jnp.int32, (LANES, LANES), 1)
    swap = jnp.where((p_row ^ 32) == p_col, 1.0, 0.0).astype(BF16)

    def norm_rope(ref, r, lanes, cw_ref, sw_ref):
        vb = ref[0, pl.ds(r, DIFF_TQ), lanes]
        v = vb.astype(F32)
        rinv = lax.rsqrt(_group_mean_sq(v, seg) + EPS)
        return rinv * (v * cw_ref[pl.ds(r, DIFF_TQ), :] + _dot(vb, swap) * sw_ref[pl.ds(r, DIFF_TQ), :])

    n_blk = L // DIFF_TQ
    n_items = DIFF_HEADS_PER_STEP * n_blk

    def item(j):
        hd = j // n_blk
        r = pl.multiple_of((j % n_blk) * DIFF_TQ, DIFF_TQ)
        return hd, r, pl.ds(pl.multiple_of(hd * LANES, LANES), LANES)

    for hd in range(DIFF_HEADS_PER_STEP):
        vt_scr[hd, LANES:LANES + ONES_ROWS, :] = jnp.ones((ONES_ROWS, L), BF16)

    def prep(j, carry):
        hd, r, lanes = item(j)
        q_scr[hd, pl.ds(r, DIFF_TQ), :] = norm_rope(q_ref, r, lanes, qc_ref, qs_ref).astype(BF16)
        kn = norm_rope(k_ref, r, lanes, kc_ref, ks_ref)
        k_scr[hd, 0, pl.ds(r, DIFF_TQ), :] = jnp.where(comp0, kn, 0.0).astype(BF16)
        k_scr[hd, 1, pl.ds(r, DIFF_TQ), :] = jnp.where(comp0, 0.0, kn).astype(BF16)
        vt_scr[hd, 0:LANES, pl.ds(r, DIFF_TQ)] = v_ref[0, pl.ds(r, DIFF_TQ), lanes].astype(F32).T.astype(BF16)
        return carry

    lax.fori_loop(0, n_items, prep, 0, unroll=2)

    lam = lam_ref[...]
    lam_full = (jnp.exp(jnp.sum(lam[0:1] * lam[1:2], axis=-1, keepdims=True))
                - jnp.exp(jnp.sum(lam[2:3] * lam[3:4], axis=-1, keepdims=True)) + lam_init)

    def scores(j, s_ref):
        hd, r, _ = item(j)
        qb = q_scr[hd, pl.ds(r, DIFF_TQ), :]
        for c in range(2):
            s_ref[c] = _dot_nt(k_scr[hd, c], qb)

    def finish(j, s_ref):
        hd, r, lanes = item(j)
        parts = []
        for c in range(2):
            s = s_ref[c]
            e = jnp.exp2(s - jnp.max(s, axis=0, keepdims=True)).astype(BF16)
            acc = _dot(vt_scr[hd], e)
            parts.append(acc[0:LANES, :] * (1.0 / acc[LANES:LANES + 1, :]))
        ot = parts[0] - lam_full * parts[1]
        o = ot.T
        o = o * lax.rsqrt(jnp.mean(o * o, axis=-1, keepdims=True) + EPS) * sub_ref[...]
        o = o * (1.0 - lam_init) * _silu(g_ref[0, pl.ds(r, DIFF_TQ), lanes].astype(F32))
        o_ref[0, pl.ds(r, DIFF_TQ), lanes] = o.astype(o_ref.dtype)

    scores(0, sa_scr)

    def pair(i, carry):
        scores(2 * i + 1, sb_scr)
        finish(2 * i, sa_scr)
        scores(2 * i + 2, sa_scr)
        finish(2 * i + 1, sb_scr)
        return carry

    lax.fori_loop(0, n_items // 2 - 1, pair, 0)
    scores(n_items - 1, sb_scr)
    finish(n_items - 2, sa_scr)
    finish(n_items - 1, sb_scr)


def _diff(u3, rope_tabs, lam, subln_w, lam_init):
    b = u3.shape[0]
    L = SEQ
    step_w = DIFF_HEADS_PER_STEP * LANES
    base = COL_DIFF // step_w
    nb = DIFF_WIDTH // step_w

    def spec(part):
        return pl.BlockSpec((1, L, step_w), lambda i, h: (i, 0, base + part * nb + h))

    return pl.pallas_call(
        functools.partial(_diff_kernel, lam_init),
        grid=(b, DIFF_HEADS // DIFF_HEADS_PER_STEP),
        in_specs=[
            spec(0), spec(1), spec(2), spec(3),
            pl.BlockSpec((L, LANES), lambda i, h: (0, 0)),
            pl.BlockSpec((L, LANES), lambda i, h: (0, 0)),
            pl.BlockSpec((L, LANES), lambda i, h: (0, 0)),
            pl.BlockSpec((L, LANES), lambda i, h: (0, 0)),
            pl.BlockSpec((4, DIFF_HEAD_DIM), lambda i, h: (0, 0)),
            pl.BlockSpec((1, LANES), lambda i, h: (0, 0)),
        ],
        out_specs=pl.BlockSpec((1, L, step_w), lambda i, h: (i, 0, h)),
        out_shape=jax.ShapeDtypeStruct((b, L, DIFF_WIDTH), BF16),
        scratch_shapes=[
            pltpu.VMEM((DIFF_HEADS_PER_STEP, L, LANES), BF16),
            pltpu.VMEM((DIFF_HEADS_PER_STEP, 2, L, LANES), BF16),
            pltpu.VMEM((DIFF_HEADS_PER_STEP, LANES + ONES_ROWS, L), BF16),
            pltpu.VMEM((2, L, DIFF_TQ), F32),
            pltpu.VMEM((2, L, DIFF_TQ), F32),
        ],
        compiler_params=pltpu.CompilerParams(
            dimension_semantics=("arbitrary", "arbitrary"), vmem_limit_bytes=VMEM_LIMIT),
        name="diff_attn",
    )(u3, u3, u3, u3, *rope_tabs, lam, subln_w)


def _na_block_geometry(qb):
    r0 = qb * NA_QROWS
    kb = min(max(r0 - NA_KH // 2, 0), ROWS - NA_BAND)
    return r0, kb


def _na_bias_type(qb):
    return 0 if qb == 0 else (2 if qb == NA_NBLK - 1 else 1)


def _na_kernel(q_ref, k_ref, v_ref, g_ref, qkw_ref, bias_ref, o_ref, q_scr, k_scr, vt_scr, s_scr):
    L = SEQ
    seg = _seg_matrix()
    head0_blk = lax.broadcasted_iota(jnp.int32, (NA_QBLK, LANES), 1) < 64
    q_scale = (NA_HEAD_DIM ** -0.5) * LOG2E
    for h in range(2):
        vt_scr[h, NA_HEAD_DIM:NA_HEAD_DIM + ONES_ROWS, :] = jnp.ones((ONES_ROWS, L), BF16)

    def norm(ref, r, w_row):
        v = ref[0, pl.ds(r, NA_QBLK), :].astype(F32)
        return v * lax.rsqrt(_group_mean_sq(v, seg) + EPS) * w_row

    def prep(i, carry):
        r = pl.multiple_of(i * NA_QBLK, NA_QBLK)
        q_scr[pl.ds(r, NA_QBLK), :] = (norm(q_ref, r, qkw_ref[0:1, :]) * q_scale).astype(BF16)
        kn = norm(k_ref, r, qkw_ref[1:2, :])
        k_scr[0, pl.ds(r, NA_QBLK), :] = jnp.where(head0_blk, kn, 0.0).astype(BF16)
        k_scr[1, pl.ds(r, NA_QBLK), :] = jnp.where(head0_blk, 0.0, kn).astype(BF16)
        vt = v_ref[0, pl.ds(r, NA_QBLK), :].astype(F32).T.astype(BF16)
        for h in range(2):
            vt_scr[h, 0:NA_HEAD_DIM, pl.ds(r, NA_QBLK)] = vt[h * NA_HEAD_DIM:(h + 1) * NA_HEAD_DIM, :]
        return carry

    lax.fori_loop(0, L // NA_QBLK, prep, 0, unroll=2)

    def slices(qb):
        r0, kb = _na_block_geometry(qb)
        return (slice(r0 * GRID_W, r0 * GRID_W + NA_QBLK), slice(kb * GRID_W, kb * GRID_W + NA_KBLK))

    def scores(qb):
        qs, ks = slices(qb)
        q_blk = q_scr[qs, :]
        for h in range(2):
            s_scr[qb % 2, h] = _dot_nt(k_scr[h, ks, :], q_blk)

    def finish(qb):
        qs, ks = slices(qb)
        t = _na_bias_type(qb)
        outs = []
        for h in range(2):
            s = s_scr[qb % 2, h] + bias_ref[h, t]
            e = jnp.exp2(s - jnp.max(s, axis=0, keepdims=True)).astype(BF16)
            acc = _dot(vt_scr[h, :, ks], e)
            outs.append(acc[0:NA_HEAD_DIM, :] * (1.0 / acc[NA_HEAD_DIM:NA_HEAD_DIM + 1, :]))
        o = jnp.concatenate(outs, axis=0).T
        o = o * _silu(g_ref[0, qs, :].astype(F32))
        o_ref[0, qs, :] = o.astype(o_ref.dtype)

    scores(0)
    for qb in range(NA_NBLK):
        if qb + 1 < NA_NBLK:
            scores(qb + 1)
        finish(qb)


def _na(u3, qk_w, bias, layer):
    b = u3.shape[0]
    L = SEQ
    base = COL_NA // LANES
    nb = NA_WIDTH // LANES

    def spec(part):
        return pl.BlockSpec((1, L, LANES), lambda hp, i: (i, 0, base + part * nb + hp))

    return pl.pallas_call(
        _na_kernel,
        grid=(NA_HEADS // 2, b),
        in_specs=[
            spec(0), spec(1), spec(2), spec(3),
            pl.BlockSpec((2, LANES), lambda hp, i: (0, 0)),
            pl.BlockSpec((None, 2, 3, NA_KBLK, NA_QBLK), lambda hp, i: (layer, hp, 0, 0, 0)),
        ],
        out_specs=pl.BlockSpec((1, L, LANES), lambda hp, i: (i, 0, hp)),
        out_shape=jax.ShapeDtypeStruct((b, L, NA_WIDTH), BF16),
        scratch_shapes=[
            pltpu.VMEM((L, LANES), BF16),
            pltpu.VMEM((2, L, LANES), BF16),
            pltpu.VMEM((2, NA_HEAD_DIM + ONES_ROWS, L), BF16),
            pltpu.VMEM((2, 2, NA_KBLK, NA_QBLK), F32),
        ],
        compiler_params=pltpu.CompilerParams(
            dimension_semantics=("arbitrary", "arbitrary"), vmem_limit_bytes=VMEM_LIMIT),
        name="na_attn",
    )(u3, u3, u3, u3, qk_w, bias)


def _na_bias(rpb):
    c = np.arange(GRID_W)
    col_start = np.clip(c - NA_KW // 2, 0, GRID_W - NA_KW)
    col_ok = (c[None, :] >= col_start[:, None]) & (c[None, :] < col_start[:, None] + NA_KW)
    lead = rpb.shape[:-2]
    pad = GRID_W - NA_KW
    rev = jnp.pad(rpb.astype(F32)[..., ::-1] * LOG2E, [(0, 0)] * (rpb.ndim - 1) + [(pad, pad)])
    tiles = jnp.stack([rev[..., GRID_W - 1 - k:2 * GRID_W - 1 - k] for k in range(GRID_W)], axis=-2)
    tiles = jnp.where(col_ok.T, tiles, NEG)
    tiles = jnp.concatenate([tiles, tiles], axis=-1)
    n_dr = 2 * NA_KH - 1
    tiles = tiles.reshape((-1, n_dr, GRID_W, LANES))
    out = pl.pallas_call(
        _na_bias_kernel,
        grid=(tiles.shape[0],),
        in_specs=[pl.BlockSpec((1, n_dr, GRID_W, LANES), lambda n: (n, 0, 0, 0))],
        out_specs=pl.BlockSpec((1, 3, NA_KBLK, NA_QBLK), lambda n: (n, 0, 0, 0)),
        out_shape=jax.ShapeDtypeStruct((tiles.shape[0], 3, NA_KBLK, NA_QBLK), F32),
        compiler_params=pltpu.CompilerParams(
            dimension_semantics=("arbitrary",), vmem_limit_bytes=VMEM_LIMIT),
        name="na_bias",
    )(tiles)
    return out.reshape(lead + (3, NA_KBLK, NA_QBLK))


def _na_bias_kernel(t_ref, o_ref):
    left = lax.broadcasted_iota(jnp.int32, (GRID_W, LANES), 1) < GRID_W
    masked = jnp.full((GRID_W, LANES), NEG, F32)

    def tile(qb, j, i):
        r0, kb = _na_block_geometry(qb)
        r, rk = r0 + i, kb + j
        rs = min(max(r - NA_KH // 2, 0), ROWS - NA_KH)
        return t_ref[0, rk - r + NA_KH - 1] if rs <= rk < rs + NA_KH else masked

    for t, qb in enumerate((0, 1, NA_NBLK - 1)):
        for j in range(NA_BAND):
            for p in range(NA_QROWS // 2):
                o_ref[0, t, j * GRID_W:(j + 1) * GRID_W, p * LANES:(p + 1) * LANES] = jnp.where(
                    left, tile(qb, j, 2 * p), tile(qb, j, 2 * p + 1))


OUT_TM = 512


def _outproj_kernel(x_ref, ys_ref, yd_ref, yn_ref, w_ref, o_ref):
    acc = _dot(ys_ref[...], w_ref[0:SSD_WIDTH, :])
    acc = acc + _dot(yd_ref[...], w_ref[SSD_WIDTH:SSD_WIDTH + DIFF_WIDTH, :])
    acc = acc + _dot(yn_ref[...], w_ref[SSD_WIDTH + DIFF_WIDTH:MIX_WIDTH, :])
    o_ref[...] = x_ref[...] + acc


def _outproj(x2, y_ssd, y_diff, y_na, w_out):
    m = x2.shape[0]
    return pl.pallas_call(
        _outproj_kernel,
        grid=(m // OUT_TM,),
        in_specs=[
            pl.BlockSpec((OUT_TM, D_MODEL), lambda i: (i, 0)),
            pl.BlockSpec((OUT_TM, SSD_WIDTH), lambda i: (i, 0)),
            pl.BlockSpec((OUT_TM, DIFF_WIDTH), lambda i: (i, 0)),
            pl.BlockSpec((OUT_TM, NA_WIDTH), lambda i: (i, 0)),
            pl.BlockSpec((MIX_WIDTH, D_MODEL), lambda i: (0, 0)),
        ],
        out_specs=pl.BlockSpec((OUT_TM, D_MODEL), lambda i: (i, 0)),
        out_shape=jax.ShapeDtypeStruct((m, D_MODEL), F32),
        compiler_params=pltpu.CompilerParams(
            dimension_semantics=("arbitrary",), vmem_limit_bytes=VMEM_LIMIT),
        name="outproj",
    )(x2, y_ssd, y_diff, y_na, w_out)


def _rope_tables():
    inv_freq = ROPE_THETA ** (-jnp.arange(0, DIFF_HEAD_DIM, 2, dtype=F32) / DIFF_HEAD_DIM)
    ang = jnp.arange(SEQ, dtype=F32)[:, None] * inv_freq[None, :]
    cos, sin = jnp.cos(ang), jnp.sin(ang)
    cos_t = jnp.concatenate([cos, cos, cos, cos], axis=1)
    sin_t = jnp.concatenate([-sin, sin, -sin, sin], axis=1)
    return cos_t, sin_t


def kernel(x, norm_w, w_in, conv_w, conv_b, a_log, dt_bias, d_skip, ssd_norm_w, diff_qk_norm,
           diff_lambda, diff_subln, na_qk_norm, na_rpb, w_out):
    b, L, d = x.shape
    assert (L, d) == (SEQ, D_MODEL)
    depth = w_in.shape[0]
    cos_t, sin_t = _rope_tables()
    na_bias = _na_bias(na_rpb)
    dt_lo = SSD_WIDTH + SSD_XBC
    dt_hi = dt_lo + 2 * SSD_HEADS
    w_main = jnp.concatenate([w_in[:, :, :dt_lo], w_in[:, :, dt_hi:]], axis=2).astype(BF16)
    w_dt = jnp.pad(w_in[:, :, dt_lo:dt_hi], ((0, 0), (0, 0), (0, DT_PAD - 2 * SSD_HEADS))).astype(BF16)
    x2 = x.reshape(b * L, d)
    for i in range(depth):
        lam_init = 0.8 - 0.6 * math.exp(-0.3 * i)
        u, dt = _inproj(x2, norm_w[i][None, :], w_main, w_dt, i)
        u3 = u.reshape(b, L, U_MAIN)
        dt_t = jnp.swapaxes(dt.reshape(b, L, DT_PAD)[:, :, :2 * SSD_HEADS], 1, 2)

        hp = jnp.stack([a_log[i][0], a_log[i][1], dt_bias[i][0], dt_bias[i][1]], axis=-1)
        hp = jnp.pad(hp, ((0, 0), (0, 4))).reshape(SSD_GROUPS, GROUP_HEADS, 8)
        dsk_row = jnp.repeat(d_skip[i][0] + d_skip[i][1], SSD_HEAD_DIM).reshape(SSD_GROUPS, 1, GROUP_WIDTH)
        y_ssd = _ssd(u3, dt_t, conv_w[i], conv_b[i][None, :], hp, dsk_row,
                     ssd_norm_w[i].reshape(SSD_GROUPS, 1, GROUP_WIDTH))

        half = DIFF_HEAD_DIM // 2
        qk_w = jnp.tile(diff_qk_norm[i], (1, 2))
        qk_w_sw = jnp.tile(jnp.concatenate([diff_qk_norm[i][:, half:], diff_qk_norm[i][:, :half]], axis=1), (1, 2))
        q_scale = (DIFF_HEAD_DIM ** -0.5) * LOG2E
        rope_tabs = (cos_t * (qk_w[0:1] * q_scale), sin_t * (qk_w_sw[0:1] * q_scale),
                     cos_t * qk_w[1:2], sin_t * qk_w_sw[1:2])
        y_diff = _diff(u3, rope_tabs, diff_lambda[i], diff_subln[i][None, :], lam_init)

        na_w = jnp.concatenate([na_qk_norm[i], na_qk_norm[i]], axis=1)
        y_na = _na(u3, na_w, na_bias, i)

        x2 = _outproj(x2, y_ssd.reshape(b * L, SSD_WIDTH), y_diff.reshape(b * L, DIFF_WIDTH),
                      y_na.reshape(b * L, NA_WIDTH), w_out[i].astype(BF16))
    return x2.reshape(b, L, d)
```

```python
import functools
import math

import numpy as np
import jax
import jax.numpy as jnp
from jax import lax
from jax.experimental import pallas as pl
from jax.experimental.pallas import tpu as pltpu

F32 = jnp.float32
BF16 = jnp.bfloat16

D_MODEL = 1024
SEQ = 2048
GRID_W = 64
ROWS = SEQ // GRID_W
SSD_WIDTH = 1024
SSD_HEAD_DIM = 64
SSD_HEADS = 16
SSD_GROUPS = 2
SSD_STATE = 128
SSD_CONV = 5
SSD_CHUNK = 128
SSD_XBC = SSD_WIDTH + 2 * SSD_GROUPS * SSD_STATE
GROUP_HEADS = SSD_HEADS // SSD_GROUPS
GROUP_WIDTH = SSD_WIDTH // SSD_GROUPS
N_CHUNKS = SEQ // SSD_CHUNK
DIFF_WIDTH = 512
DIFF_HEAD_DIM = 64
DIFF_HEADS = 4
NA_WIDTH = 512
NA_HEAD_DIM = 64
NA_HEADS = 8
NA_KH = 8
NA_KW = 16
MIX_WIDTH = SSD_WIDTH + DIFF_WIDTH + NA_WIDTH
ROPE_THETA = 10000.0
EPS = 1e-6
LANES = 128
CONV_HALO = 16
CONV_WIN = SSD_CHUNK + 2 * CONV_HALO

U_MAIN = SSD_WIDTH + SSD_XBC + 4 * DIFF_WIDTH + 4 * NA_WIDTH
COL_Z = 0
COL_X = SSD_WIDTH
COL_B = COL_X + SSD_WIDTH
COL_C = COL_B + SSD_GROUPS * SSD_STATE
COL_DIFF = COL_C + SSD_GROUPS * SSD_STATE
COL_NA = COL_DIFF + 4 * DIFF_WIDTH
DT_PAD = LANES

NA_QROWS = 4
NA_BAND = 12
NA_QBLK = NA_QROWS * GRID_W
NA_KBLK = NA_BAND * GRID_W
NA_NBLK = ROWS // NA_QROWS
NEG = -1e30
LOG2E = math.log2(math.e)
ONES_ROWS = 16

VMEM_LIMIT = 56 * 1024 * 1024


def _silu(v):
    return v * (1.0 / (1.0 + jnp.exp(-v)))


def _dot(a, b):
    return jnp.dot(a, b, preferred_element_type=F32)


def _dot_nt(a, b):
    return lax.dot_general(a, b, (((1,), (1,)), ((), ())), preferred_element_type=F32)


def _dot_tn(a, b):
    return lax.dot_general(a, b, (((0,), (0,)), ((), ())), preferred_element_type=F32)


IN_TM = 1024
IN_TN = 3328


def _inproj_kernel(x_ref, nw_ref, w_ref, wdt_ref, u_ref, dt_ref, h_scr):
    @pl.when(pl.program_id(1) == 0)
    def _():
        x = x_ref[...]
        ms = jnp.mean(x * x, axis=-1, keepdims=True)
        h = (x * lax.rsqrt(ms + EPS) * nw_ref[...]).astype(BF16)
        h_scr[...] = h
        dt_ref[...] = _dot(h, wdt_ref[...])

    u_ref[...] = _dot(h_scr[...], w_ref[...]).astype(u_ref.dtype)


def _inproj(x2, norm_w, w_main, w_dt, layer):
    m = x2.shape[0]
    return pl.pallas_call(
        _inproj_kernel,
        grid=(m // IN_TM, U_MAIN // IN_TN),
        in_specs=[
            pl.BlockSpec((IN_TM, D_MODEL), lambda i, j: (i, 0)),
            pl.BlockSpec((1, D_MODEL), lambda i, j: (0, 0)),
            pl.BlockSpec((None, D_MODEL, IN_TN), lambda i, j: (layer, 0, j)),
            pl.BlockSpec((None, D_MODEL, DT_PAD), lambda i, j: (layer, 0, 0)),
        ],
        out_specs=[
            pl.BlockSpec((IN_TM, IN_TN), lambda i, j: (i, j)),
            pl.BlockSpec((IN_TM, DT_PAD), lambda i, j: (i, 0)),
        ],
        out_shape=[
            jax.ShapeDtypeStruct((m, U_MAIN), BF16),
            jax.ShapeDtypeStruct((m, DT_PAD), F32),
        ],
        scratch_shapes=[pltpu.VMEM((IN_TM, D_MODEL), BF16)],
        compiler_params=pltpu.CompilerParams(
            dimension_semantics=("arbitrary", "arbitrary"), vmem_limit_bytes=VMEM_LIMIT),
        name="inproj",
    )(x2, norm_w, w_main, w_dt)


R_WF, R_DF = 0, 1
R_WB, R_DB = 4, 5
R_COLF, R_COLB = 8, 9
R_EF, R_EB = 10, 11
R_ROWF, R_ROWB = 14, 15


def _softplus(v):
    return jnp.maximum(v, 0.0) + jnp.log1p(jnp.exp(-jnp.abs(v)))


def _chunk_scan(a, lane, reverse):
    n = a.shape[-1]
    out = a
    k = 1
    while k < SSD_CHUNK:
        if reverse:
            out = out + jnp.where(lane < SSD_CHUNK - k, pltpu.roll(out, n - k, axis=1), 0.0)
        else:
            out = out + jnp.where(lane >= k, pltpu.roll(out, k, axis=1), 0.0)
        k *= 2
    return out


def _ssd_kernel(z_ref, x_ref, b_ref, c_ref, dtf_ref, dtb_ref, cwx_ref, cwb_ref, cwc_ref,
                cbx_ref, cbb_ref, cbc_ref, hp_ref, dsk_ref, nw_ref, o_ref,
                shift_scr, xs_scr, bt_scr, cm_scr, y_scr, hm_scr, tm_scr, dec_scr, sf_scr, sb_scr):
    L = SEQ
    W = GROUP_WIDTH
    NS = SSD_STATE

    side = (SSD_CONV - 1) * SSD_CHUNK
    tap_t = lax.broadcasted_iota(jnp.int32, (side, CONV_WIN), 0)
    tap_j = lax.broadcasted_iota(jnp.int32, (side, CONV_WIN), 1)
    tap_k = tap_t // SSD_CHUNK
    tap_src = tap_t % SSD_CHUNK + jnp.where(tap_k >= SSD_CONV // 2, tap_k + 1, tap_k) - SSD_CONV // 2
    for variant in range(3):
        shift_scr[variant] = jnp.where(tap_j == tap_src + variant * CONV_HALO, 1.0, 0.0).astype(BF16)
    side_taps = [k for k in range(SSD_CONV) if k != SSD_CONV // 2]

    def conv(c, carry):
        r = pl.multiple_of(c * SSD_CHUNK, SSD_CHUNK)
        w0 = pl.multiple_of(jnp.clip(r - CONV_HALO, 0, L - CONV_WIN), CONV_HALO)
        variant = jnp.where(c == 0, 0, jnp.where(c == N_CHUNKS - 1, 2, 1))
        shift = shift_scr[variant]
        rows = pl.ds(r, SSD_CHUNK)

        def taps(win, centre, cw, cb):
            sh = _dot(shift, win)
            acc = cb + centre.astype(F32) * cw[SSD_CONV // 2:SSD_CONV // 2 + 1, :]
            for slot, k in enumerate(side_taps):
                acc = acc + sh[slot * SSD_CHUNK:(slot + 1) * SSD_CHUNK, :] * cw[k:k + 1, :]
            return _silu(acc)

        xs_scr[rows, :] = taps(x_ref[0, pl.ds(w0, CONV_WIN), :], x_ref[0, rows, :], cwx_ref[...], cbx_ref[...])
        win_bc = jnp.concatenate([b_ref[0, pl.ds(w0, CONV_WIN), :], c_ref[0, pl.ds(w0, CONV_WIN), :]], axis=1)
        mid_bc = jnp.concatenate([b_ref[0, rows, :], c_ref[0, rows, :]], axis=1)
        act_bc = taps(win_bc, mid_bc, jnp.concatenate([cwb_ref[...], cwc_ref[...]], axis=1),
                      jnp.concatenate([cbb_ref[...], cbc_ref[...]], axis=1))
        bt_scr[:, rows] = act_bc[:, 0:NS].T.astype(BF16)
        cm_scr[rows, :] = act_bc[:, NS:2 * NS].astype(BF16)
        return carry

    lax.fori_loop(0, N_CHUNKS, conv, 0, unroll=4)

    hp = hp_ref[0]
    a_f = -jnp.exp(hp[:, 0:1])
    a_b = -jnp.exp(hp[:, 1:2])
    dt_f = _softplus(dtf_ref[0] + hp[:, 2:3])
    dt_b = _softplus(dtb_ref[0] + hp[:, 3:4])
    da_f = dt_f * a_f
    da_b = dt_b * a_b
    lane = lax.broadcasted_iota(jnp.int32, (GROUP_HEADS, L), 1) % SSD_CHUNK
    cs_f = _chunk_scan(da_f, lane, False)
    rs_f = _chunk_scan(da_f, lane, True) - da_f
    cs_b = _chunk_scan(da_b, lane, False)
    ecs_b = cs_b - da_b
    rs_b = _chunk_scan(da_b, lane, True)

    def put(row, v):
        hm_scr[row * GROUP_HEADS:(row + 1) * GROUP_HEADS, :] = v

    def put_split(row_hi, v):
        hi = v.astype(BF16).astype(F32)
        put(row_hi, hi)
        put(row_hi + 2, (v - hi).astype(BF16).astype(F32))

    put_split(R_WF, jnp.exp(rs_f) * dt_f)
    put_split(R_DF, jnp.exp(cs_f))
    put_split(R_WB, jnp.exp(ecs_b) * dt_b)
    put_split(R_DB, jnp.exp(rs_b))
    put_split(R_EF, jnp.exp(cs_f + rs_f))
    put_split(R_EB, jnp.exp(ecs_b + rs_b))
    put(R_COLF, cs_f * LOG2E)
    put(R_COLB, ecs_b * LOG2E)
    put(R_ROWF, cs_f * LOG2E - jnp.log2(dt_f))
    put(R_ROWB, ecs_b * LOG2E + jnp.log2(dt_b))

    def to_time_major(c, carry):
        r = pl.multiple_of(c * SSD_CHUNK, SSD_CHUNK)
        tm_scr[pl.ds(r, SSD_CHUNK), :] = hm_scr[:, pl.ds(r, SSD_CHUNK)].T
        return carry

    lax.fori_loop(0, N_CHUNKS, to_time_major, 0, unroll=4)

    e_row = lax.broadcasted_iota(jnp.int32, (4 * GROUP_HEADS, 2 * W), 0) % (2 * GROUP_HEADS)
    e_col = lax.broadcasted_iota(jnp.int32, (4 * GROUP_HEADS, 2 * W), 1) // SSD_HEAD_DIM
    spread = jnp.where(e_row == e_col, 1.0, 0.0).astype(BF16)

    def expand(rows, first_row):
        cols = slice(first_row * GROUP_HEADS, (first_row + 4) * GROUP_HEADS)
        return _dot(tm_scr[rows, :][:, cols].astype(BF16), spread)

    dec_scr[...] = expand(pl.ds(0, N_CHUNKS, stride=SSD_CHUNK), R_EF)

    row_i = lax.broadcasted_iota(jnp.int32, (SSD_CHUNK, SSD_CHUNK), 0)
    col_i = lax.broadcasted_iota(jnp.int32, (SSD_CHUNK, SSD_CHUNK), 1)
    lower = col_i <= row_i
    upper = col_i >= row_i
    left = lax.broadcasted_iota(jnp.int32, (SSD_CHUNK, LANES), 1) < SSD_HEAD_DIM
    dsk = dsk_ref[0]
    nw = nw_ref[0]

    def hm_row(row, h, r):
        return hm_scr[row * GROUP_HEADS + h:row * GROUP_HEADS + h + 1, pl.ds(r, SSD_CHUNK)]

    def forward_part(c):
        r = pl.multiple_of(c * SSD_CHUNK, SSD_CHUNK)
        rows = pl.ds(r, SSD_CHUNK)
        xs_c = xs_scr[rows, :]
        bt_c = bt_scr[:, rows]
        cm_c = cm_scr[rows, :]
        g = _dot(cm_c, bt_c)
        xs_b = xs_c.astype(BF16)
        colf = tm_scr[rows, R_COLF * GROUP_HEADS:(R_COLF + 1) * GROUP_HEADS]
        colb = tm_scr[rows, R_COLB * GROUP_HEADS:(R_COLB + 1) * GROUP_HEADS]
        y_pairs = []
        for hp_i in range(GROUP_HEADS // 2):
            ms = []
            for h in (2 * hp_i, 2 * hp_i + 1):
                seg_f = jnp.where(lower, colf[:, h:h + 1] - hm_row(R_ROWF, h, r), NEG)
                seg_b = jnp.where(upper, hm_row(R_ROWB, h, r) - colb[:, h:h + 1], NEG)
                ms.append((g * (jnp.exp2(seg_f) + jnp.exp2(seg_b))).astype(BF16))
            xp = xs_b[:, hp_i * LANES:(hp_i + 1) * LANES]
            zero = jnp.zeros_like(xp)
            rhs = jnp.concatenate([jnp.where(left, xp, zero), jnp.where(left, zero, xp)], axis=0)
            y_pairs.append(_dot(jnp.concatenate(ms, axis=1), rhs))
        y_diag = jnp.concatenate(y_pairs, axis=1)
        ex = expand(rows, R_WF)
        s_f = sf_scr[...]
        y_off = _dot(cm_c, s_f.astype(BF16)) * ex[:, W:2 * W]
        xw = (xs_c * ex[:, 0:W]).astype(BF16)
        sf_scr[...] = s_f * dec_scr[pl.ds(c, 1), 0:W] + _dot(bt_c, xw)
        return y_diag + y_off

    def backward_part(c):
        r = pl.multiple_of(c * SSD_CHUNK, SSD_CHUNK)
        rows = pl.ds(r, SSD_CHUNK)
        ex = expand(rows, R_WB)
        s_b = sb_scr[...]
        y_off = _dot(cm_scr[rows, :], s_b.astype(BF16)) * ex[:, W:2 * W]
        xw = (xs_scr[rows, :] * ex[:, 0:W]).astype(BF16)
        sb_scr[...] = s_b * dec_scr[pl.ds(c, 1), W:2 * W] + _dot(bt_scr[:, rows], xw)
        return y_off

    def finalize(c, y):
        r = pl.multiple_of(c * SSD_CHUNK, SSD_CHUNK)
        rows = pl.ds(r, SSD_CHUNK)
        y = y + y_scr[rows, :] + dsk * xs_scr[rows, :]
        y = y * _silu(z_ref[0, rows, :].astype(F32))
        y = y * lax.rsqrt(jnp.mean(y * y, axis=-1, keepdims=True) + EPS) * nw
        o_ref[0, rows, :] = y.astype(o_ref.dtype)

    sf_scr[...] = jnp.zeros((NS, W), F32)
    sb_scr[...] = jnp.zeros((NS, W), F32)
    half = N_CHUNKS // 2

    def first_half(i, carry):
        cb = N_CHUNKS - 1 - i
        y_scr[pl.ds(pl.multiple_of(i * SSD_CHUNK, SSD_CHUNK), SSD_CHUNK), :] = forward_part(i)
        y_scr[pl.ds(pl.multiple_of(cb * SSD_CHUNK, SSD_CHUNK), SSD_CHUNK), :] = backward_part(cb)
        return carry

    def second_half(i, carry):
        cb = N_CHUNKS - 1 - i
        finalize(i, forward_part(i))
        finalize(cb, backward_part(cb))
        return carry

    lax.fori_loop(0, half, first_half, 0, unroll=True)
    lax.fori_loop(half, N_CHUNKS, second_half, 0, unroll=True)


def _ssd(u3, dt_t, conv_w, conv_b, head_params, dsk_row, norm_w):
    b = u3.shape[0]
    L = SEQ
    W = GROUP_WIDTH
    NS = SSD_STATE
    G = SSD_GROUPS
    xblk = COL_X // W
    bblk = COL_B // NS
    cblk = COL_C // NS
    in_specs = [
        pl.BlockSpec((1, L, W), lambda i, g: (i, 0, g)),
        pl.BlockSpec((1, L, W), lambda i, g: (i, 0, xblk + g)),
        pl.BlockSpec((1, L, NS), lambda i, g: (i, 0, bblk + g)),
        pl.BlockSpec((1, L, NS), lambda i, g: (i, 0, cblk + g)),
        pl.BlockSpec((1, GROUP_HEADS, L), lambda i, g: (i, g, 0)),
        pl.BlockSpec((1, GROUP_HEADS, L), lambda i, g: (i, G + g, 0)),
        pl.BlockSpec((SSD_CONV, W), lambda i, g: (0, g)),
        pl.BlockSpec((SSD_CONV, NS), lambda i, g: (0, SSD_WIDTH // NS + g)),
        pl.BlockSpec((SSD_CONV, NS), lambda i, g: (0, SSD_WIDTH // NS + G + g)),
        pl.BlockSpec((1, W), lambda i, g: (0, g)),
        pl.BlockSpec((1, NS), lambda i, g: (0, SSD_WIDTH // NS + g)),
        pl.BlockSpec((1, NS), lambda i, g: (0, SSD_WIDTH // NS + G + g)),
        pl.BlockSpec((1, GROUP_HEADS, 8), lambda i, g: (g, 0, 0)),
        pl.BlockSpec((1, 1, W), lambda i, g: (g, 0, 0)),
        pl.BlockSpec((1, 1, W), lambda i, g: (g, 0, 0)),
    ]
    return pl.pallas_call(
        _ssd_kernel,
        grid=(b, G),
        in_specs=in_specs,
        out_specs=pl.BlockSpec((1, L, W), lambda i, g: (i, 0, g)),
        out_shape=jax.ShapeDtypeStruct((b, L, SSD_WIDTH), BF16),
        scratch_shapes=[
            pltpu.VMEM((3, (SSD_CONV - 1) * SSD_CHUNK, CONV_WIN), BF16),
            pltpu.VMEM((L, W), F32),
            pltpu.VMEM((NS, L), BF16),
            pltpu.VMEM((L, NS), BF16),
            pltpu.VMEM((L, W), F32),
            pltpu.VMEM((LANES, L), F32),
            pltpu.VMEM((L, LANES), F32),
            pltpu.VMEM((N_CHUNKS, 2 * W), F32),
            pltpu.VMEM((NS, W), F32),
            pltpu.VMEM((NS, W), F32),
        ],
        compiler_params=pltpu.CompilerParams(
            dimension_semantics=("arbitrary", "arbitrary"), vmem_limit_bytes=VMEM_LIMIT),
        name="ssd",
    )(u3, u3, u3, u3, dt_t, dt_t, conv_w, conv_w, conv_w, conv_b, conv_b, conv_b,
      head_params, dsk_row, norm_w)


def _group_mean_sq(v, seg):
    sq = v * v
    hi = sq.astype(BF16)
    lo = (sq - hi.astype(F32)).astype(BF16)
    return (_dot(hi, seg) + _dot(lo, seg)) * (1.0 / 64.0)


def _seg_matrix():
    r = lax.broadcasted_iota(jnp.int32, (LANES, LANES), 0) // 64
    c = lax.broadcasted_iota(jnp.int32, (LANES, LANES), 1) // 64
    return jnp.where(r == c, 1.0, 0.0).astype(BF16)


DIFF_TQ = 256
DIFF_HEADS_PER_STEP = 2


def _diff_kernel(lam_init, q_ref, k_ref, v_ref, g_ref, qc_ref, qs_ref, kc_ref, ks_ref, lam_ref,
                 sub_ref, o_ref, q_scr, k_scr, vt_scr, sa_scr, sb_scr):
    L = SEQ
    seg = _seg_matrix()
    comp0 = lax.broadcasted_iota(jnp.int32, (DIFF_TQ, LANES), 1) < 64
    p_row = lax.broadcasted_iota(jnp.int32, (LANES, LANES), 0)
    p_col = lax.broadcasted_iota(jnp.int32, (LANES, LANES), 1)
    swap = jnp.where((p_row ^ 32) == p_col, 1.0, 0.0).astype(BF16)

    def norm_rope(ref, r, lanes, cw_ref, sw_ref):
        vb = ref[0, pl.ds(r, DIFF_TQ), lanes]
        v = vb.astype(F32)
        rinv = lax.rsqrt(_group_mean_sq(v, seg) + EPS)
        return rinv * (v * cw_ref[pl.ds(r, DIFF_TQ), :] + _dot(vb, swap) * sw_ref[pl.ds(r, DIFF_TQ), :])

    n_blk = L // DIFF_TQ
    n_items = DIFF_HEADS_PER_STEP * n_blk

    def item(j):
        hd = j // n_blk
        r = pl.multiple_of((j % n_blk) * DIFF_TQ, DIFF_TQ)
        return hd, r, pl.ds(pl.multiple_of(hd * LANES, LANES), LANES)

    for hd in range(DIFF_HEADS_PER_STEP):
        vt_scr[hd, LANES:LANES + ONES_ROWS, :] = jnp.ones((ONES_ROWS, L), BF16)

    def prep(j, carry):
        hd, r, lanes = item(j)
        q_scr[hd, pl.ds(r, DIFF_TQ), :] = norm_rope(q_ref, r, lanes, qc_ref, qs_ref).astype(BF16)
        kn = norm_rope(k_ref, r, lanes, kc_ref, ks_ref)
        k_scr[hd, 0, pl.ds(r, DIFF_TQ), :] = jnp.where(comp0, kn, 0.0).astype(BF16)
        k_scr[hd, 1, pl.ds(r, DIFF_TQ), :] = jnp.where(comp0, 0.0, kn).astype(BF16)
        vt_scr[hd, 0:LANES, pl.ds(r, DIFF_TQ)] = v_ref[0, pl.ds(r, DIFF_TQ), lanes].astype(F32).T.astype(BF16)
        return carry

    lax.fori_loop(0, n_items, prep, 0, unroll=4)

    lam = lam_ref[...]
    lam_full = (jnp.exp(jnp.sum(lam[0:1] * lam[1:2], axis=-1, keepdims=True))
                - jnp.exp(jnp.sum(lam[2:3] * lam[3:4], axis=-1, keepdims=True)) + lam_init)

    def scores(j, s_ref):
        hd, r, _ = item(j)
        qb = q_scr[hd, pl.ds(r, DIFF_TQ), :]
        for c in range(2):
            s_ref[c] = _dot_nt(k_scr[hd, c], qb)

    def finish(j, s_ref):
        hd, r, lanes = item(j)
        parts = []
        for c in range(2):
            s = s_ref[c]
            e = jnp.exp2(s - jnp.max(s, axis=0, keepdims=True)).astype(BF16)
            acc = _dot(vt_scr[hd], e)
            parts.append(acc[0:LANES, :] * (1.0 / acc[LANES:LANES + 1, :]))
        ot = parts[0] - lam_full * parts[1]
        o = ot.T
        o = o * lax.rsqrt(jnp.mean(o * o, axis=-1, keepdims=True) + EPS) * sub_ref[...]
        o = o * (1.0 - lam_init) * _silu(g_ref[0, pl.ds(r, DIFF_TQ), lanes].astype(F32))
        o_ref[0, pl.ds(r, DIFF_TQ), lanes] = o.astype(o_ref.dtype)

    scores(0, sa_scr)

    def pair(i, carry):
        scores(2 * i + 1, sb_scr)
        finish(2 * i, sa_scr)
        scores(2 * i + 2, sa_scr)
        finish(2 * i + 1, sb_scr)
        return carry

    lax.fori_loop(0, n_items // 2 - 1, pair, 0, unroll=2)
    scores(n_items - 1, sb_scr)
    finish(n_items - 2, sa_scr)
    finish(n_items - 1, sb_scr)


def _diff(u3, rope_tabs, lam, subln_w, lam_init):
    b = u3.shape[0]
    L = SEQ
    step_w = DIFF_HEADS_PER_STEP * LANES
    base = COL_DIFF // step_w
    nb = DIFF_WIDTH // step_w

    def spec(part):
        return pl.BlockSpec((1, L, step_w), lambda i, h: (i, 0, base + part * nb + h))

    return pl.pallas_call(
        functools.partial(_diff_kernel, lam_init),
        grid=(b, DIFF_HEADS // DIFF_HEADS_PER_STEP),
        in_specs=[
            spec(0), spec(1), spec(2), spec(3),
            pl.BlockSpec((L, LANES), lambda i, h: (0, 0)),
            pl.BlockSpec((L, LANES), lambda i, h: (0, 0)),
            pl.BlockSpec((L, LANES), lambda i, h: (0, 0)),
            pl.BlockSpec((L, LANES), lambda i, h: (0, 0)),
            pl.BlockSpec((4, DIFF_HEAD_DIM), lambda i, h: (0, 0)),
            pl.BlockSpec((1, LANES), lambda i, h: (0, 0)),
        ],
        out_specs=pl.BlockSpec((1, L, step_w), lambda i, h: (i, 0, h)),
        out_shape=jax.ShapeDtypeStruct((b, L, DIFF_WIDTH), BF16),
        scratch_shapes=[
            pltpu.VMEM((DIFF_HEADS_PER_STEP, L, LANES), BF16),
            pltpu.VMEM((DIFF_HEADS_PER_STEP, 2, L, LANES), BF16),
            pltpu.VMEM((DIFF_HEADS_PER_STEP, LANES + ONES_ROWS, L), BF16),
            pltpu.VMEM((2, L, DIFF_TQ), F32),
            pltpu.VMEM((2, L, DIFF_TQ), F32),
        ],
        compiler_params=pltpu.CompilerParams(
            dimension_semantics=("arbitrary", "arbitrary"), vmem_limit_bytes=VMEM_LIMIT),
        name="diff_attn",
    )(u3, u3, u3, u3, *rope_tabs, lam, subln_w)


def _na_block_geometry(qb):
    r0 = qb * NA_QROWS
    kb = min(max(r0 - NA_KH // 2, 0), ROWS - NA_BAND)
    return r0, kb


def _na_bias_type(qb):
    return 0 if qb == 0 else (2 if qb == NA_NBLK - 1 else 1)


def _na_kernel(q_ref, k_ref, v_ref, g_ref, qkw_ref, bias_ref, o_ref, q_scr, k_scr, vt_scr, s_scr):
    L = SEQ
    seg = _seg_matrix()
    head0_blk = lax.broadcasted_iota(jnp.int32, (NA_QBLK, LANES), 1) < 64
    q_scale = (NA_HEAD_DIM ** -0.5) * LOG2E
    for h in range(2):
        vt_scr[h, NA_HEAD_DIM:NA_HEAD_DIM + ONES_ROWS, :] = jnp.ones((ONES_ROWS, L), BF16)

    def norm(ref, r, w_row):
        v = ref[0, pl.ds(r, NA_QBLK), :].astype(F32)
        return v * lax.rsqrt(_group_mean_sq(v, seg) + EPS) * w_row

    def prep(i, carry):
        r = pl.multiple_of(i * NA_QBLK, NA_QBLK)
        q_scr[pl.ds(r, NA_QBLK), :] = (norm(q_ref, r, qkw_ref[0:1, :]) * q_scale).astype(BF16)
        kn = norm(k_ref, r, qkw_ref[1:2, :])
        k_scr[0, pl.ds(r, NA_QBLK), :] = jnp.where(head0_blk, kn, 0.0).astype(BF16)
        k_scr[1, pl.ds(r, NA_QBLK), :] = jnp.where(head0_blk, 0.0, kn).astype(BF16)
        vt = v_ref[0, pl.ds(r, NA_QBLK), :].astype(F32).T.astype(BF16)
        for h in range(2):
            vt_scr[h, 0:NA_HEAD_DIM, pl.ds(r, NA_QBLK)] = vt[h * NA_HEAD_DIM:(h + 1) * NA_HEAD_DIM, :]
        return carry

    lax.fori_loop(0, L // NA_QBLK, prep, 0, unroll=4)

    def slices(qb):
        r0, kb = _na_block_geometry(qb)
        return (slice(r0 * GRID_W, r0 * GRID_W + NA_QBLK), slice(kb * GRID_W, kb * GRID_W + NA_KBLK))

    def scores(qb):
        qs, ks = slices(qb)
        q_blk = q_scr[qs, :]
        for h in range(2):
            s_scr[qb % 2, h] = _dot_nt(k_scr[h, ks, :], q_blk)

    def finish(qb):
        qs, ks = slices(qb)
        t = _na_bias_type(qb)
        outs = []
        for h in range(2):
            s = s_scr[qb % 2, h] + bias_ref[h, t]
            e = jnp.exp2(s - jnp.max(s, axis=0, keepdims=True)).astype(BF16)
            acc = _dot(vt_scr[h, :, ks], e)
            outs.append(acc[0:NA_HEAD_DIM, :] * (1.0 / acc[NA_HEAD_DIM:NA_HEAD_DIM + 1, :]))
        o = jnp.concatenate(outs, axis=0).T
        o = o * _silu(g_ref[0, qs, :].astype(F32))
        o_ref[0, qs, :] = o.astype(o_ref.dtype)

    scores(0)
    for qb in range(NA_NBLK):
        if qb + 1 < NA_NBLK:
            scores(qb + 1)
        finish(qb)


def _na(u3, qk_w, bias, layer):
    b = u3.shape[0]
    L = SEQ
    base = COL_NA // LANES
    nb = NA_WIDTH // LANES

    def spec(part):
        return pl.BlockSpec((1, L, LANES), lambda hp, i: (i, 0, base + part * nb + hp))

    return pl.pallas_call(
        _na_kernel,
        grid=(NA_HEADS // 2, b),
        in_specs=[
            spec(0), spec(1), spec(2), spec(3),
            pl.BlockSpec((2, LANES), lambda hp, i: (0, 0)),
            pl.BlockSpec((None, 2, 3, NA_KBLK, NA_QBLK), lambda hp, i: (layer, hp, 0, 0, 0)),
        ],
        out_specs=pl.BlockSpec((1, L, LANES), lambda hp, i: (i, 0, hp)),
        out_shape=jax.ShapeDtypeStruct((b, L, NA_WIDTH), BF16),
        scratch_shapes=[
            pltpu.VMEM((L, LANES), BF16),
            pltpu.VMEM((2, L, LANES), BF16),
            pltpu.VMEM((2, NA_HEAD_DIM + ONES_ROWS, L), BF16),
            pltpu.VMEM((2, 2, NA_KBLK, NA_QBLK), F32),
        ],
        compiler_params=pltpu.CompilerParams(
            dimension_semantics=("arbitrary", "arbitrary"), vmem_limit_bytes=VMEM_LIMIT),
        name="na_attn",
    )(u3, u3, u3, u3, qk_w, bias)


def _na_bias(rpb):
    c = np.arange(GRID_W)
    col_start = np.clip(c - NA_KW // 2, 0, GRID_W - NA_KW)
    col_ok = (c[None, :] >= col_start[:, None]) & (c[None, :] < col_start[:, None] + NA_KW)
    lead = rpb.shape[:-2]
    pad = GRID_W - NA_KW
    rev = jnp.pad(rpb.astype(F32)[..., ::-1] * LOG2E, [(0, 0)] * (rpb.ndim - 1) + [(pad, pad)])
    tiles = jnp.stack([rev[..., GRID_W - 1 - k:2 * GRID_W - 1 - k] for k in range(GRID_W)], axis=-2)
    tiles = jnp.where(col_ok.T, tiles, NEG)
    tiles = jnp.concatenate([tiles, tiles], axis=-1)
    n_dr = 2 * NA_KH - 1
    tiles = tiles.reshape((-1, n_dr, GRID_W, LANES))
    out = pl.pallas_call(
        _na_bias_kernel,
        grid=(tiles.shape[0],),
        in_specs=[pl.BlockSpec((1, n_dr, GRID_W, LANES), lambda n: (n, 0, 0, 0))],
        out_specs=pl.BlockSpec((1, 3, NA_KBLK, NA_QBLK), lambda n: (n, 0, 0, 0)),
        out_shape=jax.ShapeDtypeStruct((tiles.shape[0], 3, NA_KBLK, NA_QBLK), F32),
        compiler_params=pltpu.CompilerParams(
            dimension_semantics=("arbitrary",), vmem_limit_bytes=VMEM_LIMIT),
        name="na_bias",
    )(tiles)
    return out.reshape(lead + (3, NA_KBLK, NA_QBLK))


def _na_bias_kernel(t_ref, o_ref):
    left = lax.broadcasted_iota(jnp.int32, (GRID_W, LANES), 1) < GRID_W
    masked = jnp.full((GRID_W, LANES), NEG, F32)

    def tile(qb, j, i):
        r0, kb = _na_block_geometry(qb)
        r, rk = r0 + i, kb + j
        rs = min(max(r - NA_KH // 2, 0), ROWS - NA_KH)
        return t_ref[0, rk - r + NA_KH - 1] if rs <= rk < rs + NA_KH else masked

    for t, qb in enumerate((0, 1, NA_NBLK - 1)):
        for j in range(NA_BAND):
            for p in range(NA_QROWS // 2):
                o_ref[0, t, j * GRID_W:(j + 1) * GRID_W, p * LANES:(p + 1) * LANES] = jnp.where(
                    left, tile(qb, j, 2 * p), tile(qb, j, 2 * p + 1))


OUT_TM = 512


def _outproj_kernel(x_ref, ys_ref, yd_ref, yn_ref, w_ref, o_ref):
    acc = _dot(ys_ref[...], w_ref[0:SSD_WIDTH, :])
    acc = acc + _dot(yd_ref[...], w_ref[SSD_WIDTH:SSD_WIDTH + DIFF_WIDTH, :])
    acc = acc + _dot(yn_ref[...], w_ref[SSD_WIDTH + DIFF_WIDTH:MIX_WIDTH, :])
    o_ref[...] = x_ref[...] + acc


def _outproj(x2, y_ssd, y_diff, y_na, w_out):
    m = x2.shape[0]
    return pl.pallas_call(
        _outproj_kernel,
        grid=(m // OUT_TM,),
        in_specs=[
            pl.BlockSpec((OUT_TM, D_MODEL), lambda i: (i, 0)),
            pl.BlockSpec((OUT_TM, SSD_WIDTH), lambda i: (i, 0)),
            pl.BlockSpec((OUT_TM, DIFF_WIDTH), lambda i: (i, 0)),
            pl.BlockSpec((OUT_TM, NA_WIDTH), lambda i: (i, 0)),
            pl.BlockSpec((MIX_WIDTH, D_MODEL), lambda i: (0, 0)),
        ],
        out_specs=pl.BlockSpec((OUT_TM, D_MODEL), lambda i: (i, 0)),
        out_shape=jax.ShapeDtypeStruct((m, D_MODEL), F32),
        compiler_params=pltpu.CompilerParams(
            dimension_semantics=("arbitrary",), vmem_limit_bytes=VMEM_LIMIT),
        name="outproj",
    )(x2, y_ssd, y_diff, y_na, w_out)


def _rope_tables():
    inv_freq = ROPE_THETA ** (-jnp.arange(0, DIFF_HEAD_DIM, 2, dtype=F32) / DIFF_HEAD_DIM)
    ang = jnp.arange(SEQ, dtype=F32)[:, None] * inv_freq[None, :]
    cos, sin = jnp.cos(ang), jnp.sin(ang)
    cos_t = jnp.concatenate([cos, cos, cos, cos], axis=1)
    sin_t = jnp.concatenate([-sin, sin, -sin, sin], axis=1)
    return cos_t, sin_t


def kernel(x, norm_w, w_in, conv_w, conv_b, a_log, dt_bias, d_skip, ssd_norm_w, diff_qk_norm,
           diff_lambda, diff_subln, na_qk_norm, na_rpb, w_out):
    b, L, d = x.shape
    assert (L, d) == (SEQ, D_MODEL)
    depth = w_in.shape[0]
    cos_t, sin_t = _rope_tables()
    na_bias = _na_bias(na_rpb)
    dt_lo = SSD_WIDTH + SSD_XBC
    dt_hi = dt_lo + 2 * SSD_HEADS
    w_main = jnp.concatenate([w_in[:, :, :dt_lo], w_in[:, :, dt_hi:]], axis=2).astype(BF16)
    w_dt = jnp.pad(w_in[:, :, dt_lo:dt_hi], ((0, 0), (0, 0), (0, DT_PAD - 2 * SSD_HEADS))).astype(BF16)
    x2 = x.reshape(b * L, d)
    for i in range(depth):
        lam_init = 0.8 - 0.6 * math.exp(-0.3 * i)
        u, dt = _inproj(x2, norm_w[i][None, :], w_main, w_dt, i)
        u3 = u.reshape(b, L, U_MAIN)
        dt_t = jnp.swapaxes(dt.reshape(b, L, DT_PAD)[:, :, :2 * SSD_HEADS], 1, 2)

        hp = jnp.stack([a_log[i][0], a_log[i][1], dt_bias[i][0], dt_bias[i][1]], axis=-1)
        hp = jnp.pad(hp, ((0, 0), (0, 4))).reshape(SSD_GROUPS, GROUP_HEADS, 8)
        dsk_row = jnp.repeat(d_skip[i][0] + d_skip[i][1], SSD_HEAD_DIM).reshape(SSD_GROUPS, 1, GROUP_WIDTH)
        y_ssd = _ssd(u3, dt_t, conv_w[i], conv_b[i][None, :], hp, dsk_row,
                     ssd_norm_w[i].reshape(SSD_GROUPS, 1, GROUP_WIDTH))

        half = DIFF_HEAD_DIM // 2
        qk_w = jnp.tile(diff_qk_norm[i], (1, 2))
        qk_w_sw = jnp.tile(jnp.concatenate([diff_qk_norm[i][:, half:], diff_qk_norm[i][:, :half]], axis=1), (1, 2))
        q_scale = (DIFF_HEAD_DIM ** -0.5) * LOG2E
        rope_tabs = (cos_t * (qk_w[0:1] * q_scale), sin_t * (qk_w_sw[0:1] * q_scale),
                     cos_t * qk_w[1:2], sin_t * qk_w_sw[1:2])
        y_diff = _diff(u3, rope_tabs, diff_lambda[i], diff_subln[i][None, :], lam_init)

        na_w = jnp.concatenate([na_qk_norm[i], na_qk_norm[i]], axis=1)
        y_na = _na(u3, na_w, na_bias, i)

        x2 = _outproj(x2, y_ssd.reshape(b * L, SSD_WIDTH), y_diff.reshape(b * L, DIFF_WIDTH),
                      y_na.reshape(b * L, NA_WIDTH), w_out[i].astype(BF16))
    return x2.reshape(b, L, d)
```

```python
import functools
import math

import numpy as np
import jax
import jax.numpy as jnp
from jax import lax
from jax.experimental import pallas as pl
from jax.experimental.pallas import tpu as pltpu

F32 = jnp.float32
BF16 = jnp.bfloat16

D_MODEL = 1024
SEQ = 2048
GRID_W = 64
ROWS = SEQ // GRID_W
SSD_WIDTH = 1024
SSD_HEAD_DIM = 64
SSD_HEADS = 16
SSD_GROUPS = 2
SSD_STATE = 128
SSD_CONV = 5
SSD_CHUNK = 128
SSD_XBC = SSD_WIDTH + 2 * SSD_GROUPS * SSD_STATE
GROUP_HEADS = SSD_HEADS // SSD_GROUPS
GROUP_WIDTH = SSD_WIDTH // SSD_GROUPS
N_CHUNKS = SEQ // SSD_CHUNK
DIFF_WIDTH = 512
DIFF_HEAD_DIM = 64
DIFF_HEADS = 4
NA_WIDTH = 512
NA_HEAD_DIM = 64
NA_HEADS = 8
NA_KH = 8
NA_KW = 16
MIX_WIDTH = SSD_WIDTH + DIFF_WIDTH + NA_WIDTH
ROPE_THETA = 10000.0
EPS = 1e-6
LANES = 128
CONV_HALO = 16
CONV_WIN = SSD_CHUNK + 2 * CONV_HALO

U_MAIN = SSD_WIDTH + SSD_XBC + 4 * DIFF_WIDTH + 4 * NA_WIDTH
COL_Z = 0
COL_X = SSD_WIDTH
COL_B = COL_X + SSD_WIDTH
COL_C = COL_B + SSD_GROUPS * SSD_STATE
COL_DIFF = COL_C + SSD_GROUPS * SSD_STATE
COL_NA = COL_DIFF + 4 * DIFF_WIDTH
DT_PAD = LANES

NA_QROWS = 4
NA_BAND = 12
NA_QBLK = NA_QROWS * GRID_W
NA_KBLK = NA_BAND * GRID_W
NA_NBLK = ROWS // NA_QROWS
NEG = -1e30
LOG2E = math.log2(math.e)
ONES_ROWS = 16

VMEM_LIMIT = 56 * 1024 * 1024


def _silu(v):
    h = 0.5 * v
    return h * jnp.tanh(h) + h


def _dot(a, b):
    return jnp.dot(a, b, preferred_element_type=F32)


def _dot_nt(a, b):
    return lax.dot_general(a, b, (((1,), (1,)), ((), ())), preferred_element_type=F32)


def _dot_tn(a, b):
    return lax.dot_general(a, b, (((0,), (0,)), ((), ())), preferred_element_type=F32)


IN_TM = 1024
IN_TN = 3328


def _inproj_kernel(x_ref, nw_ref, w_ref, wdt_ref, u_ref, dt_ref, h_scr):
    @pl.when(pl.program_id(1) == 0)
    def _():
        x = x_ref[...]
        ms = jnp.mean(x * x, axis=-1, keepdims=True)
        h = (x * lax.rsqrt(ms + EPS) * nw_ref[...]).astype(BF16)
        h_scr[...] = h
        dt_ref[...] = _dot(h, wdt_ref[...])

    u_ref[...] = _dot(h_scr[...], w_ref[...]).astype(u_ref.dtype)


def _inproj(x2, norm_w, w_main, w_dt, layer):
    m = x2.shape[0]
    return pl.pallas_call(
        _inproj_kernel,
        grid=(m // IN_TM, U_MAIN // IN_TN),
        in_specs=[
            pl.BlockSpec((IN_TM, D_MODEL), lambda i, j: (i, 0)),
            pl.BlockSpec((1, D_MODEL), lambda i, j: (0, 0)),
            pl.BlockSpec((None, D_MODEL, IN_TN), lambda i, j: (layer, 0, j)),
            pl.BlockSpec((None, D_MODEL, DT_PAD), lambda i, j: (layer, 0, 0)),
        ],
        out_specs=[
            pl.BlockSpec((IN_TM, IN_TN), lambda i, j: (i, j)),
            pl.BlockSpec((IN_TM, DT_PAD), lambda i, j: (i, 0)),
        ],
        out_shape=[
            jax.ShapeDtypeStruct((m, U_MAIN), BF16),
            jax.ShapeDtypeStruct((m, DT_PAD), F32),
        ],
        scratch_shapes=[pltpu.VMEM((IN_TM, D_MODEL), BF16)],
        compiler_params=pltpu.CompilerParams(
            dimension_semantics=("arbitrary", "arbitrary"), vmem_limit_bytes=VMEM_LIMIT),
        name="inproj",
    )(x2, norm_w, w_main, w_dt)


R_WF, R_DF = 0, 1
R_WB, R_DB = 4, 5
R_COLF, R_COLB = 8, 9
R_EF, R_EB = 10, 11
R_ROWF, R_ROWB = 14, 15


def _softplus(v):
    return jnp.maximum(v, 0.0) + jnp.log1p(jnp.exp(-jnp.abs(v)))


def _chunk_scan(a, lane, reverse):
    n = a.shape[-1]
    out = a
    k = 1
    while k < SSD_CHUNK:
        if reverse:
            out = out + jnp.where(lane < SSD_CHUNK - k, pltpu.roll(out, n - k, axis=1), 0.0)
        else:
            out = out + jnp.where(lane >= k, pltpu.roll(out, k, axis=1), 0.0)
        k *= 2
    return out


def _ssd_kernel(z_ref, x_ref, b_ref, c_ref, dtf_ref, dtb_ref, cwx_ref, cwb_ref, cwc_ref,
                cbx_ref, cbb_ref, cbc_ref, hp_ref, dsk_ref, nw_ref, o_ref,
                shift_scr, xs_scr, bt_scr, cm_scr, y_scr, hm_scr, tm_scr, dec_scr, sf_scr, sb_scr):
    L = SEQ
    W = GROUP_WIDTH
    NS = SSD_STATE

    side = (SSD_CONV - 1) * SSD_CHUNK
    tap_t = lax.broadcasted_iota(jnp.int32, (side, CONV_WIN), 0)
    tap_j = lax.broadcasted_iota(jnp.int32, (side, CONV_WIN), 1)
    tap_k = tap_t // SSD_CHUNK
    tap_src = tap_t % SSD_CHUNK + jnp.where(tap_k >= SSD_CONV // 2, tap_k + 1, tap_k) - SSD_CONV // 2
    for variant in range(3):
        shift_scr[variant] = jnp.where(tap_j == tap_src + variant * CONV_HALO, 1.0, 0.0).astype(BF16)
    side_taps = [k for k in range(SSD_CONV) if k != SSD_CONV // 2]

    def conv(c, carry):
        r = pl.multiple_of(c * SSD_CHUNK, SSD_CHUNK)
        w0 = pl.multiple_of(jnp.clip(r - CONV_HALO, 0, L - CONV_WIN), CONV_HALO)
        variant = jnp.where(c == 0, 0, jnp.where(c == N_CHUNKS - 1, 2, 1))
        shift = shift_scr[variant]
        rows = pl.ds(r, SSD_CHUNK)

        def taps(win, centre, cw, cb):
            sh = _dot(shift, win)
            acc = cb + centre.astype(F32) * cw[SSD_CONV // 2:SSD_CONV // 2 + 1, :]
            for slot, k in enumerate(side_taps):
                acc = acc + sh[slot * SSD_CHUNK:(slot + 1) * SSD_CHUNK, :] * cw[k:k + 1, :]
            return _silu(acc)

        xs_scr[rows, :] = taps(x_ref[0, pl.ds(w0, CONV_WIN), :], x_ref[0, rows, :], cwx_ref[...], cbx_ref[...])
        win_bc = jnp.concatenate([b_ref[0, pl.ds(w0, CONV_WIN), :], c_ref[0, pl.ds(w0, CONV_WIN), :]], axis=1)
        mid_bc = jnp.concatenate([b_ref[0, rows, :], c_ref[0, rows, :]], axis=1)
        act_bc = taps(win_bc, mid_bc, jnp.concatenate([cwb_ref[...], cwc_ref[...]], axis=1),
                      jnp.concatenate([cbb_ref[...], cbc_ref[...]], axis=1))
        bt_scr[:, rows] = act_bc[:, 0:NS].T.astype(BF16)
        cm_scr[rows, :] = act_bc[:, NS:2 * NS].astype(BF16)
        return carry

    lax.fori_loop(0, N_CHUNKS, conv, 0, unroll=4)

    hp = hp_ref[0]
    a_f = -jnp.exp(hp[:, 0:1])
    a_b = -jnp.exp(hp[:, 1:2])
    dt_f = _softplus(dtf_ref[0] + hp[:, 2:3])
    dt_b = _softplus(dtb_ref[0] + hp[:, 3:4])
    da_f = dt_f * a_f
    da_b = dt_b * a_b
    lane = lax.broadcasted_iota(jnp.int32, (GROUP_HEADS, L), 1) % SSD_CHUNK
    cs_f = _chunk_scan(da_f, lane, False)
    rs_f = _chunk_scan(da_f, lane, True) - da_f
    cs_b = _chunk_scan(da_b, lane, False)
    ecs_b = cs_b - da_b
    rs_b = _chunk_scan(da_b, lane, True)

    def put(row, v):
        hm_scr[row * GROUP_HEADS:(row + 1) * GROUP_HEADS, :] = v

    def put_split(row_hi, v):
        hi = v.astype(BF16).astype(F32)
        put(row_hi, hi)
        put(row_hi + 2, (v - hi).astype(BF16).astype(F32))

    put_split(R_WF, jnp.exp(rs_f) * dt_f)
    put_split(R_DF, jnp.exp(cs_f))
    put_split(R_WB, jnp.exp(ecs_b) * dt_b)
    put_split(R_DB, jnp.exp(rs_b))
    put_split(R_EF, jnp.exp(cs_f + rs_f))
    put_split(R_EB, jnp.exp(ecs_b + rs_b))
    put(R_COLF, cs_f * LOG2E)
    put(R_COLB, ecs_b * LOG2E)
    put(R_ROWF, cs_f * LOG2E - jnp.log2(dt_f))
    put(R_ROWB, ecs_b * LOG2E + jnp.log2(dt_b))

    def to_time_major(c, carry):
        r = pl.multiple_of(c * SSD_CHUNK, SSD_CHUNK)
        tm_scr[pl.ds(r, SSD_CHUNK), :] = hm_scr[:, pl.ds(r, SSD_CHUNK)].T
        return carry

    lax.fori_loop(0, N_CHUNKS, to_time_major, 0, unroll=4)

    e_row = lax.broadcasted_iota(jnp.int32, (4 * GROUP_HEADS, 2 * W), 0) % (2 * GROUP_HEADS)
    e_col = lax.broadcasted_iota(jnp.int32, (4 * GROUP_HEADS, 2 * W), 1) // SSD_HEAD_DIM
    spread = jnp.where(e_row == e_col, 1.0, 0.0).astype(BF16)

    def expand(rows, first_row):
        cols = slice(first_row * GROUP_HEADS, (first_row + 4) * GROUP_HEADS)
        return _dot(tm_scr[rows, :][:, cols].astype(BF16), spread)

    dec_scr[...] = expand(pl.ds(0, N_CHUNKS, stride=SSD_CHUNK), R_EF)

    row_i = lax.broadcasted_iota(jnp.int32, (SSD_CHUNK, SSD_CHUNK), 0)
    col_i = lax.broadcasted_iota(jnp.int32, (SSD_CHUNK, SSD_CHUNK), 1)
    lower = col_i <= row_i
    upper = col_i >= row_i
    left = lax.broadcasted_iota(jnp.int32, (SSD_CHUNK, LANES), 1) < SSD_HEAD_DIM
    dsk = dsk_ref[0]
    nw = nw_ref[0]

    def hm_row(row, h, r):
        return hm_scr[row * GROUP_HEADS + h:row * GROUP_HEADS + h + 1, pl.ds(r, SSD_CHUNK)]

    def forward_part(c):
        r = pl.multiple_of(c * SSD_CHUNK, SSD_CHUNK)
        rows = pl.ds(r, SSD_CHUNK)
        xs_c = xs_scr[rows, :]
        bt_c = bt_scr[:, rows]
        cm_c = cm_scr[rows, :]
        g = _dot(cm_c, bt_c)
        xs_b = xs_c.astype(BF16)
        colf = tm_scr[rows, R_COLF * GROUP_HEADS:(R_COLF + 1) * GROUP_HEADS]
        colb = tm_scr[rows, R_COLB * GROUP_HEADS:(R_COLB + 1) * GROUP_HEADS]
        y_pairs = []
        for hp_i in range(GROUP_HEADS // 2):
            ms = []
            for h in (2 * hp_i, 2 * hp_i + 1):
                seg_f = jnp.where(lower, colf[:, h:h + 1] - hm_row(R_ROWF, h, r), NEG)
                seg_b = jnp.where(upper, hm_row(R_ROWB, h, r) - colb[:, h:h + 1], NEG)
                ms.append((g * (jnp.exp2(seg_f) + jnp.exp2(seg_b))).astype(BF16))
            xp = xs_b[:, hp_i * LANES:(hp_i + 1) * LANES]
            zero = jnp.zeros_like(xp)
            rhs = jnp.concatenate([jnp.where(left, xp, zero), jnp.where(left, zero, xp)], axis=0)
            y_pairs.append(_dot(jnp.concatenate(ms, axis=1), rhs))
        y_diag = jnp.concatenate(y_pairs, axis=1)
        ex = expand(rows, R_WF)
        s_f = sf_scr[...]
        y_off = _dot(cm_c, s_f.astype(BF16)) * ex[:, W:2 * W]
        xw = (xs_c * ex[:, 0:W]).astype(BF16)
        sf_scr[...] = s_f * dec_scr[pl.ds(c, 1), 0:W] + _dot(bt_c, xw)
        return y_diag + y_off

    def backward_part(c):
        r = pl.multiple_of(c * SSD_CHUNK, SSD_CHUNK)
        rows = pl.ds(r, SSD_CHUNK)
        ex = expand(rows, R_WB)
        s_b = sb_scr[...]
        y_off = _dot(cm_scr[rows, :], s_b.astype(BF16)) * ex[:, W:2 * W]
        xw = (xs_scr[rows, :] * ex[:, 0:W]).astype(BF16)
        sb_scr[...] = s_b * dec_scr[pl.ds(c, 1), W:2 * W] + _dot(bt_scr[:, rows], xw)
        return y_off

    def finalize(c, y):
        r = pl.multiple_of(c * SSD_CHUNK, SSD_CHUNK)
        rows = pl.ds(r, SSD_CHUNK)
        y = y + y_scr[rows, :] + dsk * xs_scr[rows, :]
        y = y * _silu(z_ref[0, rows, :].astype(F32))
        y = y * lax.rsqrt(jnp.mean(y * y, axis=-1, keepdims=True) + EPS) * nw
        o_ref[0, rows, :] = y.astype(o_ref.dtype)

    sf_scr[...] = jnp.zeros((NS, W), F32)
    sb_scr[...] = jnp.zeros((NS, W), F32)
    half = N_CHUNKS // 2

    def first_half(i, carry):
        cb = N_CHUNKS - 1 - i
        y_scr[pl.ds(pl.multiple_of(i * SSD_CHUNK, SSD_CHUNK), SSD_CHUNK), :] = forward_part(i)
        y_scr[pl.ds(pl.multiple_of(cb * SSD_CHUNK, SSD_CHUNK), SSD_CHUNK), :] = backward_part(cb)
        return carry

    def second_half(i, carry):
        cb = N_CHUNKS - 1 - i
        finalize(i, forward_part(i))
        finalize(cb, backward_part(cb))
        return carry

    lax.fori_loop(0, half, first_half, 0, unroll=True)
    lax.fori_loop(half, N_CHUNKS, second_half, 0, unroll=True)


def _ssd(u3, dt_t, conv_w, conv_b, head_params, dsk_row, norm_w):
    b = u3.shape[0]
    L = SEQ
    W = GROUP_WIDTH
    NS = SSD_STATE
    G = SSD_GROUPS
    xblk = COL_X // W
    bblk = COL_B // NS
    cblk = COL_C // NS
    in_specs = [
        pl.BlockSpec((1, L, W), lambda i, g: (i, 0, g)),
        pl.BlockSpec((1, L, W), lambda i, g: (i, 0, xblk + g)),
        pl.BlockSpec((1, L, NS), lambda i, g: (i, 0, bblk + g)),
        pl.BlockSpec((1, L, NS), lambda i, g: (i, 0, cblk + g)),
        pl.BlockSpec((1, GROUP_HEADS, L), lambda i, g: (i, g, 0)),
        pl.BlockSpec((1, GROUP_HEADS, L), lambda i, g: (i, G + g, 0)),
        pl.BlockSpec((SSD_CONV, W), lambda i, g: (0, g)),
        pl.BlockSpec((SSD_CONV, NS), lambda i, g: (0, SSD_WIDTH // NS + g)),
        pl.BlockSpec((SSD_CONV, NS), lambda i, g: (0, SSD_WIDTH // NS + G + g)),
        pl.BlockSpec((1, W), lambda i, g: (0, g)),
        pl.BlockSpec((1, NS), lambda i, g: (0, SSD_WIDTH // NS + g)),
        pl.BlockSpec((1, NS), lambda i, g: (0, SSD_WIDTH // NS + G + g)),
        pl.BlockSpec((1, GROUP_HEADS, 8), lambda i, g: (g, 0, 0)),
        pl.BlockSpec((1, 1, W), lambda i, g: (g, 0, 0)),
        pl.BlockSpec((1, 1, W), lambda i, g: (g, 0, 0)),
    ]
    return pl.pallas_call(
        _ssd_kernel,
        grid=(b, G),
        in_specs=in_specs,
        out_specs=pl.BlockSpec((1, L, W), lambda i, g: (i, 0, g)),
        out_shape=jax.ShapeDtypeStruct((b, L, SSD_WIDTH), BF16),
        scratch_shapes=[
            pltpu.VMEM((3, (SSD_CONV - 1) * SSD_CHUNK, CONV_WIN), BF16),
            pltpu.VMEM((L, W), F32),
            pltpu.VMEM((NS, L), BF16),
            pltpu.VMEM((L, NS), BF16),
            pltpu.VMEM((L, W), F32),
            pltpu.VMEM((LANES, L), F32),
            pltpu.VMEM((L, LANES), F32),
            pltpu.VMEM((N_CHUNKS, 2 * W), F32),
            pltpu.VMEM((NS, W), F32),
            pltpu.VMEM((NS, W), F32),
        ],
        compiler_params=pltpu.CompilerParams(
            dimension_semantics=("arbitrary", "arbitrary"), vmem_limit_bytes=VMEM_LIMIT),
        name="ssd",
    )(u3, u3, u3, u3, dt_t, dt_t, conv_w, conv_w, conv_w, conv_b, conv_b, conv_b,
      head_params, dsk_row, norm_w)


def _group_mean_sq(v, seg):
    sq = v * v
    hi = sq.astype(BF16)
    lo = (sq - hi.astype(F32)).astype(BF16)
    return (_dot(hi, seg) + _dot(lo, seg)) * (1.0 / 64.0)


def _seg_matrix():
    r = lax.broadcasted_iota(jnp.int32, (LANES, LANES), 0) // 64
    c = lax.broadcasted_iota(jnp.int32, (LANES, LANES), 1) // 64
    return jnp.where(r == c, 1.0, 0.0).astype(BF16)


DIFF_TQ = 256
DIFF_HEADS_PER_STEP = 2


def _diff_kernel(lam_init, q_ref, k_ref, v_ref, g_ref, qc_ref, qs_ref, kc_ref, ks_ref, lam_ref,
                 sub_ref, o_ref, q_scr, k_scr, vt_scr, sa_scr, sb_scr):
    L = SEQ
    seg = _seg_matrix()
    comp0 = lax.broadcasted_iota(jnp.int32, (DIFF_TQ, LANES), 1) < 64
    p_row = lax.broadcasted_iota(jnp.int32, (LANES, LANES), 0)
    p_col = lax.broadcasted_iota(jnp.int32, (LANES, LANES), 1)
    swap = jnp.where((p_row ^ 32) == p_col, 1.0, 0.0).astype(BF16)

    def norm_rope(ref, r, lanes, cw_ref, sw_ref):
        vb = ref[0, pl.ds(r, DIFF_TQ), lanes]
        v = vb.astype(F32)
        rinv = lax.rsqrt(_group_mean_sq(v, seg) + EPS)
        return rinv * (v * cw_ref[pl.ds(r, DIFF_TQ), :] + _dot(vb, swap) * sw_ref[pl.ds(r, DIFF_TQ), :])

    n_blk = L // DIFF_TQ
    n_items = DIFF_HEADS_PER_STEP * n_blk

    def item(j):
        hd = j // n_blk
        r = pl.multiple_of((j % n_blk) * DIFF_TQ, DIFF_TQ)
        return hd, r, pl.ds(pl.multiple_of(hd * LANES, LANES), LANES)

    for hd in range(DIFF_HEADS_PER_STEP):
        vt_scr[hd, LANES:LANES + ONES_ROWS, :] = jnp.ones((ONES_ROWS, L), BF16)

    def prep(j, carry):
        hd, r, lanes = item(j)
        q_scr[hd, :, pl.ds(r, DIFF_TQ)] = norm_rope(q_ref, r, lanes, qc_ref, qs_ref).T.astype(BF16)
        kn = norm_rope(k_ref, r, lanes, kc_ref, ks_ref)
        k_scr[hd, 0, pl.ds(r, DIFF_TQ), :] = jnp.where(comp0, kn, 0.0).astype(BF16)
        k_scr[hd, 1, pl.ds(r, DIFF_TQ), :] = jnp.where(comp0, 0.0, kn).astype(BF16)
        vt_scr[hd, 0:LANES, pl.ds(r, DIFF_TQ)] = v_ref[0, pl.ds(r, DIFF_TQ), lanes].astype(F32).T.astype(BF16)
        return carry

    lax.fori_loop(0, n_items, prep, 0, unroll=4)

    lam = lam_ref[...]
    lam_full = (jnp.exp(jnp.sum(lam[0:1] * lam[1:2], axis=-1, keepdims=True))
                - jnp.exp(jnp.sum(lam[2:3] * lam[3:4], axis=-1, keepdims=True)) + lam_init)

    def scores(j, s_ref):
        hd, r, _ = item(j)
        qt = q_scr[hd, :, pl.ds(r, DIFF_TQ)]
        for c in range(2):
            s_ref[c] = _dot(k_scr[hd, c], qt)

    def finish(j, s_ref):
        hd, r, lanes = item(j)
        parts = []
        for c in range(2):
            s = s_ref[c]
            e = jnp.exp2(s - jnp.max(s, axis=0, keepdims=True)).astype(BF16)
            acc = _dot(vt_scr[hd], e)
            parts.append(acc[0:LANES, :] * (1.0 / acc[LANES:LANES + 1, :]))
        ot = parts[0] - lam_full * parts[1]
        o = ot.T
        o = o * lax.rsqrt(jnp.mean(o * o, axis=-1, keepdims=True) + EPS) * sub_ref[...]
        o = o * (1.0 - lam_init) * _silu(g_ref[0, pl.ds(r, DIFF_TQ), lanes].astype(F32))
        o_ref[0, pl.ds(r, DIFF_TQ), lanes] = o.astype(o_ref.dtype)

    scores(0, sa_scr)

    def pair(i, carry):
        scores(2 * i + 1, sb_scr)
        finish(2 * i, sa_scr)
        scores(2 * i + 2, sa_scr)
        finish(2 * i + 1, sb_scr)
        return carry

    lax.fori_loop(0, n_items // 2 - 1, pair, 0, unroll=2)
    scores(n_items - 1, sb_scr)
    finish(n_items - 2, sa_scr)
    finish(n_items - 1, sb_scr)


def _diff(u3, rope_tabs, lam, subln_w, lam_init):
    b = u3.shape[0]
    L = SEQ
    step_w = DIFF_HEADS_PER_STEP * LANES
    base = COL_DIFF // step_w
    nb = DIFF_WIDTH // step_w

    def spec(part):
        return pl.BlockSpec((1, L, step_w), lambda i, h: (i, 0, base + part * nb + h))

    return pl.pallas_call(
        functools.partial(_diff_kernel, lam_init),
        grid=(b, DIFF_HEADS // DIFF_HEADS_PER_STEP),
        in_specs=[
            spec(0), spec(1), spec(2), spec(3),
            pl.BlockSpec((L, LANES), lambda i, h: (0, 0)),
            pl.BlockSpec((L, LANES), lambda i, h: (0, 0)),
            pl.BlockSpec((L, LANES), lambda i, h: (0, 0)),
            pl.BlockSpec((L, LANES), lambda i, h: (0, 0)),
            pl.BlockSpec((4, DIFF_HEAD_DIM), lambda i, h: (0, 0)),
            pl.BlockSpec((1, LANES), lambda i, h: (0, 0)),
        ],
        out_specs=pl.BlockSpec((1, L, step_w), lambda i, h: (i, 0, h)),
        out_shape=jax.ShapeDtypeStruct((b, L, DIFF_WIDTH), BF16),
        scratch_shapes=[
            pltpu.VMEM((DIFF_HEADS_PER_STEP, LANES, L), BF16),
            pltpu.VMEM((DIFF_HEADS_PER_STEP, 2, L, LANES), BF16),
            pltpu.VMEM((DIFF_HEADS_PER_STEP, LANES + ONES_ROWS, L), BF16),
            pltpu.VMEM((2, L, DIFF_TQ), F32),
            pltpu.VMEM((2, L, DIFF_TQ), F32),
        ],
        compiler_params=pltpu.CompilerParams(
            dimension_semantics=("arbitrary", "arbitrary"), vmem_limit_bytes=VMEM_LIMIT),
        name="diff_attn",
    )(u3, u3, u3, u3, *rope_tabs, lam, subln_w)


def _na_block_geometry(qb):
    r0 = qb * NA_QROWS
    kb = min(max(r0 - NA_KH // 2, 0), ROWS - NA_BAND)
    return r0, kb


def _na_bias_type(qb):
    return 0 if qb == 0 else (2 if qb == NA_NBLK - 1 else 1)


def _na_kernel(q_ref, k_ref, v_ref, g_ref, qkw_ref, bias_ref, o_ref, q_scr, k_scr, vt_scr, s_scr):
    L = SEQ
    seg = _seg_matrix()
    head0_blk = lax.broadcasted_iota(jnp.int32, (NA_QBLK, LANES), 1) < 64
    q_scale = (NA_HEAD_DIM ** -0.5) * LOG2E
    for h in range(2):
        vt_scr[h, NA_HEAD_DIM:NA_HEAD_DIM + ONES_ROWS, :] = jnp.ones((ONES_ROWS, L), BF16)

    def norm(ref, r, w_row):
        v = ref[0, pl.ds(r, NA_QBLK), :].astype(F32)
        return v * lax.rsqrt(_group_mean_sq(v, seg) + EPS) * w_row

    def prep(i, carry):
        r = pl.multiple_of(i * NA_QBLK, NA_QBLK)
        q_scr[:, pl.ds(r, NA_QBLK)] = (norm(q_ref, r, qkw_ref[0:1, :]) * q_scale).T.astype(BF16)
        kn = norm(k_ref, r, qkw_ref[1:2, :])
        k_scr[0, pl.ds(r, NA_QBLK), :] = jnp.where(head0_blk, kn, 0.0).astype(BF16)
        k_scr[1, pl.ds(r, NA_QBLK), :] = jnp.where(head0_blk, 0.0, kn).astype(BF16)
        vt = v_ref[0, pl.ds(r, NA_QBLK), :].astype(F32).T.astype(BF16)
        for h in range(2):
            vt_scr[h, 0:NA_HEAD_DIM, pl.ds(r, NA_QBLK)] = vt[h * NA_HEAD_DIM:(h + 1) * NA_HEAD_DIM, :]
        return carry

    lax.fori_loop(0, L // NA_QBLK, prep, 0, unroll=4)

    def slices(qb):
        r0, kb = _na_block_geometry(qb)
        return (slice(r0 * GRID_W, r0 * GRID_W + NA_QBLK), slice(kb * GRID_W, kb * GRID_W + NA_KBLK))

    def scores(qb):
        qs, ks = slices(qb)
        qt = q_scr[:, qs]
        for h in range(2):
            s_scr[qb % 2, h] = _dot(k_scr[h, ks, :], qt)

    def finish(qb):
        qs, ks = slices(qb)
        t = _na_bias_type(qb)
        outs = []
        for h in range(2):
            s = s_scr[qb % 2, h] + bias_ref[h, t]
            e = jnp.exp2(s - jnp.max(s, axis=0, keepdims=True)).astype(BF16)
            acc = _dot(vt_scr[h, :, ks], e)
            outs.append(acc[0:NA_HEAD_DIM, :] * (1.0 / acc[NA_HEAD_DIM:NA_HEAD_DIM + 1, :]))
        o = jnp.concatenate(outs, axis=0).T
        o = o * _silu(g_ref[0, qs, :].astype(F32))
        o_ref[0, qs, :] = o.astype(o_ref.dtype)

    scores(0)
    for qb in range(NA_NBLK):
        if qb + 1 < NA_NBLK:
            scores(qb + 1)
        finish(qb)


def _na(u3, qk_w, bias, layer):
    b = u3.shape[0]
    L = SEQ
    base = COL_NA // LANES
    nb = NA_WIDTH // LANES

    def spec(part):
        return pl.BlockSpec((1, L, LANES), lambda hp, i: (i, 0, base + part * nb + hp))

    return pl.pallas_call(
        _na_kernel,
        grid=(NA_HEADS // 2, b),
        in_specs=[
            spec(0), spec(1), spec(2), spec(3),
            pl.BlockSpec((2, LANES), lambda hp, i: (0, 0)),
            pl.BlockSpec((None, 2, 3, NA_KBLK, NA_QBLK), lambda hp, i: (layer, hp, 0, 0, 0)),
        ],
        out_specs=pl.BlockSpec((1, L, LANES), lambda hp, i: (i, 0, hp)),
        out_shape=jax.ShapeDtypeStruct((b, L, NA_WIDTH), BF16),
        scratch_shapes=[
            pltpu.VMEM((LANES, L), BF16),
            pltpu.VMEM((2, L, LANES), BF16),
            pltpu.VMEM((2, NA_HEAD_DIM + ONES_ROWS, L), BF16),
            pltpu.VMEM((2, 2, NA_KBLK, NA_QBLK), F32),
        ],
        compiler_params=pltpu.CompilerParams(
            dimension_semantics=("arbitrary", "arbitrary"), vmem_limit_bytes=VMEM_LIMIT),
        name="na_attn",
    )(u3, u3, u3, u3, qk_w, bias)


def _na_bias(rpb):
    c = np.arange(GRID_W)
    col_start = np.clip(c - NA_KW // 2, 0, GRID_W - NA_KW)
    col_ok = (c[None, :] >= col_start[:, None]) & (c[None, :] < col_start[:, None] + NA_KW)
    lead = rpb.shape[:-2]
    pad = GRID_W - NA_KW
    rev = jnp.pad(rpb.astype(F32)[..., ::-1] * LOG2E, [(0, 0)] * (rpb.ndim - 1) + [(pad, pad)])
    tiles = jnp.stack([rev[..., GRID_W - 1 - k:2 * GRID_W - 1 - k] for k in range(GRID_W)], axis=-2)
    tiles = jnp.where(col_ok.T, tiles, NEG)
    tiles = jnp.concatenate([tiles, tiles], axis=-1)
    n_dr = 2 * NA_KH - 1
    tiles = tiles.reshape((-1, n_dr, GRID_W, LANES))
    out = pl.pallas_call(
        _na_bias_kernel,
        grid=(tiles.shape[0],),
        in_specs=[pl.BlockSpec((1, n_dr, GRID_W, LANES), lambda n: (n, 0, 0, 0))],
        out_specs=pl.BlockSpec((1, 3, NA_KBLK, NA_QBLK), lambda n: (n, 0, 0, 0)),
        out_shape=jax.ShapeDtypeStruct((tiles.shape[0], 3, NA_KBLK, NA_QBLK), F32),
        compiler_params=pltpu.CompilerParams(
            dimension_semantics=("arbitrary",), vmem_limit_bytes=VMEM_LIMIT),
        name="na_bias",
    )(tiles)
    return out.reshape(lead + (3, NA_KBLK, NA_QBLK))


def _na_bias_kernel(t_ref, o_ref):
    left = lax.broadcasted_iota(jnp.int32, (GRID_W, LANES), 1) < GRID_W
    masked = jnp.full((GRID_W, LANES), NEG, F32)

    def tile(qb, j, i):
        r0, kb = _na_block_geometry(qb)
        r, rk = r0 + i, kb + j
        rs = min(max(r - NA_KH // 2, 0), ROWS - NA_KH)
        return t_ref[0, rk - r + NA_KH - 1] if rs <= rk < rs + NA_KH else masked

    for t, qb in enumerate((0, 1, NA_NBLK - 1)):
        for j in range(NA_BAND):
            for p in range(NA_QROWS // 2):
                o_ref[0, t, j * GRID_W:(j + 1) * GRID_W, p * LANES:(p + 1) * LANES] = jnp.where(
                    left, tile(qb, j, 2 * p), tile(qb, j, 2 * p + 1))


OUT_TM = 1024


def _outproj_kernel(x_ref, ys_ref, yd_ref, yn_ref, w_ref, o_ref):
    acc = _dot(ys_ref[...], w_ref[0:SSD_WIDTH, :])
    acc = acc + _dot(yd_ref[...], w_ref[SSD_WIDTH:SSD_WIDTH + DIFF_WIDTH, :])
    acc = acc + _dot(yn_ref[...], w_ref[SSD_WIDTH + DIFF_WIDTH:MIX_WIDTH, :])
    o_ref[...] = x_ref[...] + acc


def _outproj(x2, y_ssd, y_diff, y_na, w_out):
    m = x2.shape[0]
    return pl.pallas_call(
        _outproj_kernel,
        grid=(m // OUT_TM,),
        in_specs=[
            pl.BlockSpec((OUT_TM, D_MODEL), lambda i: (i, 0)),
            pl.BlockSpec((OUT_TM, SSD_WIDTH), lambda i: (i, 0)),
            pl.BlockSpec((OUT_TM, DIFF_WIDTH), lambda i: (i, 0)),
            pl.BlockSpec((OUT_TM, NA_WIDTH), lambda i: (i, 0)),
            pl.BlockSpec((MIX_WIDTH, D_MODEL), lambda i: (0, 0)),
        ],
        out_specs=pl.BlockSpec((OUT_TM, D_MODEL), lambda i: (i, 0)),
        out_shape=jax.ShapeDtypeStruct((m, D_MODEL), F32),
        compiler_params=pltpu.CompilerParams(
            dimension_semantics=("arbitrary",), vmem_limit_bytes=VMEM_LIMIT),
        name="outproj",
    )(x2, y_ssd, y_diff, y_na, w_out)


def _rope_tables():
    inv_freq = ROPE_THETA ** (-jnp.arange(0, DIFF_HEAD_DIM, 2, dtype=F32) / DIFF_HEAD_DIM)
    ang = jnp.arange(SEQ, dtype=F32)[:, None] * inv_freq[None, :]
    cos, sin = jnp.cos(ang), jnp.sin(ang)
    cos_t = jnp.concatenate([cos, cos, cos, cos], axis=1)
    sin_t = jnp.concatenate([-sin, sin, -sin, sin], axis=1)
    return cos_t, sin_t


def kernel(x, norm_w, w_in, conv_w, conv_b, a_log, dt_bias, d_skip, ssd_norm_w, diff_qk_norm,
           diff_lambda, diff_subln, na_qk_norm, na_rpb, w_out):
    b, L, d = x.shape
    assert (L, d) == (SEQ, D_MODEL)
    depth = w_in.shape[0]
    cos_t, sin_t = _rope_tables()
    na_bias = _na_bias(na_rpb)
    dt_lo = SSD_WIDTH + SSD_XBC
    dt_hi = dt_lo + 2 * SSD_HEADS
    keep = lax.broadcasted_iota(jnp.int32, (1, 1, U_MAIN), 2) < dt_lo
    w_main = jnp.where(keep, w_in[:, :, :U_MAIN], w_in[:, :, dt_hi - dt_lo:]).astype(BF16)
    w_dt = jnp.pad(w_in[:, :, dt_lo:dt_hi], ((0, 0), (0, 0), (0, DT_PAD - 2 * SSD_HEADS))).astype(BF16)
    x2 = x.reshape(b * L, d)
    for i in range(depth):
        lam_init = 0.8 - 0.6 * math.exp(-0.3 * i)
        u, dt = _inproj(x2, norm_w[i][None, :], w_main, w_dt, i)
        u3 = u.reshape(b, L, U_MAIN)
        dt_t = jnp.swapaxes(dt.reshape(b, L, DT_PAD)[:, :, :2 * SSD_HEADS], 1, 2)

        hp = jnp.stack([a_log[i][0], a_log[i][1], dt_bias[i][0], dt_bias[i][1]], axis=-1)
        hp = jnp.pad(hp, ((0, 0), (0, 4))).reshape(SSD_GROUPS, GROUP_HEADS, 8)
        dsk_row = jnp.repeat(d_skip[i][0] + d_skip[i][1], SSD_HEAD_DIM).reshape(SSD_GROUPS, 1, GROUP_WIDTH)
        y_ssd = _ssd(u3, dt_t, conv_w[i], conv_b[i][None, :], hp, dsk_row,
                     ssd_norm_w[i].reshape(SSD_GROUPS, 1, GROUP_WIDTH))

        half = DIFF_HEAD_DIM // 2
        qk_w = jnp.tile(diff_qk_norm[i], (1, 2))
        qk_w_sw = jnp.tile(jnp.concatenate([diff_qk_norm[i][:, half:], diff_qk_norm[i][:, :half]], axis=1), (1, 2))
        q_scale = (DIFF_HEAD_DIM ** -0.5) * LOG2E
        rope_tabs = (cos_t * (qk_w[0:1] * q_scale), sin_t * (qk_w_sw[0:1] * q_scale),
                     cos_t * qk_w[1:2], sin_t * qk_w_sw[1:2])
        y_diff = _diff(u3, rope_tabs, diff_lambda[i], diff_subln[i][None, :], lam_init)

        na_w = jnp.concatenate([na_qk_norm[i], na_qk_norm[i]], axis=1)
        y_na = _na(u3, na_w, na_bias, i)

        x2 = _outproj(x2, y_ssd.reshape(b * L, SSD_WIDTH), y_diff.reshape(b * L, DIFF_WIDTH),
                      y_na.reshape(b * L, NA_WIDTH), w_out[i].astype(BF16))
    return x2.reshape(b, L, d)
```

```python
import functools
import math

import numpy as np
import jax
import jax.numpy as jnp
from jax import lax
from jax.experimental import pallas as pl
from jax.experimental.pallas import tpu as pltpu

F32 = jnp.float32
BF16 = jnp.bfloat16

D_MODEL = 1024
SEQ = 2048
GRID_W = 64
ROWS = SEQ // GRID_W
SSD_WIDTH = 1024
SSD_HEAD_DIM = 64
SSD_HEADS = 16
SSD_GROUPS = 2
SSD_STATE = 128
SSD_CONV = 5
SSD_CHUNK = 128
SSD_XBC = SSD_WIDTH + 2 * SSD_GROUPS * SSD_STATE
GROUP_HEADS = SSD_HEADS // SSD_GROUPS
GROUP_WIDTH = SSD_WIDTH // SSD_GROUPS
N_CHUNKS = SEQ // SSD_CHUNK
DIFF_WIDTH = 512
DIFF_HEAD_DIM = 64
DIFF_HEADS = 4
NA_WIDTH = 512
NA_HEAD_DIM = 64
NA_HEADS = 8
NA_KH = 8
NA_KW = 16
MIX_WIDTH = SSD_WIDTH + DIFF_WIDTH + NA_WIDTH
ROPE_THETA = 10000.0
EPS = 1e-6
LANES = 128
CONV_HALO = 16
CONV_WIN = SSD_CHUNK + 2 * CONV_HALO

U_MAIN = SSD_WIDTH + SSD_XBC + 4 * DIFF_WIDTH + 4 * NA_WIDTH
COL_Z = 0
COL_X = SSD_WIDTH
COL_B = COL_X + SSD_WIDTH
COL_C = COL_B + SSD_GROUPS * SSD_STATE
COL_DIFF = COL_C + SSD_GROUPS * SSD_STATE
COL_NA = COL_DIFF + 4 * DIFF_WIDTH
DT_PAD = LANES

NA_QROWS = 4
NA_BAND = 12
NA_QBLK = NA_QROWS * GRID_W
NA_KBLK = NA_BAND * GRID_W
NA_NBLK = ROWS // NA_QROWS
NEG = -1e30
LOG2E = math.log2(math.e)
ONES_ROWS = 16

VMEM_LIMIT = 56 * 1024 * 1024


def _silu(v):
    h = 0.5 * v
    return h * jnp.tanh(h) + h


def _dot(a, b):
    return jnp.dot(a, b, preferred_element_type=F32)


def _dot_nt(a, b):
    return lax.dot_general(a, b, (((1,), (1,)), ((), ())), preferred_element_type=F32)


def _dot_tn(a, b):
    return lax.dot_general(a, b, (((0,), (0,)), ((), ())), preferred_element_type=F32)


IN_TM = 1024
IN_TN = 3328


def _inproj_kernel(x_ref, nw_ref, w_ref, wdt_ref, u_ref, dt_ref, h_scr):
    @pl.when(pl.program_id(1) == 0)
    def _():
        x = x_ref[...]
        ms = jnp.mean(x * x, axis=-1, keepdims=True)
        h = (x * lax.rsqrt(ms + EPS) * nw_ref[...]).astype(BF16)
        h_scr[...] = h
        dt_ref[...] = _dot_nt(h, wdt_ref[...])

    u_ref[...] = _dot_nt(h_scr[...], w_ref[...]).astype(u_ref.dtype)


def _inproj(x2, norm_w, w_main, w_dt, layer):
    m = x2.shape[0]
    return pl.pallas_call(
        _inproj_kernel,
        grid=(m // IN_TM, U_MAIN // IN_TN),
        in_specs=[
            pl.BlockSpec((IN_TM, D_MODEL), lambda i, j: (i, 0)),
            pl.BlockSpec((1, D_MODEL), lambda i, j: (0, 0)),
            pl.BlockSpec((None, IN_TN, D_MODEL), lambda i, j: (layer, j, 0)),
            pl.BlockSpec((None, DT_PAD, D_MODEL), lambda i, j: (layer, 0, 0)),
        ],
        out_specs=[
            pl.BlockSpec((IN_TM, IN_TN), lambda i, j: (i, j)),
            pl.BlockSpec((IN_TM, DT_PAD), lambda i, j: (i, 0)),
        ],
        out_shape=[
            jax.ShapeDtypeStruct((m, U_MAIN), BF16),
            jax.ShapeDtypeStruct((m, DT_PAD), F32),
        ],
        scratch_shapes=[pltpu.VMEM((IN_TM, D_MODEL), BF16)],
        compiler_params=pltpu.CompilerParams(
            dimension_semantics=("arbitrary", "arbitrary"), vmem_limit_bytes=VMEM_LIMIT),
        name="inproj",
    )(x2, norm_w, w_main, w_dt)


R_WF, R_DF = 0, 1
R_WB, R_DB = 4, 5
R_COLF, R_COLB = 8, 9
R_EF, R_EB = 10, 11
R_ROWF, R_ROWB = 14, 15


def _softplus(v):
    return jnp.maximum(v, 0.0) + jnp.log1p(jnp.exp(-jnp.abs(v)))


def _chunk_scan(a, lane, reverse):
    n = a.shape[-1]
    out = a
    k = 1
    while k < SSD_CHUNK:
        if reverse:
            out = out + jnp.where(lane < SSD_CHUNK - k, pltpu.roll(out, n - k, axis=1), 0.0)
        else:
            out = out + jnp.where(lane >= k, pltpu.roll(out, k, axis=1), 0.0)
        k *= 2
    return out


def _ssd_kernel(z_ref, x_ref, b_ref, c_ref, dtf_ref, dtb_ref, cwx_ref, cwb_ref, cwc_ref,
                cbx_ref, cbb_ref, cbc_ref, hp_ref, dsk_ref, nw_ref, o_ref,
                shift_scr, xs_scr, bt_scr, cm_scr, y_scr, hm_scr, tm_scr, dec_scr, sf_scr, sb_scr):
    L = SEQ
    W = GROUP_WIDTH
    NS = SSD_STATE

    side = (SSD_CONV - 1) * SSD_CHUNK
    tap_t = lax.broadcasted_iota(jnp.int32, (side, CONV_WIN), 0)
    tap_j = lax.broadcasted_iota(jnp.int32, (side, CONV_WIN), 1)
    tap_k = tap_t // SSD_CHUNK
    tap_src = tap_t % SSD_CHUNK + jnp.where(tap_k >= SSD_CONV // 2, tap_k + 1, tap_k) - SSD_CONV // 2
    for variant in range(3):
        shift_scr[variant] = jnp.where(tap_j == tap_src + variant * CONV_HALO, 1.0, 0.0).astype(BF16)
    side_taps = [k for k in range(SSD_CONV) if k != SSD_CONV // 2]

    def conv(c, carry):
        r = pl.multiple_of(c * SSD_CHUNK, SSD_CHUNK)
        w0 = pl.multiple_of(jnp.clip(r - CONV_HALO, 0, L - CONV_WIN), CONV_HALO)
        variant = jnp.where(c == 0, 0, jnp.where(c == N_CHUNKS - 1, 2, 1))
        shift = shift_scr[variant]
        rows = pl.ds(r, SSD_CHUNK)

        def taps(win, centre, cw, cb):
            sh = _dot(shift, win)
            acc = cb + centre.astype(F32) * cw[SSD_CONV // 2:SSD_CONV // 2 + 1, :]
            for slot, k in enumerate(side_taps):
                acc = acc + sh[slot * SSD_CHUNK:(slot + 1) * SSD_CHUNK, :] * cw[k:k + 1, :]
            return _silu(acc)

        xs_scr[rows, :] = taps(x_ref[0, pl.ds(w0, CONV_WIN), :], x_ref[0, rows, :], cwx_ref[...], cbx_ref[...])
        win_bc = jnp.concatenate([b_ref[0, pl.ds(w0, CONV_WIN), :], c_ref[0, pl.ds(w0, CONV_WIN), :]], axis=1)
        mid_bc = jnp.concatenate([b_ref[0, rows, :], c_ref[0, rows, :]], axis=1)
        act_bc = taps(win_bc, mid_bc, jnp.concatenate([cwb_ref[...], cwc_ref[...]], axis=1),
                      jnp.concatenate([cbb_ref[...], cbc_ref[...]], axis=1))
        bt_scr[:, rows] = act_bc[:, 0:NS].T.astype(BF16)
        cm_scr[rows, :] = act_bc[:, NS:2 * NS].astype(BF16)
        return carry

    lax.fori_loop(0, N_CHUNKS, conv, 0, unroll=4)

    hp = hp_ref[0]
    a_f = -jnp.exp(hp[:, 0:1])
    a_b = -jnp.exp(hp[:, 1:2])
    dt_f = _softplus(dtf_ref[0] + hp[:, 2:3])
    dt_b = _softplus(dtb_ref[0] + hp[:, 3:4])
    da_f = dt_f * a_f
    da_b = dt_b * a_b
    lane = lax.broadcasted_iota(jnp.int32, (GROUP_HEADS, L), 1) % SSD_CHUNK
    cs_f = _chunk_scan(da_f, lane, False)
    rs_f = _chunk_scan(da_f, lane, True) - da_f
    cs_b = _chunk_scan(da_b, lane, False)
    ecs_b = cs_b - da_b
    rs_b = _chunk_scan(da_b, lane, True)

    def put(row, v):
        hm_scr[row * GROUP_HEADS:(row + 1) * GROUP_HEADS, :] = v

    def put_split(row_hi, v):
        hi = v.astype(BF16).astype(F32)
        put(row_hi, hi)
        put(row_hi + 2, (v - hi).astype(BF16).astype(F32))

    put_split(R_WF, jnp.exp(rs_f) * dt_f)
    put_split(R_DF, jnp.exp(cs_f))
    put_split(R_WB, jnp.exp(ecs_b) * dt_b)
    put_split(R_DB, jnp.exp(rs_b))
    put_split(R_EF, jnp.exp(cs_f + rs_f))
    put_split(R_EB, jnp.exp(ecs_b + rs_b))
    put(R_COLF, cs_f * LOG2E)
    put(R_COLB, ecs_b * LOG2E)
    put(R_ROWF, cs_f * LOG2E - jnp.log2(dt_f))
    put(R_ROWB, ecs_b * LOG2E + jnp.log2(dt_b))

    def to_time_major(c, carry):
        r = pl.multiple_of(c * SSD_CHUNK, SSD_CHUNK)
        tm_scr[pl.ds(r, SSD_CHUNK), :] = hm_scr[:, pl.ds(r, SSD_CHUNK)].T
        return carry

    lax.fori_loop(0, N_CHUNKS, to_time_major, 0, unroll=4)

    e_row = lax.broadcasted_iota(jnp.int32, (4 * GROUP_HEADS, 2 * W), 0) % (2 * GROUP_HEADS)
    e_col = lax.broadcasted_iota(jnp.int32, (4 * GROUP_HEADS, 2 * W), 1) // SSD_HEAD_DIM
    spread = jnp.where(e_row == e_col, 1.0, 0.0).astype(BF16)

    def expand(rows, first_row):
        cols = slice(first_row * GROUP_HEADS, (first_row + 4) * GROUP_HEADS)
        return _dot(tm_scr[rows, :][:, cols].astype(BF16), spread)

    dec_scr[...] = expand(pl.ds(0, N_CHUNKS, stride=SSD_CHUNK), R_EF)

    row_i = lax.broadcasted_iota(jnp.int32, (SSD_CHUNK, SSD_CHUNK), 0)
    col_i = lax.broadcasted_iota(jnp.int32, (SSD_CHUNK, SSD_CHUNK), 1)
    lower = col_i <= row_i
    upper = col_i >= row_i
    left = lax.broadcasted_iota(jnp.int32, (SSD_CHUNK, LANES), 1) < SSD_HEAD_DIM
    dsk = dsk_ref[0]
    nw = nw_ref[0]

    def hm_row(row, h, r):
        return hm_scr[row * GROUP_HEADS + h:row * GROUP_HEADS + h + 1, pl.ds(r, SSD_CHUNK)]

    def forward_part(c):
        r = pl.multiple_of(c * SSD_CHUNK, SSD_CHUNK)
        rows = pl.ds(r, SSD_CHUNK)
        xs_c = xs_scr[rows, :]
        bt_c = bt_scr[:, rows]
        cm_c = cm_scr[rows, :]
        g = _dot(cm_c, bt_c)
        xs_b = xs_c.astype(BF16)
        colf = tm_scr[rows, R_COLF * GROUP_HEADS:(R_COLF + 1) * GROUP_HEADS]
        colb = tm_scr[rows, R_COLB * GROUP_HEADS:(R_COLB + 1) * GROUP_HEADS]
        y_pairs = []
        for hp_i in range(GROUP_HEADS // 2):
            ms = []
            for h in (2 * hp_i, 2 * hp_i + 1):
                seg_f = jnp.where(lower, colf[:, h:h + 1] - hm_row(R_ROWF, h, r), NEG)
                seg_b = jnp.where(upper, hm_row(R_ROWB, h, r) - colb[:, h:h + 1], NEG)
                ms.append((g * (jnp.exp2(seg_f) + jnp.exp2(seg_b))).astype(BF16))
            xp = xs_b[:, hp_i * LANES:(hp_i + 1) * LANES]
            zero = jnp.zeros_like(xp)
            rhs = jnp.concatenate([jnp.where(left, xp, zero), jnp.where(left, zero, xp)], axis=0)
            y_pairs.append(_dot(jnp.concatenate(ms, axis=1), rhs))
        y_diag = jnp.concatenate(y_pairs, axis=1)
        ex = expand(rows, R_WF)
        s_f = sf_scr[...]
        y_off = _dot(cm_c, s_f.astype(BF16)) * ex[:, W:2 * W]
        xw = (xs_c * ex[:, 0:W]).astype(BF16)
        sf_scr[...] = s_f * dec_scr[pl.ds(c, 1), 0:W] + _dot(bt_c, xw)
        return y_diag + y_off

    def backward_part(c):
        r = pl.multiple_of(c * SSD_CHUNK, SSD_CHUNK)
        rows = pl.ds(r, SSD_CHUNK)
        ex = expand(rows, R_WB)
        s_b = sb_scr[...]
        y_off = _dot(cm_scr[rows, :], s_b.astype(BF16)) * ex[:, W:2 * W]
        xw = (xs_scr[rows, :] * ex[:, 0:W]).astype(BF16)
        sb_scr[...] = s_b * dec_scr[pl.ds(c, 1), W:2 * W] + _dot(bt_scr[:, rows], xw)
        return y_off

    def finalize(c, y):
        r = pl.multiple_of(c * SSD_CHUNK, SSD_CHUNK)
        rows = pl.ds(r, SSD_CHUNK)
        y = y + y_scr[rows, :] + dsk * xs_scr[rows, :]
        y = y * _silu(z_ref[0, rows, :].astype(F32))
        y = y * lax.rsqrt(jnp.mean(y * y, axis=-1, keepdims=True) + EPS) * nw
        o_ref[0, rows, :] = y.astype(o_ref.dtype)

    sf_scr[...] = jnp.zeros((NS, W), F32)
    sb_scr[...] = jnp.zeros((NS, W), F32)
    half = N_CHUNKS // 2

    def first_half(i, carry):
        cb = N_CHUNKS - 1 - i
        y_scr[pl.ds(pl.multiple_of(i * SSD_CHUNK, SSD_CHUNK), SSD_CHUNK), :] = forward_part(i)
        y_scr[pl.ds(pl.multiple_of(cb * SSD_CHUNK, SSD_CHUNK), SSD_CHUNK), :] = backward_part(cb)
        return carry

    def second_half(i, carry):
        cb = N_CHUNKS - 1 - i
        finalize(i, forward_part(i))
        finalize(cb, backward_part(cb))
        return carry

    lax.fori_loop(0, half, first_half, 0, unroll=True)
    lax.fori_loop(half, N_CHUNKS, second_half, 0, unroll=True)


def _ssd(u3, dt_t, conv_w, conv_b, head_params, dsk_row, norm_w):
    b = u3.shape[0]
    L = SEQ
    W = GROUP_WIDTH
    NS = SSD_STATE
    G = SSD_GROUPS
    xblk = COL_X // W
    bblk = COL_B // NS
    cblk = COL_C // NS
    in_specs = [
        pl.BlockSpec((1, L, W), lambda i, g: (i, 0, g)),
        pl.BlockSpec((1, L, W), lambda i, g: (i, 0, xblk + g)),
        pl.BlockSpec((1, L, NS), lambda i, g: (i, 0, bblk + g)),
        pl.BlockSpec((1, L, NS), lambda i, g: (i, 0, cblk + g)),
        pl.BlockSpec((1, GROUP_HEADS, L), lambda i, g: (i, g, 0)),
        pl.BlockSpec((1, GROUP_HEADS, L), lambda i, g: (i, G + g, 0)),
        pl.BlockSpec((SSD_CONV, W), lambda i, g: (0, g)),
        pl.BlockSpec((SSD_CONV, NS), lambda i, g: (0, SSD_WIDTH // NS + g)),
        pl.BlockSpec((SSD_CONV, NS), lambda i, g: (0, SSD_WIDTH // NS + G + g)),
        pl.BlockSpec((1, W), lambda i, g: (0, g)),
        pl.BlockSpec((1, NS), lambda i, g: (0, SSD_WIDTH // NS + g)),
        pl.BlockSpec((1, NS), lambda i, g: (0, SSD_WIDTH // NS + G + g)),
        pl.BlockSpec((1, GROUP_HEADS, 8), lambda i, g: (g, 0, 0)),
        pl.BlockSpec((1, 1, W), lambda i, g: (g, 0, 0)),
        pl.BlockSpec((1, 1, W), lambda i, g: (g, 0, 0)),
    ]
    return pl.pallas_call(
        _ssd_kernel,
        grid=(b, G),
        in_specs=in_specs,
        out_specs=pl.BlockSpec((1, L, W), lambda i, g: (i, 0, g)),
        out_shape=jax.ShapeDtypeStruct((b, L, SSD_WIDTH), BF16),
        scratch_shapes=[
            pltpu.VMEM((3, (SSD_CONV - 1) * SSD_CHUNK, CONV_WIN), BF16),
            pltpu.VMEM((L, W), F32),
            pltpu.VMEM((NS, L), BF16),
            pltpu.VMEM((L, NS), BF16),
            pltpu.VMEM((L, W), F32),
            pltpu.VMEM((LANES, L), F32),
            pltpu.VMEM((L, LANES), F32),
            pltpu.VMEM((N_CHUNKS, 2 * W), F32),
            pltpu.VMEM((NS, W), F32),
            pltpu.VMEM((NS, W), F32),
        ],
        compiler_params=pltpu.CompilerParams(
            dimension_semantics=("arbitrary", "arbitrary"), vmem_limit_bytes=VMEM_LIMIT),
        name="ssd",
    )(u3, u3, u3, u3, dt_t, dt_t, conv_w, conv_w, conv_w, conv_b, conv_b, conv_b,
      head_params, dsk_row, norm_w)


def _group_mean_sq(v, seg):
    sq = v * v
    hi = sq.astype(BF16)
    lo = (sq - hi.astype(F32)).astype(BF16)
    return (_dot(hi, seg) + _dot(lo, seg)) * (1.0 / 64.0)


def _seg_matrix():
    r = lax.broadcasted_iota(jnp.int32, (LANES, LANES), 0) // 64
    c = lax.broadcasted_iota(jnp.int32, (LANES, LANES), 1) // 64
    return jnp.where(r == c, 1.0, 0.0).astype(BF16)


DIFF_TQ = 256
DIFF_HEADS_PER_STEP = 2


def _diff_kernel(lam_init, q_ref, k_ref, v_ref, g_ref, qc_ref, qs_ref, kc_ref, ks_ref, lam_ref,
                 sub_ref, o_ref, q_scr, k_scr, vt_scr, sa_scr, sb_scr):
    L = SEQ
    seg = _seg_matrix()
    comp0 = lax.broadcasted_iota(jnp.int32, (DIFF_TQ, LANES), 1) < 64
    p_row = lax.broadcasted_iota(jnp.int32, (LANES, LANES), 0)
    p_col = lax.broadcasted_iota(jnp.int32, (LANES, LANES), 1)
    swap = jnp.where((p_row ^ 32) == p_col, 1.0, 0.0).astype(BF16)

    def norm_rope(ref, r, lanes, cw_ref, sw_ref):
        vb = ref[0, pl.ds(r, DIFF_TQ), lanes]
        v = vb.astype(F32)
        rinv = lax.rsqrt(_group_mean_sq(v, seg) + EPS)
        return rinv * (v * cw_ref[pl.ds(r, DIFF_TQ), :] + _dot(vb, swap) * sw_ref[pl.ds(r, DIFF_TQ), :])

    n_blk = L // DIFF_TQ
    n_items = DIFF_HEADS_PER_STEP * n_blk

    def item(j):
        hd = j // n_blk
        r = pl.multiple_of((j % n_blk) * DIFF_TQ, DIFF_TQ)
        return hd, r, pl.ds(pl.multiple_of(hd * LANES, LANES), LANES)

    for hd in range(DIFF_HEADS_PER_STEP):
        vt_scr[hd, LANES:LANES + ONES_ROWS, :] = jnp.ones((ONES_ROWS, L), BF16)

    def prep(j, carry):
        hd, r, lanes = item(j)
        q_scr[hd, pl.ds(r, DIFF_TQ), :] = norm_rope(q_ref, r, lanes, qc_ref, qs_ref).astype(BF16)
        kn = norm_rope(k_ref, r, lanes, kc_ref, ks_ref)
        k_scr[hd, 0, pl.ds(r, DIFF_TQ), :] = jnp.where(comp0, kn, 0.0).astype(BF16)
        k_scr[hd, 1, pl.ds(r, DIFF_TQ), :] = jnp.where(comp0, 0.0, kn).astype(BF16)
        vt_scr[hd, 0:LANES, pl.ds(r, DIFF_TQ)] = v_ref[0, pl.ds(r, DIFF_TQ), lanes].astype(F32).T.astype(BF16)
        return carry

    lax.fori_loop(0, n_items, prep, 0, unroll=4)

    lam = lam_ref[...]
    lam_full = (jnp.exp(jnp.sum(lam[0:1] * lam[1:2], axis=-1, keepdims=True))
                - jnp.exp(jnp.sum(lam[2:3] * lam[3:4], axis=-1, keepdims=True)) + lam_init)

    def scores(j, s_ref):
        hd, r, _ = item(j)
        qb = q_scr[hd, pl.ds(r, DIFF_TQ), :]
        for c in range(2):
            s_ref[c] = _dot_nt(k_scr[hd, c], qb)

    def finish(j, s_ref):
        hd, r, lanes = item(j)
        parts = []
        for c in range(2):
            s = s_ref[c]
            e = jnp.exp2(s - jnp.max(s, axis=0, keepdims=True)).astype(BF16)
            acc = _dot(vt_scr[hd], e)
            parts.append(acc[0:LANES, :] * (1.0 / acc[LANES:LANES + 1, :]))
        ot = parts[0] - lam_full * parts[1]
        o = ot.T
        o = o * lax.rsqrt(jnp.mean(o * o, axis=-1, keepdims=True) + EPS) * sub_ref[...]
        o = o * (1.0 - lam_init) * _silu(g_ref[0, pl.ds(r, DIFF_TQ), lanes].astype(F32))
        o_ref[0, pl.ds(r, DIFF_TQ), lanes] = o.astype(o_ref.dtype)

    scores(0, sa_scr)

    def pair(i, carry):
        scores(2 * i + 1, sb_scr)
        finish(2 * i, sa_scr)
        scores(2 * i + 2, sa_scr)
        finish(2 * i + 1, sb_scr)
        return carry

    lax.fori_loop(0, n_items // 2 - 1, pair, 0, unroll=2)
    scores(n_items - 1, sb_scr)
    finish(n_items - 2, sa_scr)
    finish(n_items - 1, sb_scr)


def _diff(u3, rope_tabs, lam, subln_w, lam_init):
    b = u3.shape[0]
    L = SEQ
    step_w = DIFF_HEADS_PER_STEP * LANES
    base = COL_DIFF // step_w
    nb = DIFF_WIDTH // step_w

    def spec(part):
        return pl.BlockSpec((1, L, step_w), lambda i, h: (i, 0, base + part * nb + h))

    return pl.pallas_call(
        functools.partial(_diff_kernel, lam_init),
        grid=(b, DIFF_HEADS // DIFF_HEADS_PER_STEP),
        in_specs=[
            spec(0), spec(1), spec(2), spec(3),
            pl.BlockSpec((L, LANES), lambda i, h: (0, 0)),
            pl.BlockSpec((L, LANES), lambda i, h: (0, 0)),
            pl.BlockSpec((L, LANES), lambda i, h: (0, 0)),
            pl.BlockSpec((L, LANES), lambda i, h: (0, 0)),
            pl.BlockSpec((4, DIFF_HEAD_DIM), lambda i, h: (0, 0)),
            pl.BlockSpec((1, LANES), lambda i, h: (0, 0)),
        ],
        out_specs=pl.BlockSpec((1, L, step_w), lambda i, h: (i, 0, h)),
        out_shape=jax.ShapeDtypeStruct((b, L, DIFF_WIDTH), BF16),
        scratch_shapes=[
            pltpu.VMEM((DIFF_HEADS_PER_STEP, L, LANES), BF16),
            pltpu.VMEM((DIFF_HEADS_PER_STEP, 2, L, LANES), BF16),
            pltpu.VMEM((DIFF_HEADS_PER_STEP, LANES + ONES_ROWS, L), BF16),
            pltpu.VMEM((2, L, DIFF_TQ), F32),
            pltpu.VMEM((2, L, DIFF_TQ), F32),
        ],
        compiler_params=pltpu.CompilerParams(
            dimension_semantics=("arbitrary", "arbitrary"), vmem_limit_bytes=VMEM_LIMIT),
        name="diff_attn",
    )(u3, u3, u3, u3, *rope_tabs, lam, subln_w)


def _na_block_geometry(qb):
    r0 = qb * NA_QROWS
    kb = min(max(r0 - NA_KH // 2, 0), ROWS - NA_BAND)
    return r0, kb


def _na_bias_type(qb):
    return 0 if qb == 0 else (2 if qb == NA_NBLK - 1 else 1)


def _na_kernel(q_ref, k_ref, v_ref, g_ref, qkw_ref, bias_ref, o_ref, q_scr, k_scr, vt_scr, s_scr):
    L = SEQ
    seg = _seg_matrix()
    head0_blk = lax.broadcasted_iota(jnp.int32, (NA_QBLK, LANES), 1) < 64
    q_scale = (NA_HEAD_DIM ** -0.5) * LOG2E
    for h in range(2):
        vt_scr[h, NA_HEAD_DIM:NA_HEAD_DIM + ONES_ROWS, :] = jnp.ones((ONES_ROWS, L), BF16)

    def norm(ref, r, w_row):
        v = ref[0, pl.ds(r, NA_QBLK), :].astype(F32)
        return v * lax.rsqrt(_group_mean_sq(v, seg) + EPS) * w_row

    def prep(i, carry):
        r = pl.multiple_of(i * NA_QBLK, NA_QBLK)
        q_scr[pl.ds(r, NA_QBLK), :] = (norm(q_ref, r, qkw_ref[0:1, :]) * q_scale).astype(BF16)
        kn = norm(k_ref, r, qkw_ref[1:2, :])
        k_scr[0, pl.ds(r, NA_QBLK), :] = jnp.where(head0_blk, kn, 0.0).astype(BF16)
        k_scr[1, pl.ds(r, NA_QBLK), :] = jnp.where(head0_blk, 0.0, kn).astype(BF16)
        vt = v_ref[0, pl.ds(r, NA_QBLK), :].astype(F32).T.astype(BF16)
        for h in range(2):
            vt_scr[h, 0:NA_HEAD_DIM, pl.ds(r, NA_QBLK)] = vt[h * NA_HEAD_DIM:(h + 1) * NA_HEAD_DIM, :]
        return carry

    lax.fori_loop(0, L // NA_QBLK, prep, 0, unroll=4)

    def slices(qb):
        r0, kb = _na_block_geometry(qb)
        return (slice(r0 * GRID_W, r0 * GRID_W + NA_QBLK), slice(kb * GRID_W, kb * GRID_W + NA_KBLK))

    def scores(qb):
        qs, ks = slices(qb)
        q_blk = q_scr[qs, :]
        for h in range(2):
            s_scr[qb % 2, h] = _dot_nt(k_scr[h, ks, :], q_blk)

    def finish(qb):
        qs, ks = slices(qb)
        t = _na_bias_type(qb)
        outs = []
        for h in range(2):
            s = s_scr[qb % 2, h] + bias_ref[h, t]
            e = jnp.exp2(s - jnp.max(s, axis=0, keepdims=True)).astype(BF16)
            acc = _dot(vt_scr[h, :, ks], e)
            outs.append(acc[0:NA_HEAD_DIM, :] * (1.0 / acc[NA_HEAD_DIM:NA_HEAD_DIM + 1, :]))
        o = jnp.concatenate(outs, axis=0).T
        o = o * _silu(g_ref[0, qs, :].astype(F32))
        o_ref[0, qs, :] = o.astype(o_ref.dtype)

    scores(0)
    for qb in range(NA_NBLK):
        if qb + 1 < NA_NBLK:
            scores(qb + 1)
        finish(qb)


def _na(u3, qk_w, bias, layer):
    b = u3.shape[0]
    L = SEQ
    base = COL_NA // LANES
    nb = NA_WIDTH // LANES

    def spec(part):
        return pl.BlockSpec((1, L, LANES), lambda hp, i: (i, 0, base + part * nb + hp))

    return pl.pallas_call(
        _na_kernel,
        grid=(NA_HEADS // 2, b),
        in_specs=[
            spec(0), spec(1), spec(2), spec(3),
            pl.BlockSpec((2, LANES), lambda hp, i: (0, 0)),
            pl.BlockSpec((None, 2, 3, NA_KBLK, NA_QBLK), lambda hp, i: (layer, hp, 0, 0, 0)),
        ],
        out_specs=pl.BlockSpec((1, L, LANES), lambda hp, i: (i, 0, hp)),
        out_shape=jax.ShapeDtypeStruct((b, L, NA_WIDTH), BF16),
        scratch_shapes=[
            pltpu.VMEM((L, LANES), BF16),
            pltpu.VMEM((2, L, LANES), BF16),
            pltpu.VMEM((2, NA_HEAD_DIM + ONES_ROWS, L), BF16),
            pltpu.VMEM((2, 2, NA_KBLK, NA_QBLK), F32),
        ],
        compiler_params=pltpu.CompilerParams(
            dimension_semantics=("arbitrary", "arbitrary"), vmem_limit_bytes=VMEM_LIMIT),
        name="na_attn",
    )(u3, u3, u3, u3, qk_w, bias)


def _na_bias(rpb):
    c = np.arange(GRID_W)
    col_start = np.clip(c - NA_KW // 2, 0, GRID_W - NA_KW)
    col_ok = (c[None, :] >= col_start[:, None]) & (c[None, :] < col_start[:, None] + NA_KW)
    lead = rpb.shape[:-2]
    pad = GRID_W - NA_KW
    rev = jnp.pad(rpb.astype(F32)[..., ::-1] * LOG2E, [(0, 0)] * (rpb.ndim - 1) + [(pad, pad)])
    n_rev = 2 * GRID_W - 1
    skew = jnp.tile(rev, GRID_W + 1)[..., :GRID_W * (n_rev + 1)].reshape(rev.shape[:-1] + (GRID_W, n_rev + 1))
    tiles = skew[..., ::-1, :GRID_W]
    tiles = jnp.where(col_ok.T, tiles, NEG)
    tiles = jnp.concatenate([tiles, tiles], axis=-1)
    n_dr = 2 * NA_KH - 1
    tiles = tiles.reshape((-1, n_dr, GRID_W, LANES))
    out = pl.pallas_call(
        _na_bias_kernel,
        grid=(tiles.shape[0],),
        in_specs=[pl.BlockSpec((1, n_dr, GRID_W, LANES), lambda n: (n, 0, 0, 0))],
        out_specs=pl.BlockSpec((1, 3, NA_KBLK, NA_QBLK), lambda n: (n, 0, 0, 0)),
        out_shape=jax.ShapeDtypeStruct((tiles.shape[0], 3, NA_KBLK, NA_QBLK), F32),
        compiler_params=pltpu.CompilerParams(
            dimension_semantics=("arbitrary",), vmem_limit_bytes=VMEM_LIMIT),
        name="na_bias",
    )(tiles)
    return out.reshape(lead + (3, NA_KBLK, NA_QBLK))


def _na_bias_kernel(t_ref, o_ref):
    left = lax.broadcasted_iota(jnp.int32, (GRID_W, LANES), 1) < GRID_W
    masked = jnp.full((GRID_W, LANES), NEG, F32)

    def tile(qb, j, i):
        r0, kb = _na_block_geometry(qb)
        r, rk = r0 + i, kb + j
        rs = min(max(r - NA_KH // 2, 0), ROWS - NA_KH)
        return t_ref[0, rk - r + NA_KH - 1] if rs <= rk < rs + NA_KH else masked

    for t, qb in enumerate((0, 1, NA_NBLK - 1)):
        for j in range(NA_BAND):
            for p in range(NA_QROWS // 2):
                o_ref[0, t, j * GRID_W:(j + 1) * GRID_W, p * LANES:(p + 1) * LANES] = jnp.where(
                    left, tile(qb, j, 2 * p), tile(qb, j, 2 * p + 1))


OUT_TM = 1024


def _outproj_kernel(x_ref, ys_ref, yd_ref, yn_ref, w_ref, o_ref):
    acc = _dot(ys_ref[...], w_ref[0:SSD_WIDTH, :])
    acc = acc + _dot(yd_ref[...], w_ref[SSD_WIDTH:SSD_WIDTH + DIFF_WIDTH, :])
    acc = acc + _dot(yn_ref[...], w_ref[SSD_WIDTH + DIFF_WIDTH:MIX_WIDTH, :])
    o_ref[...] = x_ref[...] + acc


def _outproj(x2, y_ssd, y_diff, y_na, w_out):
    m = x2.shape[0]
    return pl.pallas_call(
        _outproj_kernel,
        grid=(m // OUT_TM,),
        in_specs=[
            pl.BlockSpec((OUT_TM, D_MODEL), lambda i: (i, 0)),
            pl.BlockSpec((OUT_TM, SSD_WIDTH), lambda i: (i, 0)),
            pl.BlockSpec((OUT_TM, DIFF_WIDTH), lambda i: (i, 0)),
            pl.BlockSpec((OUT_TM, NA_WIDTH), lambda i: (i, 0)),
            pl.BlockSpec((MIX_WIDTH, D_MODEL), lambda i: (0, 0)),
        ],
        out_specs=pl.BlockSpec((OUT_TM, D_MODEL), lambda i: (i, 0)),
        out_shape=jax.ShapeDtypeStruct((m, D_MODEL), F32),
        compiler_params=pltpu.CompilerParams(
            dimension_semantics=("arbitrary",), vmem_limit_bytes=VMEM_LIMIT),
        name="outproj",
    )(x2, y_ssd, y_diff, y_na, w_out)


def _rope_tables():
    inv_freq = ROPE_THETA ** (-jnp.arange(0, DIFF_HEAD_DIM, 2, dtype=F32) / DIFF_HEAD_DIM)
    ang = jnp.arange(SEQ, dtype=F32)[:, None] * inv_freq[None, :]
    cos, sin = jnp.cos(ang), jnp.sin(ang)
    cos_t = jnp.concatenate([cos, cos, cos, cos], axis=1)
    sin_t = jnp.concatenate([-sin, sin, -sin, sin], axis=1)
    return cos_t, sin_t


def kernel(x, norm_w, w_in, conv_w, conv_b, a_log, dt_bias, d_skip, ssd_norm_w, diff_qk_norm,
           diff_lambda, diff_subln, na_qk_norm, na_rpb, w_out):
    b, L, d = x.shape
    assert (L, d) == (SEQ, D_MODEL)
    depth = w_in.shape[0]
    cos_t, sin_t = _rope_tables()
    na_bias = _na_bias(na_rpb)
    dt_lo = SSD_WIDTH + SSD_XBC
    dt_hi = dt_lo + 2 * SSD_HEADS
    w_t = jnp.swapaxes(w_in, 1, 2)
    w_main = jnp.concatenate([w_t[:, :dt_lo], w_t[:, dt_hi:]], axis=1).astype(BF16)
    w_dt = jnp.pad(w_t[:, dt_lo:dt_hi], ((0, 0), (0, DT_PAD - 2 * SSD_HEADS), (0, 0))).astype(BF16)
    x2 = x.reshape(b * L, d)
    for i in range(depth):
        lam_init = 0.8 - 0.6 * math.exp(-0.3 * i)
        u, dt = _inproj(x2, norm_w[i][None, :], w_main, w_dt, i)
        u3 = u.reshape(b, L, U_MAIN)
        dt_t = jnp.swapaxes(dt.reshape(b, L, DT_PAD)[:, :, :2 * SSD_HEADS], 1, 2)

        hp = jnp.stack([a_log[i][0], a_log[i][1], dt_bias[i][0], dt_bias[i][1]], axis=-1)
        hp = jnp.pad(hp, ((0, 0), (0, 4))).reshape(SSD_GROUPS, GROUP_HEADS, 8)
        dsk_row = jnp.repeat(d_skip[i][0] + d_skip[i][1], SSD_HEAD_DIM).reshape(SSD_GROUPS, 1, GROUP_WIDTH)
        y_ssd = _ssd(u3, dt_t, conv_w[i], conv_b[i][None, :], hp, dsk_row,
                     ssd_norm_w[i].reshape(SSD_GROUPS, 1, GROUP_WIDTH))

        half = DIFF_HEAD_DIM // 2
        qk_w = jnp.tile(diff_qk_norm[i], (1, 2))
        qk_w_sw = jnp.tile(jnp.concatenate([diff_qk_norm[i][:, half:], diff_qk_norm[i][:, :half]], axis=1), (1, 2))
        q_scale = (DIFF_HEAD_DIM ** -0.5) * LOG2E
        rope_tabs = (cos_t * (qk_w[0:1] * q_scale), sin_t * (qk_w_sw[0:1] * q_scale),
                     cos_t * qk_w[1:2], sin_t * qk_w_sw[1:2])
        y_diff = _diff(u3, rope_tabs, diff_lambda[i], diff_subln[i][None, :], lam_init)

        na_w = jnp.concatenate([na_qk_norm[i], na_qk_norm[i]], axis=1)
        y_na = _na(u3, na_w, na_bias, i)

        x2 = _outproj(x2, y_ssd.reshape(b * L, SSD_WIDTH), y_diff.reshape(b * L, DIFF_WIDTH),
                      y_na.reshape(b * L, NA_WIDTH), w_out[i].astype(BF16))
    return x2.reshape(b, L, d)
```

```python
import functools
import math

import numpy as np
import jax
import jax.numpy as jnp
from jax import lax
from jax.experimental import pallas as pl
from jax.experimental.pallas import tpu as pltpu

F32 = jnp.float32
BF16 = jnp.bfloat16

D_MODEL = 1024
SEQ = 2048
GRID_W = 64
ROWS = SEQ // GRID_W
SSD_WIDTH = 1024
SSD_HEAD_DIM = 64
SSD_HEADS = 16
SSD_GROUPS = 2
SSD_STATE = 128
SSD_CONV = 5
SSD_CHUNK = 128
SSD_XBC = SSD_WIDTH + 2 * SSD_GROUPS * SSD_STATE
GROUP_HEADS = SSD_HEADS // SSD_GROUPS
GROUP_WIDTH = SSD_WIDTH // SSD_GROUPS
N_CHUNKS = SEQ // SSD_CHUNK
DIFF_WIDTH = 512
DIFF_HEAD_DIM = 64
DIFF_HEADS = 4
NA_WIDTH = 512
NA_HEAD_DIM = 64
NA_HEADS = 8
NA_KH = 8
NA_KW = 16
MIX_WIDTH = SSD_WIDTH + DIFF_WIDTH + NA_WIDTH
ROPE_THETA = 10000.0
EPS = 1e-6
LANES = 128
CONV_HALO = 16
CONV_WIN = SSD_CHUNK + 2 * CONV_HALO

U_MAIN = SSD_WIDTH + SSD_XBC + 4 * DIFF_WIDTH + 4 * NA_WIDTH
COL_Z = 0
COL_X = SSD_WIDTH
COL_B = COL_X + SSD_WIDTH
COL_C = COL_B + SSD_GROUPS * SSD_STATE
COL_DIFF = COL_C + SSD_GROUPS * SSD_STATE
COL_NA = COL_DIFF + 4 * DIFF_WIDTH
DT_PAD = LANES

NA_QROWS = 4
NA_BAND = 12
NA_QBLK = NA_QROWS * GRID_W
NA_KBLK = NA_BAND * GRID_W
NA_NBLK = ROWS // NA_QROWS
NEG = -1e30
LOG2E = math.log2(math.e)
ONES_ROWS = 16

VMEM_LIMIT = 56 * 1024 * 1024


def _silu(v):
    h = 0.5 * v
    return h * jnp.tanh(h) + h


def _dot(a, b):
    return jnp.dot(a, b, preferred_element_type=F32)


def _dot_nt(a, b):
    return lax.dot_general(a, b, (((1,), (1,)), ((), ())), preferred_element_type=F32)


def _dot_tn(a, b):
    return lax.dot_general(a, b, (((0,), (0,)), ((), ())), preferred_element_type=F32)


IN_TM = 1024
IN_TN = 3328


def _inproj_kernel(x_ref, nw_ref, w_ref, wdt_ref, u_ref, dt_ref, h_scr):
    @pl.when(pl.program_id(1) == 0)
    def _():
        x = x_ref[...]
        ms = jnp.mean(x * x, axis=-1, keepdims=True)
        h = (x * lax.rsqrt(ms + EPS) * nw_ref[...]).astype(BF16)
        h_scr[...] = h
        dt_ref[...] = _dot_nt(h, wdt_ref[...])

    u_ref[...] = _dot_nt(h_scr[...], w_ref[...]).astype(u_ref.dtype)


def _inproj(x2, norm_w, w_main, w_dt, layer):
    m = x2.shape[0]
    return pl.pallas_call(
        _inproj_kernel,
        grid=(m // IN_TM, U_MAIN // IN_TN),
        in_specs=[
            pl.BlockSpec((IN_TM, D_MODEL), lambda i, j: (i, 0)),
            pl.BlockSpec((1, D_MODEL), lambda i, j: (0, 0)),
            pl.BlockSpec((None, IN_TN, D_MODEL), lambda i, j: (layer, j, 0)),
            pl.BlockSpec((None, DT_PAD, D_MODEL), lambda i, j: (layer, 0, 0)),
        ],
        out_specs=[
            pl.BlockSpec((IN_TM, IN_TN), lambda i, j: (i, j)),
            pl.BlockSpec((IN_TM, DT_PAD), lambda i, j: (i, 0)),
        ],
        out_shape=[
            jax.ShapeDtypeStruct((m, U_MAIN), BF16),
            jax.ShapeDtypeStruct((m, DT_PAD), F32),
        ],
        scratch_shapes=[pltpu.VMEM((IN_TM, D_MODEL), BF16)],
        compiler_params=pltpu.CompilerParams(
            dimension_semantics=("arbitrary", "arbitrary"), vmem_limit_bytes=VMEM_LIMIT),
        name="inproj",
    )(x2, norm_w, w_main, w_dt)


R_WF, R_DF = 0, 1
R_WB, R_DB = 4, 5
R_COLF, R_COLB = 8, 9
R_EF, R_EB = 10, 11
R_ROWF, R_ROWB = 14, 15


def _softplus(v):
    return jnp.maximum(v, 0.0) + jnp.log1p(jnp.exp(-jnp.abs(v)))


def _chunk_scan(a, lane, reverse):
    n = a.shape[-1]
    out = a
    k = 1
    while k < SSD_CHUNK:
        if reverse:
            out = out + jnp.where(lane < SSD_CHUNK - k, pltpu.roll(out, n - k, axis=1), 0.0)
        else:
            out = out + jnp.where(lane >= k, pltpu.roll(out, k, axis=1), 0.0)
        k *= 2
    return out


def _ssd_kernel(z_ref, x_ref, b_ref, c_ref, dtf_ref, dtb_ref, cwx_ref, cwb_ref, cwc_ref,
                cbx_ref, cbb_ref, cbc_ref, hp_ref, dsk_ref, nw_ref, o_ref,
                shift_scr, xs_scr, bt_scr, cm_scr, y_scr, hm_scr, tm_scr, dec_scr, sf_scr, sb_scr):
    L = SEQ
    W = GROUP_WIDTH
    NS = SSD_STATE

    side = (SSD_CONV - 1) * SSD_CHUNK
    tap_t = lax.broadcasted_iota(jnp.int32, (side, CONV_WIN), 0)
    tap_j = lax.broadcasted_iota(jnp.int32, (side, CONV_WIN), 1)
    tap_k = tap_t // SSD_CHUNK
    tap_src = tap_t % SSD_CHUNK + jnp.where(tap_k >= SSD_CONV // 2, tap_k + 1, tap_k) - SSD_CONV // 2
    for variant in range(3):
        shift_scr[variant] = jnp.where(tap_j == tap_src + variant * CONV_HALO, 1.0, 0.0).astype(BF16)
    side_taps = [k for k in range(SSD_CONV) if k != SSD_CONV // 2]

    def conv(c, carry):
        r = pl.multiple_of(c * SSD_CHUNK, SSD_CHUNK)
        w0 = pl.multiple_of(jnp.clip(r - CONV_HALO, 0, L - CONV_WIN), CONV_HALO)
        variant = jnp.where(c == 0, 0, jnp.where(c == N_CHUNKS - 1, 2, 1))
        shift = shift_scr[variant]
        rows = pl.ds(r, SSD_CHUNK)

        def taps(win, centre, cw, cb):
            sh = _dot(shift, win)
            acc = cb + centre.astype(F32) * cw[SSD_CONV // 2:SSD_CONV // 2 + 1, :]
            for slot, k in enumerate(side_taps):
                acc = acc + sh[slot * SSD_CHUNK:(slot + 1) * SSD_CHUNK, :] * cw[k:k + 1, :]
            return _silu(acc)

        xs_scr[rows, :] = taps(x_ref[0, pl.ds(w0, CONV_WIN), :], x_ref[0, rows, :], cwx_ref[...], cbx_ref[...])
        win_bc = jnp.concatenate([b_ref[0, pl.ds(w0, CONV_WIN), :], c_ref[0, pl.ds(w0, CONV_WIN), :]], axis=1)
        mid_bc = jnp.concatenate([b_ref[0, rows, :], c_ref[0, rows, :]], axis=1)
        act_bc = taps(win_bc, mid_bc, jnp.concatenate([cwb_ref[...], cwc_ref[...]], axis=1),
                      jnp.concatenate([cbb_ref[...], cbc_ref[...]], axis=1))
        bt_scr[:, rows] = act_bc[:, 0:NS].T.astype(BF16)
        cm_scr[rows, :] = act_bc[:, NS:2 * NS].astype(BF16)
        return carry

    lax.fori_loop(0, N_CHUNKS, conv, 0, unroll=4)

    hp = hp_ref[0]
    a_f = -jnp.exp(hp[:, 0:1])
    a_b = -jnp.exp(hp[:, 1:2])
    dt_f = _softplus(dtf_ref[0] + hp[:, 2:3])
    dt_b = _softplus(dtb_ref[0] + hp[:, 3:4])
    da_f = dt_f * a_f
    da_b = dt_b * a_b
    lane = lax.broadcasted_iota(jnp.int32, (GROUP_HEADS, L), 1) % SSD_CHUNK
    cs_f = _chunk_scan(da_f, lane, False)
    rs_f = _chunk_scan(da_f, lane, True) - da_f
    cs_b = _chunk_scan(da_b, lane, False)
    ecs_b = cs_b - da_b
    rs_b = _chunk_scan(da_b, lane, True)

    def put(row, v):
        hm_scr[row * GROUP_HEADS:(row + 1) * GROUP_HEADS, :] = v

    def put_split(row_hi, v):
        hi = v.astype(BF16).astype(F32)
        put(row_hi, hi)
        put(row_hi + 2, (v - hi).astype(BF16).astype(F32))

    put_split(R_WF, jnp.exp(rs_f) * dt_f)
    put_split(R_DF, jnp.exp(cs_f))
    put_split(R_WB, jnp.exp(ecs_b) * dt_b)
    put_split(R_DB, jnp.exp(rs_b))
    put_split(R_EF, jnp.exp(cs_f + rs_f))
    put_split(R_EB, jnp.exp(ecs_b + rs_b))
    put(R_COLF, cs_f * LOG2E)
    put(R_COLB, ecs_b * LOG2E)
    put(R_ROWF, cs_f * LOG2E - jnp.log2(dt_f))
    put(R_ROWB, ecs_b * LOG2E + jnp.log2(dt_b))

    def to_time_major(c, carry):
        r = pl.multiple_of(c * SSD_CHUNK, SSD_CHUNK)
        tm_scr[pl.ds(r, SSD_CHUNK), :] = hm_scr[:, pl.ds(r, SSD_CHUNK)].T
        return carry

    lax.fori_loop(0, N_CHUNKS, to_time_major, 0, unroll=4)

    e_row = lax.broadcasted_iota(jnp.int32, (4 * GROUP_HEADS, 2 * W), 0) % (2 * GROUP_HEADS)
    e_col = lax.broadcasted_iota(jnp.int32, (4 * GROUP_HEADS, 2 * W), 1) // SSD_HEAD_DIM
    spread = jnp.where(e_row == e_col, 1.0, 0.0).astype(BF16)

    def expand(rows, first_row):
        cols = slice(first_row * GROUP_HEADS, (first_row + 4) * GROUP_HEADS)
        return _dot(tm_scr[rows, :][:, cols].astype(BF16), spread)

    dec_scr[...] = expand(pl.ds(0, N_CHUNKS, stride=SSD_CHUNK), R_EF)

    row_i = lax.broadcasted_iota(jnp.int32, (SSD_CHUNK, SSD_CHUNK), 0)
    col_i = lax.broadcasted_iota(jnp.int32, (SSD_CHUNK, SSD_CHUNK), 1)
    lower = col_i <= row_i
    upper = col_i >= row_i
    left = lax.broadcasted_iota(jnp.int32, (SSD_CHUNK, LANES), 1) < SSD_HEAD_DIM
    dsk = dsk_ref[0]
    nw = nw_ref[0]

    def hm_row(row, h, r):
        return hm_scr[row * GROUP_HEADS + h:row * GROUP_HEADS + h + 1, pl.ds(r, SSD_CHUNK)]

    def forward_part(c):
        r = pl.multiple_of(c * SSD_CHUNK, SSD_CHUNK)
        rows = pl.ds(r, SSD_CHUNK)
        xs_c = xs_scr[rows, :]
        bt_c = bt_scr[:, rows]
        cm_c = cm_scr[rows, :]
        g = _dot(cm_c, bt_c)
        xs_b = xs_c.astype(BF16)
        colf = tm_scr[rows, R_COLF * GROUP_HEADS:(R_COLF + 1) * GROUP_HEADS]
        colb = tm_scr[rows, R_COLB * GROUP_HEADS:(R_COLB + 1) * GROUP_HEADS]
        y_pairs = []
        for hp_i in range(GROUP_HEADS // 2):
            ms = []
            for h in (2 * hp_i, 2 * hp_i + 1):
                seg_f = jnp.where(lower, colf[:, h:h + 1] - hm_row(R_ROWF, h, r), NEG)
                seg_b = jnp.where(upper, hm_row(R_ROWB, h, r) - colb[:, h:h + 1], NEG)
                ms.append((g * (jnp.exp2(seg_f) + jnp.exp2(seg_b))).astype(BF16))
            xp = xs_b[:, hp_i * LANES:(hp_i + 1) * LANES]
            zero = jnp.zeros_like(xp)
            rhs = jnp.concatenate([jnp.where(left, xp, zero), jnp.where(left, zero, xp)], axis=0)
            y_pairs.append(_dot(jnp.concatenate(ms, axis=1), rhs))
        y_diag = jnp.concatenate(y_pairs, axis=1)
        ex = expand(rows, R_WF)
        s_f = sf_scr[...]
        y_off = _dot(cm_c, s_f.astype(BF16)) * ex[:, W:2 * W]
        xw = (xs_c * ex[:, 0:W]).astype(BF16)
        sf_scr[...] = s_f * dec_scr[pl.ds(c, 1), 0:W] + _dot(bt_c, xw)
        return y_diag + y_off

    def backward_part(c):
        r = pl.multiple_of(c * SSD_CHUNK, SSD_CHUNK)
        rows = pl.ds(r, SSD_CHUNK)
        ex = expand(rows, R_WB)
        s_b = sb_scr[...]
        y_off = _dot(cm_scr[rows, :], s_b.astype(BF16)) * ex[:, W:2 * W]
        xw = (xs_scr[rows, :] * ex[:, 0:W]).astype(BF16)
        sb_scr[...] = s_b * dec_scr[pl.ds(c, 1), W:2 * W] + _dot(bt_scr[:, rows], xw)
        return y_off

    def finalize(c, y):
        r = pl.multiple_of(c * SSD_CHUNK, SSD_CHUNK)
        rows = pl.ds(r, SSD_CHUNK)
        y = y + y_scr[rows, :] + dsk * xs_scr[rows, :]
        y = y * _silu(z_ref[0, rows, :].astype(F32))
        y = y * lax.rsqrt(jnp.mean(y * y, axis=-1, keepdims=True) + EPS) * nw
        o_ref[0, rows, :] = y.astype(o_ref.dtype)

    sf_scr[...] = jnp.zeros((NS, W), F32)
    sb_scr[...] = jnp.zeros((NS, W), F32)
    half = N_CHUNKS // 2

    def first_half(i, carry):
        cb = N_CHUNKS - 1 - i
        y_scr[pl.ds(pl.multiple_of(i * SSD_CHUNK, SSD_CHUNK), SSD_CHUNK), :] = forward_part(i)
        y_scr[pl.ds(pl.multiple_of(cb * SSD_CHUNK, SSD_CHUNK), SSD_CHUNK), :] = backward_part(cb)
        return carry

    def second_half(i, carry):
        cb = N_CHUNKS - 1 - i
        finalize(i, forward_part(i))
        finalize(cb, backward_part(cb))
        return carry

    lax.fori_loop(0, half, first_half, 0, unroll=True)
    lax.fori_loop(half, N_CHUNKS, second_half, 0, unroll=True)


def _ssd(u3, dt_t, conv_w, conv_b, head_params, dsk_row, norm_w):
    b = u3.shape[0]
    L = SEQ
    W = GROUP_WIDTH
    NS = SSD_STATE
    G = SSD_GROUPS
    xblk = COL_X // W
    bblk = COL_B // NS
    cblk = COL_C // NS
    in_specs = [
        pl.BlockSpec((1, L, W), lambda i, g: (i, 0, g)),
        pl.BlockSpec((1, L, W), lambda i, g: (i, 0, xblk + g)),
        pl.BlockSpec((1, L, NS), lambda i, g: (i, 0, bblk + g)),
        pl.BlockSpec((1, L, NS), lambda i, g: (i, 0, cblk + g)),
        pl.BlockSpec((1, GROUP_HEADS, L), lambda i, g: (i, g, 0)),
        pl.BlockSpec((1, GROUP_HEADS, L), lambda i, g: (i, G + g, 0)),
        pl.BlockSpec((SSD_CONV, W), lambda i, g: (0, g)),
        pl.BlockSpec((SSD_CONV, NS), lambda i, g: (0, SSD_WIDTH // NS + g)),
        pl.BlockSpec((SSD_CONV, NS), lambda i, g: (0, SSD_WIDTH // NS + G + g)),
        pl.BlockSpec((1, W), lambda i, g: (0, g)),
        pl.BlockSpec((1, NS), lambda i, g: (0, SSD_WIDTH // NS + g)),
        pl.BlockSpec((1, NS), lambda i, g: (0, SSD_WIDTH // NS + G + g)),
        pl.BlockSpec((1, GROUP_HEADS, 8), lambda i, g: (g, 0, 0)),
        pl.BlockSpec((1, 1, W), lambda i, g: (g, 0, 0)),
        pl.BlockSpec((1, 1, W), lambda i, g: (g, 0, 0)),
    ]
    return pl.pallas_call(
        _ssd_kernel,
        grid=(b, G),
        in_specs=in_specs,
        out_specs=pl.BlockSpec((1, L, W), lambda i, g: (i, 0, g)),
        out_shape=jax.ShapeDtypeStruct((b, L, SSD_WIDTH), BF16),
        scratch_shapes=[
            pltpu.VMEM((3, (SSD_CONV - 1) * SSD_CHUNK, CONV_WIN), BF16),
            pltpu.VMEM((L, W), F32),
            pltpu.VMEM((NS, L), BF16),
            pltpu.VMEM((L, NS), BF16),
            pltpu.VMEM((L, W), F32),
            pltpu.VMEM((LANES, L), F32),
            pltpu.VMEM((L, LANES), F32),
            pltpu.VMEM((N_CHUNKS, 2 * W), F32),
            pltpu.VMEM((NS, W), F32),
            pltpu.VMEM((NS, W), F32),
        ],
        compiler_params=pltpu.CompilerParams(
            dimension_semantics=("arbitrary", "arbitrary"), vmem_limit_bytes=VMEM_LIMIT),
        name="ssd",
    )(u3, u3, u3, u3, dt_t, dt_t, conv_w, conv_w, conv_w, conv_b, conv_b, conv_b,
      head_params, dsk_row, norm_w)


def _group_mean_sq(v, seg):
    return _dot((v * v).astype(BF16), seg) * (1.0 / 64.0)


def _seg_matrix():
    r = lax.broadcasted_iota(jnp.int32, (LANES, LANES), 0) // 64
    c = lax.broadcasted_iota(jnp.int32, (LANES, LANES), 1) // 64
    return jnp.where(r == c, 1.0, 0.0).astype(BF16)


DIFF_TQ = 256
DIFF_HEADS_PER_STEP = 2


def _diff_kernel(lam_init, q_ref, k_ref, v_ref, g_ref, qc_ref, qs_ref, kc_ref, ks_ref, lam_ref,
                 sub_ref, o_ref, q_scr, k_scr, vt_scr, sa_scr, sb_scr):
    L = SEQ
    seg = _seg_matrix()
    comp0 = lax.broadcasted_iota(jnp.int32, (DIFF_TQ, LANES), 1) < 64
    p_row = lax.broadcasted_iota(jnp.int32, (LANES, LANES), 0)
    p_col = lax.broadcasted_iota(jnp.int32, (LANES, LANES), 1)
    swap = jnp.where((p_row ^ 32) == p_col, 1.0, 0.0).astype(BF16)

    def norm_rope(ref, r, lanes, cw_ref, sw_ref):
        vb = ref[0, pl.ds(r, DIFF_TQ), lanes]
        v = vb.astype(F32)
        rinv = lax.rsqrt(_group_mean_sq(v, seg) + EPS)
        return rinv * (v * cw_ref[pl.ds(r, DIFF_TQ), :] + _dot(vb, swap) * sw_ref[pl.ds(r, DIFF_TQ), :])

    n_blk = L // DIFF_TQ
    n_items = DIFF_HEADS_PER_STEP * n_blk

    def item(j):
        hd = j // n_blk
        r = pl.multiple_of((j % n_blk) * DIFF_TQ, DIFF_TQ)
        return hd, r, pl.ds(pl.multiple_of(hd * LANES, LANES), LANES)

    for hd in range(DIFF_HEADS_PER_STEP):
        vt_scr[hd, LANES:LANES + ONES_ROWS, :] = jnp.ones((ONES_ROWS, L), BF16)

    def prep(j, carry):
        hd, r, lanes = item(j)
        q_scr[hd, pl.ds(r, DIFF_TQ), :] = norm_rope(q_ref, r, lanes, qc_ref, qs_ref).astype(BF16)
        kn = norm_rope(k_ref, r, lanes, kc_ref, ks_ref)
        k_scr[hd, 0, pl.ds(r, DIFF_TQ), :] = jnp.where(comp0, kn, 0.0).astype(BF16)
        k_scr[hd, 1, pl.ds(r, DIFF_TQ), :] = jnp.where(comp0, 0.0, kn).astype(BF16)
        vt_scr[hd, 0:LANES, pl.ds(r, DIFF_TQ)] = v_ref[0, pl.ds(r, DIFF_TQ), lanes].astype(F32).T.astype(BF16)
        return carry

    lax.fori_loop(0, n_items, prep, 0, unroll=4)

    lam = lam_ref[...]
    lam_full = (jnp.exp(jnp.sum(lam[0:1] * lam[1:2], axis=-1, keepdims=True))
                - jnp.exp(jnp.sum(lam[2:3] * lam[3:4], axis=-1, keepdims=True)) + lam_init)

    def scores(j, s_ref):
        hd, r, _ = item(j)
        qb = q_scr[hd, pl.ds(r, DIFF_TQ), :]
        for c in range(2):
            s_ref[c] = _dot_nt(k_scr[hd, c], qb)

    def finish(j, s_ref):
        hd, r, lanes = item(j)
        parts = []
        for c in range(2):
            s = s_ref[c]
            e = jnp.exp2(s - jnp.max(s, axis=0, keepdims=True)).astype(BF16)
            acc = _dot(vt_scr[hd], e)
            parts.append(acc[0:LANES, :] * (1.0 / acc[LANES:LANES + 1, :]))
        ot = parts[0] - lam_full * parts[1]
        o = ot.T
        o = o * lax.rsqrt(jnp.mean(o * o, axis=-1, keepdims=True) + EPS) * sub_ref[...]
        o = o * (1.0 - lam_init) * _silu(g_ref[0, pl.ds(r, DIFF_TQ), lanes].astype(F32))
        o_ref[0, pl.ds(r, DIFF_TQ), lanes] = o.astype(o_ref.dtype)

    scores(0, sa_scr)

    def pair(i, carry):
        scores(2 * i + 1, sb_scr)
        finish(2 * i, sa_scr)
        scores(2 * i + 2, sa_scr)
        finish(2 * i + 1, sb_scr)
        return carry

    lax.fori_loop(0, n_items // 2 - 1, pair, 0, unroll=2)
    scores(n_items - 1, sb_scr)
    finish(n_items - 2, sa_scr)
    finish(n_items - 1, sb_scr)


def _diff(u3, rope_tabs, lam, subln_w, lam_init):
    b = u3.shape[0]
    L = SEQ
    step_w = DIFF_HEADS_PER_STEP * LANES
    base = COL_DIFF // step_w
    nb = DIFF_WIDTH // step_w

    def spec(part):
        return pl.BlockSpec((1, L, step_w), lambda i, h: (i, 0, base + part * nb + h))

    return pl.pallas_call(
        functools.partial(_diff_kernel, lam_init),
        grid=(b, DIFF_HEADS // DIFF_HEADS_PER_STEP),
        in_specs=[
            spec(0), spec(1), spec(2), spec(3),
            pl.BlockSpec((L, LANES), lambda i, h: (0, 0)),
            pl.BlockSpec((L, LANES), lambda i, h: (0, 0)),
            pl.BlockSpec((L, LANES), lambda i, h: (0, 0)),
            pl.BlockSpec((L, LANES), lambda i, h: (0, 0)),
            pl.BlockSpec((4, DIFF_HEAD_DIM), lambda i, h: (0, 0)),
            pl.BlockSpec((1, LANES), lambda i, h: (0, 0)),
        ],
        out_specs=pl.BlockSpec((1, L, step_w), lambda i, h: (i, 0, h)),
        out_shape=jax.ShapeDtypeStruct((b, L, DIFF_WIDTH), BF16),
        scratch_shapes=[
            pltpu.VMEM((DIFF_HEADS_PER_STEP, L, LANES), BF16),
            pltpu.VMEM((DIFF_HEADS_PER_STEP, 2, L, LANES), BF16),
            pltpu.VMEM((DIFF_HEADS_PER_STEP, LANES + ONES_ROWS, L), BF16),
            pltpu.VMEM((2, L, DIFF_TQ), F32),
            pltpu.VMEM((2, L, DIFF_TQ), F32),
        ],
        compiler_params=pltpu.CompilerParams(
            dimension_semantics=("arbitrary", "arbitrary"), vmem_limit_bytes=VMEM_LIMIT),
        name="diff_attn",
    )(u3, u3, u3, u3, *rope_tabs, lam, subln_w)


def _na_block_geometry(qb):
    r0 = qb * NA_QROWS
    kb = min(max(r0 - NA_KH // 2, 0), ROWS - NA_BAND)
    return r0, kb


def _na_bias_type(qb):
    return 0 if qb == 0 else (2 if qb == NA_NBLK - 1 else 1)


def _na_kernel(q_ref, k_ref, v_ref, g_ref, qkw_ref, bias_ref, o_ref, q_scr, k_scr, vt_scr, s_scr):
    L = SEQ
    seg = _seg_matrix()
    head0_blk = lax.broadcasted_iota(jnp.int32, (NA_QBLK, LANES), 1) < 64
    q_scale = (NA_HEAD_DIM ** -0.5) * LOG2E
    for h in range(2):
        vt_scr[h, NA_HEAD_DIM:NA_HEAD_DIM + ONES_ROWS, :] = jnp.ones((ONES_ROWS, L), BF16)

    def norm(ref, r, w_row):
        v = ref[0, pl.ds(r, NA_QBLK), :].astype(F32)
        return v * lax.rsqrt(_group_mean_sq(v, seg) + EPS) * w_row

    def prep(i, carry):
        r = pl.multiple_of(i * NA_QBLK, NA_QBLK)
        q_scr[pl.ds(r, NA_QBLK), :] = (norm(q_ref, r, qkw_ref[0:1, :]) * q_scale).astype(BF16)
        kn = norm(k_ref, r, qkw_ref[1:2, :])
        k_scr[0, pl.ds(r, NA_QBLK), :] = jnp.where(head0_blk, kn, 0.0).astype(BF16)
        k_scr[1, pl.ds(r, NA_QBLK), :] = jnp.where(head0_blk, 0.0, kn).astype(BF16)
        vt = v_ref[0, pl.ds(r, NA_QBLK), :].astype(F32).T.astype(BF16)
        for h in range(2):
            vt_scr[h, 0:NA_HEAD_DIM, pl.ds(r, NA_QBLK)] = vt[h * NA_HEAD_DIM:(h + 1) * NA_HEAD_DIM, :]
        return carry

    lax.fori_loop(0, L // NA_QBLK, prep, 0, unroll=4)

    def slices(qb):
        r0, kb = _na_block_geometry(qb)
        return (slice(r0 * GRID_W, r0 * GRID_W + NA_QBLK), slice(kb * GRID_W, kb * GRID_W + NA_KBLK))

    def scores(qb):
        qs, ks = slices(qb)
        q_blk = q_scr[qs, :]
        for h in range(2):
            s_scr[qb % 2, h] = _dot_nt(k_scr[h, ks, :], q_blk)

    def finish(qb):
        qs, ks = slices(qb)
        t = _na_bias_type(qb)
        outs = []
        for h in range(2):
            s = s_scr[qb % 2, h] + bias_ref[h, t]
            e = jnp.exp2(s - jnp.max(s, axis=0, keepdims=True)).astype(BF16)
            acc = _dot(vt_scr[h, :, ks], e)
            outs.append(acc[0:NA_HEAD_DIM, :] * (1.0 / acc[NA_HEAD_DIM:NA_HEAD_DIM + 1, :]))
        o = jnp.concatenate(outs, axis=0).T
        o = o * _silu(g_ref[0, qs, :].astype(F32))
        o_ref[0, qs, :] = o.astype(o_ref.dtype)

    scores(0)
    for qb in range(NA_NBLK):
        if qb + 1 < NA_NBLK:
            scores(qb + 1)
        finish(qb)


def _na(u3, qk_w, bias, layer):
    b = u3.shape[0]
    L = SEQ
    base = COL_NA // LANES
    nb = NA_WIDTH // LANES

    def spec(part):
        return pl.BlockSpec((1, L, LANES), lambda hp, i: (i, 0, base + part * nb + hp))

    return pl.pallas_call(
        _na_kernel,
        grid=(NA_HEADS // 2, b),
        in_specs=[
            spec(0), spec(1), spec(2), spec(3),
            pl.BlockSpec((2, LANES), lambda hp, i: (0, 0)),
            pl.BlockSpec((None, 2, 3, NA_KBLK, NA_QBLK), lambda hp, i: (layer, hp, 0, 0, 0)),
        ],
        out_specs=pl.BlockSpec((1, L, LANES), lambda hp, i: (i, 0, hp)),
        out_shape=jax.ShapeDtypeStruct((b, L, NA_WIDTH), BF16),
        scratch_shapes=[
            pltpu.VMEM((L, LANES), BF16),
            pltpu.VMEM((2, L, LANES), BF16),
            pltpu.VMEM((2, NA_HEAD_DIM + ONES_ROWS, L), BF16),
            pltpu.VMEM((2, 2, NA_KBLK, NA_QBLK), F32),
        ],
        compiler_params=pltpu.CompilerParams(
            dimension_semantics=("arbitrary", "arbitrary"), vmem_limit_bytes=VMEM_LIMIT),
        name="na_attn",
    )(u3, u3, u3, u3, qk_w, bias)


def _na_bias(rpb):
    lead = rpb.shape[:-2]
    n_dr = 2 * NA_KH - 1
    pad = GRID_W - NA_KW
    rev = jnp.pad(rpb.astype(F32)[..., ::-1] * LOG2E, [(0, 0)] * (rpb.ndim - 1) + [(pad, pad + 1)])
    rev = rev.reshape((-1, n_dr, LANES))
    out = pl.pallas_call(
        _na_bias_kernel,
        grid=(rev.shape[0],),
        in_specs=[pl.BlockSpec((1, n_dr, LANES), lambda n: (n, 0, 0))],
        out_specs=pl.BlockSpec((1, 3, NA_KBLK, NA_QBLK), lambda n: (n, 0, 0, 0)),
        out_shape=jax.ShapeDtypeStruct((rev.shape[0], 3, NA_KBLK, NA_QBLK), F32),
        scratch_shapes=[pltpu.VMEM((n_dr, GRID_W, LANES), F32)],
        compiler_params=pltpu.CompilerParams(
            dimension_semantics=("arbitrary",), vmem_limit_bytes=VMEM_LIMIT),
        name="na_bias",
    )(rev)
    return out.reshape(lead + (3, NA_KBLK, NA_QBLK))


def _na_bias_kernel(r_ref, o_ref, t_scr):
    k_i = lax.broadcasted_iota(jnp.int32, (GRID_W, LANES), 0)
    lane = lax.broadcasted_iota(jnp.int32, (GRID_W, LANES), 1)
    left = lane < GRID_W
    win_start = jnp.clip(lane % GRID_W - NA_KW // 2, 0, GRID_W - NA_KW)
    col_ok = (k_i >= win_start) & (k_i < win_start + NA_KW)
    for d in range(2 * NA_KH - 1):
        rows = jnp.broadcast_to(r_ref[0, d:d + 1, :], (GRID_W, LANES))
        skew = pltpu.roll(rows, GRID_W + 1, axis=1, stride=1, stride_axis=0)
        both = jnp.where(left, skew, pltpu.roll(skew, GRID_W, axis=1))
        t_scr[d] = jnp.where(col_ok, both, NEG)

    masked = jnp.full((GRID_W, LANES), NEG, F32)

    def tile(qb, j, i):
        r0, kb = _na_block_geometry(qb)
        r, rk = r0 + i, kb + j
        rs = min(max(r - NA_KH // 2, 0), ROWS - NA_KH)
        return t_scr[rk - r + NA_KH - 1] if rs <= rk < rs + NA_KH else masked

    for t, qb in enumerate((0, 1, NA_NBLK - 1)):
        for j in range(NA_BAND):
            for p in range(NA_QROWS // 2):
                o_ref[0, t, j * GRID_W:(j + 1) * GRID_W, p * LANES:(p + 1) * LANES] = jnp.where(
                    left, tile(qb, j, 2 * p), tile(qb, j, 2 * p + 1))


OUT_TM = 1024


def _outproj_kernel(x_ref, ys_ref, yd_ref, yn_ref, w_ref, o_ref):
    acc = _dot(ys_ref[...], w_ref[0:SSD_WIDTH, :])
    acc = acc + _dot(yd_ref[...], w_ref[SSD_WIDTH:SSD_WIDTH + DIFF_WIDTH, :])
    acc = acc + _dot(yn_ref[...], w_ref[SSD_WIDTH + DIFF_WIDTH:MIX_WIDTH, :])
    o_ref[...] = x_ref[...] + acc


def _outproj(x2, y_ssd, y_diff, y_na, w_out):
    m = x2.shape[0]
    return pl.pallas_call(
        _outproj_kernel,
        grid=(m // OUT_TM,),
        in_specs=[
            pl.BlockSpec((OUT_TM, D_MODEL), lambda i: (i, 0)),
            pl.BlockSpec((OUT_TM, SSD_WIDTH), lambda i: (i, 0)),
            pl.BlockSpec((OUT_TM, DIFF_WIDTH), lambda i: (i, 0)),
            pl.BlockSpec((OUT_TM, NA_WIDTH), lambda i: (i, 0)),
            pl.BlockSpec((MIX_WIDTH, D_MODEL), lambda i: (0, 0)),
        ],
        out_specs=pl.BlockSpec((OUT_TM, D_MODEL), lambda i: (i, 0)),
        out_shape=jax.ShapeDtypeStruct((m, D_MODEL), F32),
        compiler_params=pltpu.CompilerParams(
            dimension_semantics=("arbitrary",), vmem_limit_bytes=VMEM_LIMIT),
        name="outproj",
    )(x2, y_ssd, y_diff, y_na, w_out)


def _rope_tables():
    inv_freq = ROPE_THETA ** (-jnp.arange(0, DIFF_HEAD_DIM, 2, dtype=F32) / DIFF_HEAD_DIM)
    ang = jnp.arange(SEQ, dtype=F32)[:, None] * inv_freq[None, :]
    cos, sin = jnp.cos(ang), jnp.sin(ang)
    cos_t = jnp.concatenate([cos, cos, cos, cos], axis=1)
    sin_t = jnp.concatenate([-sin, sin, -sin, sin], axis=1)
    return cos_t, sin_t


def kernel(x, norm_w, w_in, conv_w, conv_b, a_log, dt_bias, d_skip, ssd_norm_w, diff_qk_norm,
           diff_lambda, diff_subln, na_qk_norm, na_rpb, w_out):
    b, L, d = x.shape
    assert (L, d) == (SEQ, D_MODEL)
    depth = w_in.shape[0]
    cos_t, sin_t = _rope_tables()
    na_bias = _na_bias(na_rpb)
    dt_lo = SSD_WIDTH + SSD_XBC
    dt_hi = dt_lo + 2 * SSD_HEADS
    w_t = jnp.swapaxes(w_in, 1, 2)
    w_main = jnp.concatenate([w_t[:, :dt_lo], w_t[:, dt_hi:]], axis=1).astype(BF16)
    w_dt = jnp.pad(w_t[:, dt_lo:dt_hi], ((0, 0), (0, DT_PAD - 2 * SSD_HEADS), (0, 0))).astype(BF16)
    x2 = x.reshape(b * L, d)
    for i in range(depth):
        lam_init = 0.8 - 0.6 * math.exp(-0.3 * i)
        u, dt = _inproj(x2, norm_w[i][None, :], w_main, w_dt, i)
        u3 = u.reshape(b, L, U_MAIN)
        dt_t = jnp.swapaxes(dt.reshape(b, L, DT_PAD)[:, :, :2 * SSD_HEADS], 1, 2)

        hp = jnp.stack([a_log[i][0], a_log[i][1], dt_bias[i][0], dt_bias[i][1]], axis=-1)
        hp = jnp.pad(hp, ((0, 0), (0, 4))).reshape(SSD_GROUPS, GROUP_HEADS, 8)
        dsk_row = jnp.repeat(d_skip[i][0] + d_skip[i][1], SSD_HEAD_DIM).reshape(SSD_GROUPS, 1, GROUP_WIDTH)
        y_ssd = _ssd(u3, dt_t, conv_w[i], conv_b[i][None, :], hp, dsk_row,
                     ssd_norm_w[i].reshape(SSD_GROUPS, 1, GROUP_WIDTH))

        half = DIFF_HEAD_DIM // 2
        qk_w = jnp.tile(diff_qk_norm[i], (1, 2))
        qk_w_sw = jnp.tile(jnp.concatenate([diff_qk_norm[i][:, half:], diff_qk_norm[i][:, :half]], axis=1), (1, 2))
        q_scale = (DIFF_HEAD_DIM ** -0.5) * LOG2E
        rope_tabs = (cos_t * (qk_w[0:1] * q_scale), sin_t * (qk_w_sw[0:1] * q_scale),
                     cos_t * qk_w[1:2], sin_t * qk_w_sw[1:2])
        y_diff = _diff(u3, rope_tabs, diff_lambda[i], diff_subln[i][None, :], lam_init)

        na_w = jnp.concatenate([na_qk_norm[i], na_qk_norm[i]], axis=1)
        y_na = _na(u3, na_w, na_bias, i)

        x2 = _outproj(x2, y_ssd.reshape(b * L, SSD_WIDTH), y_diff.reshape(b * L, DIFF_WIDTH),
                      y_na.reshape(b * L, NA_WIDTH), w_out[i].astype(BF16))
    return x2.reshape(b, L, d)
```

```python
import functools
import math

import numpy as np
import jax
import jax.numpy as jnp
from jax import lax
from jax.experimental import pallas as pl
from jax.experimental.pallas import tpu as pltpu

F32 = jnp.float32
BF16 = jnp.bfloat16

D_MODEL = 1024
SEQ = 2048
GRID_W = 64
ROWS = SEQ // GRID_W
SSD_WIDTH = 1024
SSD_HEAD_DIM = 64
SSD_HEADS = 16
SSD_GROUPS = 2
SSD_STATE = 128
SSD_CONV = 5
SSD_CHUNK = 128
SSD_XBC = SSD_WIDTH + 2 * SSD_GROUPS * SSD_STATE
GROUP_HEADS = SSD_HEADS // SSD_GROUPS
GROUP_WIDTH = SSD_WIDTH // SSD_GROUPS
N_CHUNKS = SEQ // SSD_CHUNK
DIFF_WIDTH = 512
DIFF_HEAD_DIM = 64
DIFF_HEADS = 4
NA_WIDTH = 512
NA_HEAD_DIM = 64
NA_HEADS = 8
NA_KH = 8
NA_KW = 16
MIX_WIDTH = SSD_WIDTH + DIFF_WIDTH + NA_WIDTH
ROPE_THETA = 10000.0
EPS = 1e-6
LANES = 128
CONV_HALO = 16
CONV_WIN = SSD_CHUNK + 2 * CONV_HALO

U_MAIN = SSD_WIDTH + SSD_XBC + 4 * DIFF_WIDTH + 4 * NA_WIDTH
COL_Z = 0
COL_X = SSD_WIDTH
COL_B = COL_X + SSD_WIDTH
COL_C = COL_B + SSD_GROUPS * SSD_STATE
COL_DIFF = COL_C + SSD_GROUPS * SSD_STATE
COL_NA = COL_DIFF + 4 * DIFF_WIDTH
DT_PAD = LANES

NA_QROWS = 4
NA_BAND = 12
NA_QBLK = NA_QROWS * GRID_W
NA_KBLK = NA_BAND * GRID_W
NA_NBLK = ROWS // NA_QROWS
NEG = -1e30
LOG2E = math.log2(math.e)
ONES_ROWS = 16

VMEM_LIMIT = 56 * 1024 * 1024


def _silu(v):
    h = 0.5 * v
    return h * jnp.tanh(h) + h


def _dot(a, b):
    return jnp.dot(a, b, preferred_element_type=F32)


def _dot_nt(a, b):
    return lax.dot_general(a, b, (((1,), (1,)), ((), ())), preferred_element_type=F32)


def _dot_tn(a, b):
    return lax.dot_general(a, b, (((0,), (0,)), ((), ())), preferred_element_type=F32)


IN_TM = 1024
IN_TN = 3328


def _inproj_kernel(x_ref, nw_ref, w_ref, wdt_ref, u_ref, dt_ref, h_scr):
    @pl.when(pl.program_id(1) == 0)
    def _():
        x = x_ref[...]
        ms = jnp.mean(x * x, axis=-1, keepdims=True)
        h = (x * lax.rsqrt(ms + EPS) * nw_ref[...]).astype(BF16)
        h_scr[...] = h
        dt_ref[...] = _dot_nt(h, wdt_ref[...])

    u_ref[...] = _dot_nt(h_scr[...], w_ref[...]).astype(u_ref.dtype)


def _inproj(x2, norm_w, w_main, w_dt, layer):
    m = x2.shape[0]
    return pl.pallas_call(
        _inproj_kernel,
        grid=(m // IN_TM, U_MAIN // IN_TN),
        in_specs=[
            pl.BlockSpec((IN_TM, D_MODEL), lambda i, j: (i, 0)),
            pl.BlockSpec((1, D_MODEL), lambda i, j: (0, 0)),
            pl.BlockSpec((None, IN_TN, D_MODEL), lambda i, j: (layer, j, 0)),
            pl.BlockSpec((None, DT_PAD, D_MODEL), lambda i, j: (layer, 0, 0)),
        ],
        out_specs=[
            pl.BlockSpec((IN_TM, IN_TN), lambda i, j: (i, j)),
            pl.BlockSpec((IN_TM, DT_PAD), lambda i, j: (i, 0)),
        ],
        out_shape=[
            jax.ShapeDtypeStruct((m, U_MAIN), BF16),
            jax.ShapeDtypeStruct((m, DT_PAD), F32),
        ],
        scratch_shapes=[pltpu.VMEM((IN_TM, D_MODEL), BF16)],
        compiler_params=pltpu.CompilerParams(
            dimension_semantics=("arbitrary", "arbitrary"), vmem_limit_bytes=VMEM_LIMIT),
        name="inproj",
    )(x2, norm_w, w_main, w_dt)


R_WF, R_DF = 0, 1
R_WB, R_DB = 4, 5
R_COLF, R_COLB = 8, 9
R_EF, R_EB = 10, 11
R_ROWF, R_ROWB = 14, 15


def _softplus(v):
    return jnp.maximum(v, 0.0) + jnp.log1p(jnp.exp(-jnp.abs(v)))


def _chunk_scan(a, lane, reverse):
    n = a.shape[-1]
    out = a
    k = 1
    while k < SSD_CHUNK:
        if reverse:
            out = out + jnp.where(lane < SSD_CHUNK - k, pltpu.roll(out, n - k, axis=1), 0.0)
        else:
            out = out + jnp.where(lane >= k, pltpu.roll(out, k, axis=1), 0.0)
        k *= 2
    return out


def _ssd_kernel(z_ref, x_ref, b_ref, c_ref, dtf_ref, dtb_ref, cwx_ref, cwb_ref, cwc_ref,
                cbx_ref, cbb_ref, cbc_ref, hp_ref, dsk_ref, nw_ref, o_ref,
                shift_scr, xs_scr, bt_scr, cm_scr, y_scr, hm_scr, tm_scr, dec_scr, sf_scr, sb_scr):
    L = SEQ
    W = GROUP_WIDTH
    NS = SSD_STATE

    side = (SSD_CONV - 1) * SSD_CHUNK
    tap_t = lax.broadcasted_iota(jnp.int32, (side, CONV_WIN), 0)
    tap_j = lax.broadcasted_iota(jnp.int32, (side, CONV_WIN), 1)
    tap_k = tap_t // SSD_CHUNK
    tap_src = tap_t % SSD_CHUNK + jnp.where(tap_k >= SSD_CONV // 2, tap_k + 1, tap_k) - SSD_CONV // 2
    for variant in range(3):
        shift_scr[variant] = jnp.where(tap_j == tap_src + variant * CONV_HALO, 1.0, 0.0).astype(BF16)
    side_taps = [k for k in range(SSD_CONV) if k != SSD_CONV // 2]

    def conv(c, carry):
        r = pl.multiple_of(c * SSD_CHUNK, SSD_CHUNK)
        w0 = pl.multiple_of(jnp.clip(r - CONV_HALO, 0, L - CONV_WIN), CONV_HALO)
        variant = jnp.where(c == 0, 0, jnp.where(c == N_CHUNKS - 1, 2, 1))
        shift = shift_scr[variant]
        rows = pl.ds(r, SSD_CHUNK)

        def taps(win, centre, cw, cb):
            sh = _dot(shift, win)
            acc = cb + centre.astype(F32) * cw[SSD_CONV // 2:SSD_CONV // 2 + 1, :]
            for slot, k in enumerate(side_taps):
                acc = acc + sh[slot * SSD_CHUNK:(slot + 1) * SSD_CHUNK, :] * cw[k:k + 1, :]
            return _silu(acc)

        xs_scr[rows, :] = taps(x_ref[0, pl.ds(w0, CONV_WIN), :], x_ref[0, rows, :], cwx_ref[...], cbx_ref[...])
        win_bc = jnp.concatenate([b_ref[0, pl.ds(w0, CONV_WIN), :], c_ref[0, pl.ds(w0, CONV_WIN), :]], axis=1)
        mid_bc = jnp.concatenate([b_ref[0, rows, :], c_ref[0, rows, :]], axis=1)
        act_bc = taps(win_bc, mid_bc, jnp.concatenate([cwb_ref[...], cwc_ref[...]], axis=1),
                      jnp.concatenate([cbb_ref[...], cbc_ref[...]], axis=1))
        bt_scr[:, rows] = act_bc[:, 0:NS].T.astype(BF16)
        cm_scr[rows, :] = act_bc[:, NS:2 * NS].astype(BF16)
        return carry

    lax.fori_loop(0, N_CHUNKS, conv, 0, unroll=8)

    hp = hp_ref[0]
    a_f = -jnp.exp(hp[:, 0:1])
    a_b = -jnp.exp(hp[:, 1:2])
    dt_f = _softplus(dtf_ref[0] + hp[:, 2:3])
    dt_b = _softplus(dtb_ref[0] + hp[:, 3:4])
    da_f = dt_f * a_f
    da_b = dt_b * a_b
    lane = lax.broadcasted_iota(jnp.int32, (GROUP_HEADS, L), 1) % SSD_CHUNK
    cs_f = _chunk_scan(da_f, lane, False)
    rs_f = _chunk_scan(da_f, lane, True) - da_f
    cs_b = _chunk_scan(da_b, lane, False)
    ecs_b = cs_b - da_b
    rs_b = _chunk_scan(da_b, lane, True)

    def put(row, v):
        hm_scr[row * GROUP_HEADS:(row + 1) * GROUP_HEADS, :] = v

    def put_split(row_hi, v):
        hi = v.astype(BF16).astype(F32)
        put(row_hi, hi)
        put(row_hi + 2, (v - hi).astype(BF16).astype(F32))

    put_split(R_WF, jnp.exp(rs_f) * dt_f)
    put_split(R_DF, jnp.exp(cs_f))
    put_split(R_WB, jnp.exp(ecs_b) * dt_b)
    put_split(R_DB, jnp.exp(rs_b))
    put_split(R_EF, jnp.exp(cs_f + rs_f))
    put_split(R_EB, jnp.exp(ecs_b + rs_b))
    put(R_COLF, cs_f * LOG2E)
    put(R_COLB, ecs_b * LOG2E)
    put(R_ROWF, cs_f * LOG2E - jnp.log2(dt_f))
    put(R_ROWB, ecs_b * LOG2E + jnp.log2(dt_b))

    def to_time_major(c, carry):
        r = pl.multiple_of(c * SSD_CHUNK, SSD_CHUNK)
        tm_scr[pl.ds(r, SSD_CHUNK), :] = hm_scr[:, pl.ds(r, SSD_CHUNK)].T
        return carry

    lax.fori_loop(0, N_CHUNKS, to_time_major, 0, unroll=True)

    e_row = lax.broadcasted_iota(jnp.int32, (4 * GROUP_HEADS, 2 * W), 0) % (2 * GROUP_HEADS)
    e_col = lax.broadcasted_iota(jnp.int32, (4 * GROUP_HEADS, 2 * W), 1) // SSD_HEAD_DIM
    spread = jnp.where(e_row == e_col, 1.0, 0.0).astype(BF16)

    def expand(rows, first_row):
        cols = slice(first_row * GROUP_HEADS, (first_row + 4) * GROUP_HEADS)
        return _dot(tm_scr[rows, :][:, cols].astype(BF16), spread)

    dec_scr[...] = expand(pl.ds(0, N_CHUNKS, stride=SSD_CHUNK), R_EF)

    row_i = lax.broadcasted_iota(jnp.int32, (SSD_CHUNK, SSD_CHUNK), 0)
    col_i = lax.broadcasted_iota(jnp.int32, (SSD_CHUNK, SSD_CHUNK), 1)
    lower = col_i <= row_i
    upper = col_i >= row_i
    left = lax.broadcasted_iota(jnp.int32, (SSD_CHUNK, LANES), 1) < SSD_HEAD_DIM
    dsk = dsk_ref[0]
    nw = nw_ref[0]

    def hm_row(row, h, r):
        return hm_scr[row * GROUP_HEADS + h:row * GROUP_HEADS + h + 1, pl.ds(r, SSD_CHUNK)]

    def forward_part(c):
        r = pl.multiple_of(c * SSD_CHUNK, SSD_CHUNK)
        rows = pl.ds(r, SSD_CHUNK)
        xs_c = xs_scr[rows, :]
        bt_c = bt_scr[:, rows]
        cm_c = cm_scr[rows, :]
        g = _dot(cm_c, bt_c)
        xs_b = xs_c.astype(BF16)
        colf = tm_scr[rows, R_COLF * GROUP_HEADS:(R_COLF + 1) * GROUP_HEADS]
        colb = tm_scr[rows, R_COLB * GROUP_HEADS:(R_COLB + 1) * GROUP_HEADS]
        y_pairs = []
        for hp_i in range(GROUP_HEADS // 2):
            ms = []
            for h in (2 * hp_i, 2 * hp_i + 1):
                seg_f = jnp.where(lower, colf[:, h:h + 1] - hm_row(R_ROWF, h, r), NEG)
                seg_b = jnp.where(upper, hm_row(R_ROWB, h, r) - colb[:, h:h + 1], NEG)
                ms.append((g * (jnp.exp2(seg_f) + jnp.exp2(seg_b))).astype(BF16))
            xp = xs_b[:, hp_i * LANES:(hp_i + 1) * LANES]
            zero = jnp.zeros_like(xp)
            rhs = jnp.concatenate([jnp.where(left, xp, zero), jnp.where(left, zero, xp)], axis=0)
            y_pairs.append(_dot(jnp.concatenate(ms, axis=1), rhs))
        y_diag = jnp.concatenate(y_pairs, axis=1)
        ex = expand(rows, R_WF)
        s_f = sf_scr[...]
        y_off = _dot(cm_c, s_f.astype(BF16)) * ex[:, W:2 * W]
        xw = (xs_c * ex[:, 0:W]).astype(BF16)
        sf_scr[...] = s_f * dec_scr[pl.ds(c, 1), 0:W] + _dot(bt_c, xw)
        return y_diag + y_off

    def backward_part(c):
        r = pl.multiple_of(c * SSD_CHUNK, SSD_CHUNK)
        rows = pl.ds(r, SSD_CHUNK)
        ex = expand(rows, R_WB)
        s_b = sb_scr[...]
        y_off = _dot(cm_scr[rows, :], s_b.astype(BF16)) * ex[:, W:2 * W]
        xw = (xs_scr[rows, :] * ex[:, 0:W]).astype(BF16)
        sb_scr[...] = s_b * dec_scr[pl.ds(c, 1), W:2 * W] + _dot(bt_scr[:, rows], xw)
        return y_off

    def finalize(c, y):
        r = pl.multiple_of(c * SSD_CHUNK, SSD_CHUNK)
        rows = pl.ds(r, SSD_CHUNK)
        y = y + y_scr[rows, :] + dsk * xs_scr[rows, :]
        y = y * _silu(z_ref[0, rows, :].astype(F32))
        y = y * lax.rsqrt(jnp.mean(y * y, axis=-1, keepdims=True) + EPS) * nw
        o_ref[0, rows, :] = y.astype(o_ref.dtype)

    sf_scr[...] = jnp.zeros((NS, W), F32)
    sb_scr[...] = jnp.zeros((NS, W), F32)
    half = N_CHUNKS // 2

    def first_half(i, carry):
        cb = N_CHUNKS - 1 - i
        y_scr[pl.ds(pl.multiple_of(i * SSD_CHUNK, SSD_CHUNK), SSD_CHUNK), :] = forward_part(i)
        y_scr[pl.ds(pl.multiple_of(cb * SSD_CHUNK, SSD_CHUNK), SSD_CHUNK), :] = backward_part(cb)
        return carry

    def second_half(i, carry):
        cb = N_CHUNKS - 1 - i
        finalize(i, forward_part(i))
        finalize(cb, backward_part(cb))
        return carry

    lax.fori_loop(0, half, first_half, 0, unroll=True)
    lax.fori_loop(half, N_CHUNKS, second_half, 0, unroll=True)


def _ssd(u3, dt_t, conv_w, conv_b, head_params, dsk_row, norm_w):
    b = u3.shape[0]
    L = SEQ
    W = GROUP_WIDTH
    NS = SSD_STATE
    G = SSD_GROUPS
    xblk = COL_X // W
    bblk = COL_B // NS
    cblk = COL_C // NS
    in_specs = [
        pl.BlockSpec((1, L, W), lambda i, g: (i, 0, g)),
        pl.BlockSpec((1, L, W), lambda i, g: (i, 0, xblk + g)),
        pl.BlockSpec((1, L, NS), lambda i, g: (i, 0, bblk + g)),
        pl.BlockSpec((1, L, NS), lambda i, g: (i, 0, cblk + g)),
        pl.BlockSpec((1, GROUP_HEADS, L), lambda i, g: (i, g, 0)),
        pl.BlockSpec((1, GROUP_HEADS, L), lambda i, g: (i, G + g, 0)),
        pl.BlockSpec((SSD_CONV, W), lambda i, g: (0, g)),
        pl.BlockSpec((SSD_CONV, NS), lambda i, g: (0, SSD_WIDTH // NS + g)),
        pl.BlockSpec((SSD_CONV, NS), lambda i, g: (0, SSD_WIDTH // NS + G + g)),
        pl.BlockSpec((1, W), lambda i, g: (0, g)),
        pl.BlockSpec((1, NS), lambda i, g: (0, SSD_WIDTH // NS + g)),
        pl.BlockSpec((1, NS), lambda i, g: (0, SSD_WIDTH // NS + G + g)),
        pl.BlockSpec((1, GROUP_HEADS, 8), lambda i, g: (g, 0, 0)),
        pl.BlockSpec((1, 1, W), lambda i, g: (g, 0, 0)),
        pl.BlockSpec((1, 1, W), lambda i, g: (g, 0, 0)),
    ]
    return pl.pallas_call(
        _ssd_kernel,
        grid=(b, G),
        in_specs=in_specs,
        out_specs=pl.BlockSpec((1, L, W), lambda i, g: (i, 0, g)),
        out_shape=jax.ShapeDtypeStruct((b, L, SSD_WIDTH), BF16),
        scratch_shapes=[
            pltpu.VMEM((3, (SSD_CONV - 1) * SSD_CHUNK, CONV_WIN), BF16),
            pltpu.VMEM((L, W), F32),
            pltpu.VMEM((NS, L), BF16),
            pltpu.VMEM((L, NS), BF16),
            pltpu.VMEM((L, W), F32),
            pltpu.VMEM((LANES, L), F32),
            pltpu.VMEM((L, LANES), F32),
            pltpu.VMEM((N_CHUNKS, 2 * W), F32),
            pltpu.VMEM((NS, W), F32),
            pltpu.VMEM((NS, W), F32),
        ],
        compiler_params=pltpu.CompilerParams(
            dimension_semantics=("arbitrary", "arbitrary"), vmem_limit_bytes=VMEM_LIMIT),
        name="ssd",
    )(u3, u3, u3, u3, dt_t, dt_t, conv_w, conv_w, conv_w, conv_b, conv_b, conv_b,
      head_params, dsk_row, norm_w)


def _group_mean_sq(v, seg):
    return _dot((v * v).astype(BF16), seg) * (1.0 / 64.0)


def _seg_matrix():
    r = lax.broadcasted_iota(jnp.int32, (LANES, LANES), 0) // 64
    c = lax.broadcasted_iota(jnp.int32, (LANES, LANES), 1) // 64
    return jnp.where(r == c, 1.0, 0.0).astype(BF16)


DIFF_TQ = 256
DIFF_HEADS_PER_STEP = 2


def _diff_kernel(lam_init, q_ref, k_ref, v_ref, g_ref, qc_ref, qs_ref, kc_ref, ks_ref, lam_ref,
                 sub_ref, o_ref, q_scr, k_scr, vt_scr, sa_scr, sb_scr):
    L = SEQ
    seg = _seg_matrix()
    comp0 = lax.broadcasted_iota(jnp.int32, (DIFF_TQ, LANES), 1) < 64
    p_row = lax.broadcasted_iota(jnp.int32, (LANES, LANES), 0)
    p_col = lax.broadcasted_iota(jnp.int32, (LANES, LANES), 1)
    swap = jnp.where((p_row ^ 32) == p_col, 1.0, 0.0).astype(BF16)

    def norm_rope(ref, r, lanes, cw_ref, sw_ref):
        vb = ref[0, pl.ds(r, DIFF_TQ), lanes]
        v = vb.astype(F32)
        rinv = lax.rsqrt(_group_mean_sq(v, seg) + EPS)
        return rinv * (v * cw_ref[pl.ds(r, DIFF_TQ), :] + _dot(vb, swap) * sw_ref[pl.ds(r, DIFF_TQ), :])

    n_blk = L // DIFF_TQ
    n_items = DIFF_HEADS_PER_STEP * n_blk

    def item(j):
        hd = j // n_blk
        r = pl.multiple_of((j % n_blk) * DIFF_TQ, DIFF_TQ)
        return hd, r, pl.ds(pl.multiple_of(hd * LANES, LANES), LANES)

    for hd in range(DIFF_HEADS_PER_STEP):
        vt_scr[hd, LANES:LANES + ONES_ROWS, :] = jnp.ones((ONES_ROWS, L), BF16)

    def prep(j, carry):
        hd, r, lanes = item(j)
        q_scr[hd, pl.ds(r, DIFF_TQ), :] = norm_rope(q_ref, r, lanes, qc_ref, qs_ref).astype(BF16)
        kn = norm_rope(k_ref, r, lanes, kc_ref, ks_ref)
        k_scr[hd, 0, pl.ds(r, DIFF_TQ), :] = jnp.where(comp0, kn, 0.0).astype(BF16)
        k_scr[hd, 1, pl.ds(r, DIFF_TQ), :] = jnp.where(comp0, 0.0, kn).astype(BF16)
        vt_scr[hd, 0:LANES, pl.ds(r, DIFF_TQ)] = v_ref[0, pl.ds(r, DIFF_TQ), lanes].astype(F32).T.astype(BF16)
        return carry

    lax.fori_loop(0, n_items, prep, 0, unroll=True)

    lam = lam_ref[...]
    lam_full = (jnp.exp(jnp.sum(lam[0:1] * lam[1:2], axis=-1, keepdims=True))
                - jnp.exp(jnp.sum(lam[2:3] * lam[3:4], axis=-1, keepdims=True)) + lam_init)

    def scores(j, s_ref):
        hd, r, _ = item(j)
        qb = q_scr[hd, pl.ds(r, DIFF_TQ), :]
        for c in range(2):
            s_ref[c] = _dot_nt(k_scr[hd, c], qb)

    def finish(j, s_ref):
        hd, r, lanes = item(j)
        parts = []
        for c in range(2):
            s = s_ref[c]
            e = jnp.exp2(s - jnp.max(s, axis=0, keepdims=True)).astype(BF16)
            acc = _dot(vt_scr[hd], e)
            parts.append(acc[0:LANES, :] * (1.0 / acc[LANES:LANES + 1, :]))
        ot = parts[0] - lam_full * parts[1]
        o = ot.T
        o = o * lax.rsqrt(jnp.mean(o * o, axis=-1, keepdims=True) + EPS) * sub_ref[...]
        o = o * (1.0 - lam_init) * _silu(g_ref[0, pl.ds(r, DIFF_TQ), lanes].astype(F32))
        o_ref[0, pl.ds(r, DIFF_TQ), lanes] = o.astype(o_ref.dtype)

    scores(0, sa_scr)

    def pair(i, carry):
        scores(2 * i + 1, sb_scr)
        finish(2 * i, sa_scr)
        scores(2 * i + 2, sa_scr)
        finish(2 * i + 1, sb_scr)
        return carry

    lax.fori_loop(0, n_items // 2 - 1, pair, 0, unroll=2)
    scores(n_items - 1, sb_scr)
    finish(n_items - 2, sa_scr)
    finish(n_items - 1, sb_scr)


def _diff(u3, rope_tabs, lam, subln_w, lam_init):
    b = u3.shape[0]
    L = SEQ
    step_w = DIFF_HEADS_PER_STEP * LANES
    base = COL_DIFF // step_w
    nb = DIFF_WIDTH // step_w

    def spec(part):
        return pl.BlockSpec((1, L, step_w), lambda i, h: (i, 0, base + part * nb + h))

    return pl.pallas_call(
        functools.partial(_diff_kernel, lam_init),
        grid=(b, DIFF_HEADS // DIFF_HEADS_PER_STEP),
        in_specs=[
            spec(0), spec(1), spec(2), spec(3),
            pl.BlockSpec((L, LANES), lambda i, h: (0, 0)),
            pl.BlockSpec((L, LANES), lambda i, h: (0, 0)),
            pl.BlockSpec((L, LANES), lambda i, h: (0, 0)),
            pl.BlockSpec((L, LANES), lambda i, h: (0, 0)),
            pl.BlockSpec((4, DIFF_HEAD_DIM), lambda i, h: (0, 0)),
            pl.BlockSpec((1, LANES), lambda i, h: (0, 0)),
        ],
        out_specs=pl.BlockSpec((1, L, step_w), lambda i, h: (i, 0, h)),
        out_shape=jax.ShapeDtypeStruct((b, L, DIFF_WIDTH), BF16),
        scratch_shapes=[
            pltpu.VMEM((DIFF_HEADS_PER_STEP, L, LANES), BF16),
            pltpu.VMEM((DIFF_HEADS_PER_STEP, 2, L, LANES), BF16),
            pltpu.VMEM((DIFF_HEADS_PER_STEP, LANES + ONES_ROWS, L), BF16),
            pltpu.VMEM((2, L, DIFF_TQ), F32),
            pltpu.VMEM((2, L, DIFF_TQ), F32),
        ],
        compiler_params=pltpu.CompilerParams(
            dimension_semantics=("arbitrary", "arbitrary"), vmem_limit_bytes=VMEM_LIMIT),
        name="diff_attn",
    )(u3, u3, u3, u3, *rope_tabs, lam, subln_w)


def _na_block_geometry(qb):
    r0 = qb * NA_QROWS
    kb = min(max(r0 - NA_KH // 2, 0), ROWS - NA_BAND)
    return r0, kb


def _na_bias_type(qb):
    return 0 if qb == 0 else (2 if qb == NA_NBLK - 1 else 1)


def _na_kernel(q_ref, k_ref, v_ref, g_ref, qkw_ref, bias_ref, o_ref, q_scr, k_scr, vt_scr, s_scr):
    L = SEQ
    seg = _seg_matrix()
    head0_blk = lax.broadcasted_iota(jnp.int32, (NA_QBLK, LANES), 1) < 64
    q_scale = (NA_HEAD_DIM ** -0.5) * LOG2E
    for h in range(2):
        vt_scr[h, NA_HEAD_DIM:NA_HEAD_DIM + ONES_ROWS, :] = jnp.ones((ONES_ROWS, L), BF16)

    def norm(ref, r, w_row):
        v = ref[0, pl.ds(r, NA_QBLK), :].astype(F32)
        return v * lax.rsqrt(_group_mean_sq(v, seg) + EPS) * w_row

    def prep(i, carry):
        r = pl.multiple_of(i * NA_QBLK, NA_QBLK)
        q_scr[pl.ds(r, NA_QBLK), :] = (norm(q_ref, r, qkw_ref[0:1, :]) * q_scale).astype(BF16)
        kn = norm(k_ref, r, qkw_ref[1:2, :])
        k_scr[0, pl.ds(r, NA_QBLK), :] = jnp.where(head0_blk, kn, 0.0).astype(BF16)
        k_scr[1, pl.ds(r, NA_QBLK), :] = jnp.where(head0_blk, 0.0, kn).astype(BF16)
        vt = v_ref[0, pl.ds(r, NA_QBLK), :].astype(F32).T.astype(BF16)
        for h in range(2):
            vt_scr[h, 0:NA_HEAD_DIM, pl.ds(r, NA_QBLK)] = vt[h * NA_HEAD_DIM:(h + 1) * NA_HEAD_DIM, :]
        return carry

    lax.fori_loop(0, L // NA_QBLK, prep, 0, unroll=True)

    def slices(qb):
        r0, kb = _na_block_geometry(qb)
        return (slice(r0 * GRID_W, r0 * GRID_W + NA_QBLK), slice(kb * GRID_W, kb * GRID_W + NA_KBLK))

    def scores(qb):
        qs, ks = slices(qb)
        q_blk = q_scr[qs, :]
        for h in range(2):
            s_scr[qb % 2, h] = _dot_nt(k_scr[h, ks, :], q_blk)

    def finish(qb):
        qs, ks = slices(qb)
        t = _na_bias_type(qb)
        outs = []
        for h in range(2):
            s = s_scr[qb % 2, h] + bias_ref[h, t]
            e = jnp.exp2(s - jnp.max(s, axis=0, keepdims=True)).astype(BF16)
            acc = _dot(vt_scr[h, :, ks], e)
            outs.append(acc[0:NA_HEAD_DIM, :] * (1.0 / acc[NA_HEAD_DIM:NA_HEAD_DIM + 1, :]))
        o = jnp.concatenate(outs, axis=0).T
        o = o * _silu(g_ref[0, qs, :].astype(F32))
        o_ref[0, qs, :] = o.astype(o_ref.dtype)

    scores(0)
    for qb in range(NA_NBLK):
        if qb + 1 < NA_NBLK:
            scores(qb + 1)
        finish(qb)


def _na(u3, qk_w, bias, layer):
    b = u3.shape[0]
    L = SEQ
    base = COL_NA // LANES
    nb = NA_WIDTH // LANES

    def spec(part):
        return pl.BlockSpec((1, L, LANES), lambda hp, i: (i, 0, base + part * nb + hp))

    return pl.pallas_call(
        _na_kernel,
        grid=(NA_HEADS // 2, b),
        in_specs=[
            spec(0), spec(1), spec(2), spec(3),
            pl.BlockSpec((2, LANES), lambda hp, i: (0, 0)),
            pl.BlockSpec((None, 2, 3, NA_KBLK, NA_QBLK), lambda hp, i: (layer, hp, 0, 0, 0)),
        ],
        out_specs=pl.BlockSpec((1, L, LANES), lambda hp, i: (i, 0, hp)),
        out_shape=jax.ShapeDtypeStruct((b, L, NA_WIDTH), BF16),
        scratch_shapes=[
            pltpu.VMEM((L, LANES), BF16),
            pltpu.VMEM((2, L, LANES), BF16),
            pltpu.VMEM((2, NA_HEAD_DIM + ONES_ROWS, L), BF16),
            pltpu.VMEM((2, 2, NA_KBLK, NA_QBLK), F32),
        ],
        compiler_params=pltpu.CompilerParams(
            dimension_semantics=("arbitrary", "arbitrary"), vmem_limit_bytes=VMEM_LIMIT),
        name="na_attn",
    )(u3, u3, u3, u3, qk_w, bias)


def _na_bias(rpb):
    lead = rpb.shape[:-2]
    n_dr = 2 * NA_KH - 1
    pad = GRID_W - NA_KW
    rev = jnp.pad(rpb.astype(F32)[..., ::-1] * LOG2E, [(0, 0)] * (rpb.ndim - 1) + [(pad, pad + 1)])
    rev = rev.reshape((-1, n_dr, LANES))
    out = pl.pallas_call(
        _na_bias_kernel,
        grid=(rev.shape[0],),
        in_specs=[pl.BlockSpec((1, n_dr, LANES), lambda n: (n, 0, 0))],
        out_specs=pl.BlockSpec((1, 3, NA_KBLK, NA_QBLK), lambda n: (n, 0, 0, 0)),
        out_shape=jax.ShapeDtypeStruct((rev.shape[0], 3, NA_KBLK, NA_QBLK), F32),
        scratch_shapes=[pltpu.VMEM((n_dr, GRID_W, LANES), F32)],
        compiler_params=pltpu.CompilerParams(
            dimension_semantics=("arbitrary",), vmem_limit_bytes=VMEM_LIMIT),
        name="na_bias",
    )(rev)
    return out.reshape(lead + (3, NA_KBLK, NA_QBLK))


def _na_bias_kernel(r_ref, o_ref, t_scr):
    k_i = lax.broadcasted_iota(jnp.int32, (GRID_W, LANES), 0)
    lane = lax.broadcasted_iota(jnp.int32, (GRID_W, LANES), 1)
    left = lane < GRID_W
    win_start = jnp.clip(lane % GRID_W - NA_KW // 2, 0, GRID_W - NA_KW)
    col_ok = (k_i >= win_start) & (k_i < win_start + NA_KW)
    for d in range(2 * NA_KH - 1):
        rows = jnp.broadcast_to(r_ref[0, d:d + 1, :], (GRID_W, LANES))
        skew = pltpu.roll(rows, GRID_W + 1, axis=1, stride=1, stride_axis=0)
        both = jnp.where(left, skew, pltpu.roll(skew, GRID_W, axis=1))
        t_scr[d] = jnp.where(col_ok, both, NEG)

    masked = jnp.full((GRID_W, LANES), NEG, F32)

    def tile(qb, j, i):
        r0, kb = _na_block_geometry(qb)
        r, rk = r0 + i, kb + j
        rs = min(max(r - NA_KH // 2, 0), ROWS - NA_KH)
        return t_scr[rk - r + NA_KH - 1] if rs <= rk < rs + NA_KH else masked

    for t, qb in enumerate((0, 1, NA_NBLK - 1)):
        for j in range(NA_BAND):
            for p in range(NA_QROWS // 2):
                o_ref[0, t, j * GRID_W:(j + 1) * GRID_W, p * LANES:(p + 1) * LANES] = jnp.where(
                    left, tile(qb, j, 2 * p), tile(qb, j, 2 * p + 1))


OUT_TM = 1024


def _outproj_kernel(x_ref, ys_ref, yd_ref, yn_ref, w_ref, o_ref):
    acc = _dot(ys_ref[...], w_ref[0:SSD_WIDTH, :])
    acc = acc + _dot(yd_ref[...], w_ref[SSD_WIDTH:SSD_WIDTH + DIFF_WIDTH, :])
    acc = acc + _dot(yn_ref[...], w_ref[SSD_WIDTH + DIFF_WIDTH:MIX_WIDTH, :])
    o_ref[...] = x_ref[...] + acc


def _outproj(x2, y_ssd, y_diff, y_na, w_out):
    m = x2.shape[0]
    return pl.pallas_call(
        _outproj_kernel,
        grid=(m // OUT_TM,),
        in_specs=[
            pl.BlockSpec((OUT_TM, D_MODEL), lambda i: (i, 0)),
            pl.BlockSpec((OUT_TM, SSD_WIDTH), lambda i: (i, 0)),
            pl.BlockSpec((OUT_TM, DIFF_WIDTH), lambda i: (i, 0)),
            pl.BlockSpec((OUT_TM, NA_WIDTH), lambda i: (i, 0)),
            pl.BlockSpec((MIX_WIDTH, D_MODEL), lambda i: (0, 0)),
        ],
        out_specs=pl.BlockSpec((OUT_TM, D_MODEL), lambda i: (i, 0)),
        out_shape=jax.ShapeDtypeStruct((m, D_MODEL), F32),
        compiler_params=pltpu.CompilerParams(
            dimension_semantics=("arbitrary",), vmem_limit_bytes=VMEM_LIMIT),
        name="outproj",
    )(x2, y_ssd, y_diff, y_na, w_out)


def _rope_tables():
    inv_freq = ROPE_THETA ** (-jnp.arange(0, DIFF_HEAD_DIM, 2, dtype=F32) / DIFF_HEAD_DIM)
    ang = jnp.arange(SEQ, dtype=F32)[:, None] * inv_freq[None, :]
    cos, sin = jnp.cos(ang), jnp.sin(ang)
    cos_t = jnp.concatenate([cos, cos, cos, cos], axis=1)
    sin_t = jnp.concatenate([-sin, sin, -sin, sin], axis=1)
    return cos_t, sin_t


def kernel(x, norm_w, w_in, conv_w, conv_b, a_log, dt_bias, d_skip, ssd_norm_w, diff_qk_norm,
           diff_lambda, diff_subln, na_qk_norm, na_rpb, w_out):
    b, L, d = x.shape
    assert (L, d) == (SEQ, D_MODEL)
    depth = w_in.shape[0]
    cos_t, sin_t = _rope_tables()
    na_bias = _na_bias(na_rpb)
    dt_lo = SSD_WIDTH + SSD_XBC
    dt_hi = dt_lo + 2 * SSD_HEADS
    w_t = jnp.swapaxes(w_in, 1, 2)
    w_main = jnp.concatenate([w_t[:, :dt_lo], w_t[:, dt_hi:]], axis=1).astype(BF16)
    w_dt = jnp.pad(w_t[:, dt_lo:dt_hi], ((0, 0), (0, DT_PAD - 2 * SSD_HEADS), (0, 0))).astype(BF16)
    x2 = x.reshape(b * L, d)
    for i in range(depth):
        lam_init = 0.8 - 0.6 * math.exp(-0.3 * i)
        u, dt = _inproj(x2, norm_w[i][None, :], w_main, w_dt, i)
        u3 = u.reshape(b, L, U_MAIN)
        dt_t = jnp.swapaxes(dt.reshape(b, L, DT_PAD)[:, :, :2 * SSD_HEADS], 1, 2)

        hp = jnp.stack([a_log[i][0], a_log[i][1], dt_bias[i][0], dt_bias[i][1]], axis=-1)
        hp = jnp.pad(hp, ((0, 0), (0, 4))).reshape(SSD_GROUPS, GROUP_HEADS, 8)
        dsk_row = jnp.repeat(d_skip[i][0] + d_skip[i][1], SSD_HEAD_DIM).reshape(SSD_GROUPS, 1, GROUP_WIDTH)
        y_ssd = _ssd(u3, dt_t, conv_w[i], conv_b[i][None, :], hp, dsk_row,
                     ssd_norm_w[i].reshape(SSD_GROUPS, 1, GROUP_WIDTH))

        half = DIFF_HEAD_DIM // 2
        qk_w = jnp.tile(diff_qk_norm[i], (1, 2))
        qk_w_sw = jnp.tile(jnp.concatenate([diff_qk_norm[i][:, half:], diff_qk_norm[i][:, :half]], axis=1), (1, 2))
        q_scale = (DIFF_HEAD_DIM ** -0.5) * LOG2E
        rope_tabs = (cos_t * (qk_w[0:1] * q_scale), sin_t * (qk_w_sw[0:1] * q_scale),
                     cos_t * qk_w[1:2], sin_t * qk_w_sw[1:2])
        y_diff = _diff(u3, rope_tabs, diff_lambda[i], diff_subln[i][None, :], lam_init)

        na_w = jnp.concatenate([na_qk_norm[i], na_qk_norm[i]], axis=1)
        y_na = _na(u3, na_w, na_bias, i)

        x2 = _outproj(x2, y_ssd.reshape(b * L, SSD_WIDTH), y_diff.reshape(b * L, DIFF_WIDTH),
                      y_na.reshape(b * L, NA_WIDTH), w_out[i].astype(BF16))
    return x2.reshape(b, L, d)
```

```python
import functools
import math

import numpy as np
import jax
import jax.numpy as jnp
from jax import lax
from jax.experimental import pallas as pl
from jax.experimental.pallas import tpu as pltpu

F32 = jnp.float32
BF16 = jnp.bfloat16

D_MODEL = 1024
SEQ = 2048
GRID_W = 64
ROWS = SEQ // GRID_W
SSD_WIDTH = 1024
SSD_HEAD_DIM = 64
SSD_HEADS = 16
SSD_GROUPS = 2
SSD_STATE = 128
SSD_CONV = 5
SSD_CHUNK = 128
SSD_XBC = SSD_WIDTH + 2 * SSD_GROUPS * SSD_STATE
GROUP_HEADS = SSD_HEADS // SSD_GROUPS
GROUP_WIDTH = SSD_WIDTH // SSD_GROUPS
N_CHUNKS = SEQ // SSD_CHUNK
DIFF_WIDTH = 512
DIFF_HEAD_DIM = 64
DIFF_HEADS = 4
NA_WIDTH = 512
NA_HEAD_DIM = 64
NA_HEADS = 8
NA_KH = 8
NA_KW = 16
MIX_WIDTH = SSD_WIDTH + DIFF_WIDTH + NA_WIDTH
ROPE_THETA = 10000.0
EPS = 1e-6
LANES = 128
CONV_HALO = 16
CONV_WIN = SSD_CHUNK + 2 * CONV_HALO

U_MAIN = SSD_WIDTH + SSD_XBC + 4 * DIFF_WIDTH + 4 * NA_WIDTH
COL_Z = 0
COL_X = SSD_WIDTH
COL_B = COL_X + SSD_WIDTH
COL_C = COL_B + SSD_GROUPS * SSD_STATE
COL_DIFF = COL_C + SSD_GROUPS * SSD_STATE
COL_NA = COL_DIFF + 4 * DIFF_WIDTH
DT_PAD = LANES

NA_QROWS = 4
NA_BAND = 12
NA_QBLK = NA_QROWS * GRID_W
NA_KBLK = NA_BAND * GRID_W
NA_NBLK = ROWS // NA_QROWS
NA_AHEAD = 2
NEG = -1e30
LOG2E = math.log2(math.e)
ONES_ROWS = 16

VMEM_LIMIT = 56 * 1024 * 1024


def _silu(v):
    h = 0.5 * v
    return h * jnp.tanh(h) + h


def _dot(a, b):
    return jnp.dot(a, b, preferred_element_type=F32)


def _dot_nt(a, b):
    return lax.dot_general(a, b, (((1,), (1,)), ((), ())), preferred_element_type=F32)


def _dot_tn(a, b):
    return lax.dot_general(a, b, (((0,), (0,)), ((), ())), preferred_element_type=F32)


IN_TM = 1024
IN_TN = 3328


def _inproj_kernel(x_ref, nw_ref, w_ref, wdt_ref, u_ref, dt_ref, h_scr):
    @pl.when(pl.program_id(1) == 0)
    def _():
        x = x_ref[...]
        ms = jnp.mean(x * x, axis=-1, keepdims=True)
        h = (x * lax.rsqrt(ms + EPS) * nw_ref[...]).astype(BF16)
        h_scr[...] = h
        dt_ref[...] = _dot_nt(h, wdt_ref[...])

    u_ref[...] = _dot_nt(h_scr[...], w_ref[...]).astype(u_ref.dtype)


def _inproj(x2, norm_w, w_main, w_dt, layer):
    m = x2.shape[0]
    return pl.pallas_call(
        _inproj_kernel,
        grid=(m // IN_TM, U_MAIN // IN_TN),
        in_specs=[
            pl.BlockSpec((IN_TM, D_MODEL), lambda i, j: (i, 0)),
            pl.BlockSpec((1, D_MODEL), lambda i, j: (0, 0)),
            pl.BlockSpec((None, IN_TN, D_MODEL), lambda i, j: (layer, j, 0)),
            pl.BlockSpec((None, DT_PAD, D_MODEL), lambda i, j: (layer, 0, 0)),
        ],
        out_specs=[
            pl.BlockSpec((IN_TM, IN_TN), lambda i, j: (i, j)),
            pl.BlockSpec((IN_TM, DT_PAD), lambda i, j: (i, 0)),
        ],
        out_shape=[
            jax.ShapeDtypeStruct((m, U_MAIN), BF16),
            jax.ShapeDtypeStruct((m, DT_PAD), F32),
        ],
        scratch_shapes=[pltpu.VMEM((IN_TM, D_MODEL), BF16)],
        compiler_params=pltpu.CompilerParams(
            dimension_semantics=("arbitrary", "arbitrary"), vmem_limit_bytes=VMEM_LIMIT),
        name="inproj",
    )(x2, norm_w, w_main, w_dt)


R_WF, R_DF = 0, 1
R_WB, R_DB = 4, 5
R_COLF, R_COLB = 8, 9
R_EF, R_EB = 10, 11
R_ROWF, R_ROWB = 14, 15


def _softplus(v):
    return jnp.maximum(v, 0.0) + jnp.log1p(jnp.exp(-jnp.abs(v)))


def _chunk_scan(a, lane, reverse):
    n = a.shape[-1]
    out = a
    k = 1
    while k < SSD_CHUNK:
        if reverse:
            out = out + jnp.where(lane < SSD_CHUNK - k, pltpu.roll(out, n - k, axis=1), 0.0)
        else:
            out = out + jnp.where(lane >= k, pltpu.roll(out, k, axis=1), 0.0)
        k *= 2
    return out


def _ssd_kernel(z_ref, x_ref, b_ref, c_ref, dtf_ref, dtb_ref, cwx_ref, cwb_ref, cwc_ref,
                cbx_ref, cbb_ref, cbc_ref, hp_ref, dsk_ref, nw_ref, o_ref,
                shift_scr, xs_scr, bt_scr, cm_scr, y_scr, hm_scr, tm_scr, dec_scr, sf_scr, sb_scr):
    L = SEQ
    W = GROUP_WIDTH
    NS = SSD_STATE

    side = (SSD_CONV - 1) * SSD_CHUNK
    tap_t = lax.broadcasted_iota(jnp.int32, (side, CONV_WIN), 0)
    tap_j = lax.broadcasted_iota(jnp.int32, (side, CONV_WIN), 1)
    tap_k = tap_t // SSD_CHUNK
    tap_src = tap_t % SSD_CHUNK + jnp.where(tap_k >= SSD_CONV // 2, tap_k + 1, tap_k) - SSD_CONV // 2
    for variant in range(3):
        shift_scr[variant] = jnp.where(tap_j == tap_src + variant * CONV_HALO, 1.0, 0.0).astype(BF16)
    side_taps = [k for k in range(SSD_CONV) if k != SSD_CONV // 2]

    def conv(c, carry):
        r = pl.multiple_of(c * SSD_CHUNK, SSD_CHUNK)
        w0 = pl.multiple_of(jnp.clip(r - CONV_HALO, 0, L - CONV_WIN), CONV_HALO)
        variant = jnp.where(c == 0, 0, jnp.where(c == N_CHUNKS - 1, 2, 1))
        shift = shift_scr[variant]
        rows = pl.ds(r, SSD_CHUNK)

        def taps(win, centre, cw, cb):
            sh = _dot(shift, win)
            acc = cb + centre.astype(F32) * cw[SSD_CONV // 2:SSD_CONV // 2 + 1, :]
            for slot, k in enumerate(side_taps):
                acc = acc + sh[slot * SSD_CHUNK:(slot + 1) * SSD_CHUNK, :] * cw[k:k + 1, :]
            return _silu(acc)

        xs_scr[rows, :] = taps(x_ref[0, pl.ds(w0, CONV_WIN), :], x_ref[0, rows, :], cwx_ref[...], cbx_ref[...])
        win_bc = jnp.concatenate([b_ref[0, pl.ds(w0, CONV_WIN), :], c_ref[0, pl.ds(w0, CONV_WIN), :]], axis=1)
        mid_bc = jnp.concatenate([b_ref[0, rows, :], c_ref[0, rows, :]], axis=1)
        act_bc = taps(win_bc, mid_bc, jnp.concatenate([cwb_ref[...], cwc_ref[...]], axis=1),
                      jnp.concatenate([cbb_ref[...], cbc_ref[...]], axis=1))
        bt_scr[:, rows] = act_bc[:, 0:NS].T.astype(BF16)
        cm_scr[rows, :] = act_bc[:, NS:2 * NS].astype(BF16)
        return carry

    lax.fori_loop(0, N_CHUNKS, conv, 0, unroll=8)

    hp = hp_ref[0]
    a_f = -jnp.exp(hp[:, 0:1])
    a_b = -jnp.exp(hp[:, 1:2])
    dt_f = _softplus(dtf_ref[0] + hp[:, 2:3])
    dt_b = _softplus(dtb_ref[0] + hp[:, 3:4])
    da_f = dt_f * a_f
    da_b = dt_b * a_b
    lane = lax.broadcasted_iota(jnp.int32, (GROUP_HEADS, L), 1) % SSD_CHUNK
    cs_f = _chunk_scan(da_f, lane, False)
    rs_f = _chunk_scan(da_f, lane, True) - da_f
    cs_b = _chunk_scan(da_b, lane, False)
    ecs_b = cs_b - da_b
    rs_b = _chunk_scan(da_b, lane, True)

    def put(row, v):
        hm_scr[row * GROUP_HEADS:(row + 1) * GROUP_HEADS, :] = v

    def put_split(row_hi, v):
        hi = v.astype(BF16).astype(F32)
        put(row_hi, hi)
        put(row_hi + 2, (v - hi).astype(BF16).astype(F32))

    put_split(R_WF, jnp.exp(rs_f) * dt_f)
    put_split(R_DF, jnp.exp(cs_f))
    put_split(R_WB, jnp.exp(ecs_b) * dt_b)
    put_split(R_DB, jnp.exp(rs_b))
    put_split(R_EF, jnp.exp(cs_f + rs_f))
    put_split(R_EB, jnp.exp(ecs_b + rs_b))
    put(R_COLF, cs_f * LOG2E)
    put(R_COLB, ecs_b * LOG2E)
    put(R_ROWF, cs_f * LOG2E - jnp.log2(dt_f))
    put(R_ROWB, ecs_b * LOG2E + jnp.log2(dt_b))

    def to_time_major(c, carry):
        r = pl.multiple_of(c * SSD_CHUNK, SSD_CHUNK)
        tm_scr[pl.ds(r, SSD_CHUNK), :] = hm_scr[:, pl.ds(r, SSD_CHUNK)].T
        return carry

    lax.fori_loop(0, N_CHUNKS, to_time_major, 0, unroll=True)

    e_row = lax.broadcasted_iota(jnp.int32, (4 * GROUP_HEADS, 2 * W), 0) % (2 * GROUP_HEADS)
    e_col = lax.broadcasted_iota(jnp.int32, (4 * GROUP_HEADS, 2 * W), 1) // SSD_HEAD_DIM
    spread = jnp.where(e_row == e_col, 1.0, 0.0).astype(BF16)

    def expand(rows, first_row):
        cols = slice(first_row * GROUP_HEADS, (first_row + 4) * GROUP_HEADS)
        return _dot(tm_scr[rows, :][:, cols].astype(BF16), spread)

    dec_scr[...] = expand(pl.ds(0, N_CHUNKS, stride=SSD_CHUNK), R_EF)

    row_i = lax.broadcasted_iota(jnp.int32, (SSD_CHUNK, SSD_CHUNK), 0)
    col_i = lax.broadcasted_iota(jnp.int32, (SSD_CHUNK, SSD_CHUNK), 1)
    lower = col_i <= row_i
    upper = col_i >= row_i
    left = lax.broadcasted_iota(jnp.int32, (SSD_CHUNK, LANES), 1) < SSD_HEAD_DIM
    dsk = dsk_ref[0]
    nw = nw_ref[0]

    def hm_row(row, h, r):
        return hm_scr[row * GROUP_HEADS + h:row * GROUP_HEADS + h + 1, pl.ds(r, SSD_CHUNK)]

    def forward_part(c):
        r = pl.multiple_of(c * SSD_CHUNK, SSD_CHUNK)
        rows = pl.ds(r, SSD_CHUNK)
        xs_c = xs_scr[rows, :]
        bt_c = bt_scr[:, rows]
        cm_c = cm_scr[rows, :]
        g = _dot(cm_c, bt_c)
        xs_b = xs_c.astype(BF16)
        colf = tm_scr[rows, R_COLF * GROUP_HEADS:(R_COLF + 1) * GROUP_HEADS]
        colb = tm_scr[rows, R_COLB * GROUP_HEADS:(R_COLB + 1) * GROUP_HEADS]
        y_pairs = []
        for hp_i in range(GROUP_HEADS // 2):
            ms = []
            for h in (2 * hp_i, 2 * hp_i + 1):
                seg_f = jnp.where(lower, colf[:, h:h + 1] - hm_row(R_ROWF, h, r), NEG)
                seg_b = jnp.where(upper, hm_row(R_ROWB, h, r) - colb[:, h:h + 1], NEG)
                ms.append((g * (jnp.exp2(seg_f) + jnp.exp2(seg_b))).astype(BF16))
            xp = xs_b[:, hp_i * LANES:(hp_i + 1) * LANES]
            zero = jnp.zeros_like(xp)
            rhs = jnp.concatenate([jnp.where(left, xp, zero), jnp.where(left, zero, xp)], axis=0)
            y_pairs.append(_dot(jnp.concatenate(ms, axis=1), rhs))
        y_diag = jnp.concatenate(y_pairs, axis=1)
        ex = expand(rows, R_WF)
        s_f = sf_scr[...]
        y_off = _dot(cm_c, s_f.astype(BF16)) * ex[:, W:2 * W]
        xw = (xs_c * ex[:, 0:W]).astype(BF16)
        sf_scr[...] = s_f * dec_scr[pl.ds(c, 1), 0:W] + _dot(bt_c, xw)
        return y_diag + y_off

    def backward_part(c):
        r = pl.multiple_of(c * SSD_CHUNK, SSD_CHUNK)
        rows = pl.ds(r, SSD_CHUNK)
        ex = expand(rows, R_WB)
        s_b = sb_scr[...]
        y_off = _dot(cm_scr[rows, :], s_b.astype(BF16)) * ex[:, W:2 * W]
        xw = (xs_scr[rows, :] * ex[:, 0:W]).astype(BF16)
        sb_scr[...] = s_b * dec_scr[pl.ds(c, 1), W:2 * W] + _dot(bt_scr[:, rows], xw)
        return y_off

    def finalize(c, y):
        r = pl.multiple_of(c * SSD_CHUNK, SSD_CHUNK)
        rows = pl.ds(r, SSD_CHUNK)
        y = y + y_scr[rows, :] + dsk * xs_scr[rows, :]
        y = y * _silu(z_ref[0, rows, :].astype(F32))
        y = y * lax.rsqrt(jnp.mean(y * y, axis=-1, keepdims=True) + EPS) * nw
        o_ref[0, rows, :] = y.astype(o_ref.dtype)

    sf_scr[...] = jnp.zeros((NS, W), F32)
    sb_scr[...] = jnp.zeros((NS, W), F32)
    half = N_CHUNKS // 2

    def first_half(i, carry):
        cb = N_CHUNKS - 1 - i
        y_scr[pl.ds(pl.multiple_of(i * SSD_CHUNK, SSD_CHUNK), SSD_CHUNK), :] = forward_part(i)
        y_scr[pl.ds(pl.multiple_of(cb * SSD_CHUNK, SSD_CHUNK), SSD_CHUNK), :] = backward_part(cb)
        return carry

    def second_half(i, carry):
        cb = N_CHUNKS - 1 - i
        finalize(i, forward_part(i))
        finalize(cb, backward_part(cb))
        return carry

    lax.fori_loop(0, half, first_half, 0, unroll=True)
    lax.fori_loop(half, N_CHUNKS, second_half, 0, unroll=True)


def _ssd(u3, dt_t, conv_w, conv_b, head_params, dsk_row, norm_w):
    b = u3.shape[0]
    L = SEQ
    W = GROUP_WIDTH
    NS = SSD_STATE
    G = SSD_GROUPS
    xblk = COL_X // W
    bblk = COL_B // NS
    cblk = COL_C // NS
    in_specs = [
        pl.BlockSpec((1, L, W), lambda i, g: (i, 0, g)),
        pl.BlockSpec((1, L, W), lambda i, g: (i, 0, xblk + g)),
        pl.BlockSpec((1, L, NS), lambda i, g: (i, 0, bblk + g)),
        pl.BlockSpec((1, L, NS), lambda i, g: (i, 0, cblk + g)),
        pl.BlockSpec((1, GROUP_HEADS, L), lambda i, g: (i, g, 0)),
        pl.BlockSpec((1, GROUP_HEADS, L), lambda i, g: (i, G + g, 0)),
        pl.BlockSpec((SSD_CONV, W), lambda i, g: (0, g)),
        pl.BlockSpec((SSD_CONV, NS), lambda i, g: (0, SSD_WIDTH // NS + g)),
        pl.BlockSpec((SSD_CONV, NS), lambda i, g: (0, SSD_WIDTH // NS + G + g)),
        pl.BlockSpec((1, W), lambda i, g: (0, g)),
        pl.BlockSpec((1, NS), lambda i, g: (0, SSD_WIDTH // NS + g)),
        pl.BlockSpec((1, NS), lambda i, g: (0, SSD_WIDTH // NS + G + g)),
        pl.BlockSpec((1, GROUP_HEADS, 8), lambda i, g: (g, 0, 0)),
        pl.BlockSpec((1, 1, W), lambda i, g: (g, 0, 0)),
        pl.BlockSpec((1, 1, W), lambda i, g: (g, 0, 0)),
    ]
    return pl.pallas_call(
        _ssd_kernel,
        grid=(b, G),
        in_specs=in_specs,
        out_specs=pl.BlockSpec((1, L, W), lambda i, g: (i, 0, g)),
        out_shape=jax.ShapeDtypeStruct((b, L, SSD_WIDTH), BF16),
        scratch_shapes=[
            pltpu.VMEM((3, (SSD_CONV - 1) * SSD_CHUNK, CONV_WIN), BF16),
            pltpu.VMEM((L, W), F32),
            pltpu.VMEM((NS, L), BF16),
            pltpu.VMEM((L, NS), BF16),
            pltpu.VMEM((L, W), F32),
            pltpu.VMEM((LANES, L), F32),
            pltpu.VMEM((L, LANES), F32),
            pltpu.VMEM((N_CHUNKS, 2 * W), F32),
            pltpu.VMEM((NS, W), F32),
            pltpu.VMEM((NS, W), F32),
        ],
        compiler_params=pltpu.CompilerParams(
            dimension_semantics=("arbitrary", "arbitrary"), vmem_limit_bytes=VMEM_LIMIT),
        name="ssd",
    )(u3, u3, u3, u3, dt_t, dt_t, conv_w, conv_w, conv_w, conv_b, conv_b, conv_b,
      head_params, dsk_row, norm_w)


def _group_mean_sq(v, seg):
    return _dot((v * v).astype(BF16), seg) * (1.0 / 64.0)


def _seg_matrix():
    r = lax.broadcasted_iota(jnp.int32, (LANES, LANES), 0) // 64
    c = lax.broadcasted_iota(jnp.int32, (LANES, LANES), 1) // 64
    return jnp.where(r == c, 1.0, 0.0).astype(BF16)


DIFF_TQ = 256
DIFF_HEADS_PER_STEP = 2


def _diff_kernel(lam_init, q_ref, k_ref, v_ref, g_ref, qc_ref, qs_ref, kc_ref, ks_ref, lam_ref,
                 sub_ref, o_ref, q_scr, k_scr, vt_scr, sa_scr, sb_scr):
    L = SEQ
    seg = _seg_matrix()
    comp0 = lax.broadcasted_iota(jnp.int32, (DIFF_TQ, LANES), 1) < 64
    p_row = lax.broadcasted_iota(jnp.int32, (LANES, LANES), 0)
    p_col = lax.broadcasted_iota(jnp.int32, (LANES, LANES), 1)
    swap = jnp.where((p_row ^ 32) == p_col, 1.0, 0.0).astype(BF16)

    def norm_rope(ref, r, lanes, cw_ref, sw_ref):
        vb = ref[0, pl.ds(r, DIFF_TQ), lanes]
        v = vb.astype(F32)
        rinv = lax.rsqrt(_group_mean_sq(v, seg) + EPS)
        return rinv * (v * cw_ref[pl.ds(r, DIFF_TQ), :] + _dot(vb, swap) * sw_ref[pl.ds(r, DIFF_TQ), :])

    n_blk = L // DIFF_TQ
    n_items = DIFF_HEADS_PER_STEP * n_blk

    def item(j):
        hd = j // n_blk
        r = pl.multiple_of((j % n_blk) * DIFF_TQ, DIFF_TQ)
        return hd, r, pl.ds(pl.multiple_of(hd * LANES, LANES), LANES)

    for hd in range(DIFF_HEADS_PER_STEP):
        vt_scr[hd, LANES:LANES + ONES_ROWS, :] = jnp.ones((ONES_ROWS, L), BF16)

    def prep(j, carry):
        hd, r, lanes = item(j)
        q_scr[hd, pl.ds(r, DIFF_TQ), :] = norm_rope(q_ref, r, lanes, qc_ref, qs_ref).astype(BF16)
        kn = norm_rope(k_ref, r, lanes, kc_ref, ks_ref)
        k_scr[hd, 0, pl.ds(r, DIFF_TQ), :] = jnp.where(comp0, kn, 0.0).astype(BF16)
        k_scr[hd, 1, pl.ds(r, DIFF_TQ), :] = jnp.where(comp0, 0.0, kn).astype(BF16)
        vt_scr[hd, 0:LANES, pl.ds(r, DIFF_TQ)] = v_ref[0, pl.ds(r, DIFF_TQ), lanes].T
        return carry

    lax.fori_loop(0, n_items, prep, 0, unroll=True)

    lam = lam_ref[...]
    lam_full = (jnp.exp(jnp.sum(lam[0:1] * lam[1:2], axis=-1, keepdims=True))
                - jnp.exp(jnp.sum(lam[2:3] * lam[3:4], axis=-1, keepdims=True)) + lam_init)

    def scores(j, s_ref):
        hd, r, _ = item(j)
        qb = q_scr[hd, pl.ds(r, DIFF_TQ), :]
        for c in range(2):
            s_ref[c] = _dot_nt(k_scr[hd, c], qb)

    def finish(j, s_ref):
        hd, r, lanes = item(j)
        parts = []
        for c in range(2):
            s = s_ref[c]
            e = jnp.exp2(s - jnp.max(s, axis=0, keepdims=True)).astype(BF16)
            acc = _dot(vt_scr[hd], e)
            parts.append(acc[0:LANES, :] * (1.0 / acc[LANES:LANES + 1, :]))
        ot = parts[0] - lam_full * parts[1]
        o = ot.T
        o = o * lax.rsqrt(jnp.mean(o * o, axis=-1, keepdims=True) + EPS) * sub_ref[...]
        o = o * (1.0 - lam_init) * _silu(g_ref[0, pl.ds(r, DIFF_TQ), lanes].astype(F32))
        o_ref[0, pl.ds(r, DIFF_TQ), lanes] = o.astype(o_ref.dtype)

    scores(0, sa_scr)

    def pair(i, carry):
        scores(2 * i + 1, sb_scr)
        finish(2 * i, sa_scr)
        scores(2 * i + 2, sa_scr)
        finish(2 * i + 1, sb_scr)
        return carry

    lax.fori_loop(0, n_items // 2 - 1, pair, 0, unroll=2)
    scores(n_items - 1, sb_scr)
    finish(n_items - 2, sa_scr)
    finish(n_items - 1, sb_scr)


def _diff(u3, rope_tabs, lam, subln_w, lam_init):
    b = u3.shape[0]
    L = SEQ
    step_w = DIFF_HEADS_PER_STEP * LANES
    base = COL_DIFF // step_w
    nb = DIFF_WIDTH // step_w

    def spec(part):
        return pl.BlockSpec((1, L, step_w), lambda i, h: (i, 0, base + part * nb + h))

    return pl.pallas_call(
        functools.partial(_diff_kernel, lam_init),
        grid=(b, DIFF_HEADS // DIFF_HEADS_PER_STEP),
        in_specs=[
            spec(0), spec(1), spec(2), spec(3),
            pl.BlockSpec((L, LANES), lambda i, h: (0, 0)),
            pl.BlockSpec((L, LANES), lambda i, h: (0, 0)),
            pl.BlockSpec((L, LANES), lambda i, h: (0, 0)),
            pl.BlockSpec((L, LANES), lambda i, h: (0, 0)),
            pl.BlockSpec((4, DIFF_HEAD_DIM), lambda i, h: (0, 0)),
            pl.BlockSpec((1, LANES), lambda i, h: (0, 0)),
        ],
        out_specs=pl.BlockSpec((1, L, step_w), lambda i, h: (i, 0, h)),
        out_shape=jax.ShapeDtypeStruct((b, L, DIFF_WIDTH), BF16),
        scratch_shapes=[
            pltpu.VMEM((DIFF_HEADS_PER_STEP, L, LANES), BF16),
            pltpu.VMEM((DIFF_HEADS_PER_STEP, 2, L, LANES), BF16),
            pltpu.VMEM((DIFF_HEADS_PER_STEP, LANES + ONES_ROWS, L), BF16),
            pltpu.VMEM((2, L, DIFF_TQ), F32),
            pltpu.VMEM((2, L, DIFF_TQ), F32),
        ],
        compiler_params=pltpu.CompilerParams(
            dimension_semantics=("arbitrary", "arbitrary"), vmem_limit_bytes=VMEM_LIMIT),
        name="diff_attn",
    )(u3, u3, u3, u3, *rope_tabs, lam, subln_w)


def _na_block_geometry(qb):
    r0 = qb * NA_QROWS
    kb = min(max(r0 - NA_KH // 2, 0), ROWS - NA_BAND)
    return r0, kb


def _na_bias_type(qb):
    return 0 if qb == 0 else (2 if qb == NA_NBLK - 1 else 1)


def _na_kernel(q_ref, k_ref, v_ref, g_ref, qkw_ref, bias_ref, o_ref, q_scr, k_scr, vt_scr, s_scr):
    L = SEQ
    seg = _seg_matrix()
    head0_blk = lax.broadcasted_iota(jnp.int32, (NA_QBLK, LANES), 1) < 64
    q_scale = (NA_HEAD_DIM ** -0.5) * LOG2E
    for h in range(2):
        vt_scr[h, NA_HEAD_DIM:NA_HEAD_DIM + ONES_ROWS, :] = jnp.ones((ONES_ROWS, L), BF16)

    def norm(ref, r, w_row):
        v = ref[0, pl.ds(r, NA_QBLK), :].astype(F32)
        return v * lax.rsqrt(_group_mean_sq(v, seg) + EPS) * w_row

    def prep(i, carry):
        r = pl.multiple_of(i * NA_QBLK, NA_QBLK)
        q_scr[pl.ds(r, NA_QBLK), :] = (norm(q_ref, r, qkw_ref[0:1, :]) * q_scale).astype(BF16)
        kn = norm(k_ref, r, qkw_ref[1:2, :])
        k_scr[0, pl.ds(r, NA_QBLK), :] = jnp.where(head0_blk, kn, 0.0).astype(BF16)
        k_scr[1, pl.ds(r, NA_QBLK), :] = jnp.where(head0_blk, 0.0, kn).astype(BF16)
        vt = v_ref[0, pl.ds(r, NA_QBLK), :].T
        for h in range(2):
            vt_scr[h, 0:NA_HEAD_DIM, pl.ds(r, NA_QBLK)] = vt[h * NA_HEAD_DIM:(h + 1) * NA_HEAD_DIM, :]
        return carry

    lax.fori_loop(0, L // NA_QBLK, prep, 0, unroll=True)

    def slices(qb):
        r0, kb = _na_block_geometry(qb)
        return (slice(r0 * GRID_W, r0 * GRID_W + NA_QBLK), slice(kb * GRID_W, kb * GRID_W + NA_KBLK))

    def scores(qb):
        qs, ks = slices(qb)
        q_blk = q_scr[qs, :]
        for h in range(2):
            s_scr[qb % NA_AHEAD, h] = _dot_nt(k_scr[h, ks, :], q_blk)

    def finish(qb):
        qs, ks = slices(qb)
        t = _na_bias_type(qb)
        outs = []
        for h in range(2):
            s = s_scr[qb % NA_AHEAD, h] + bias_ref[h, t]
            e = jnp.exp2(s - jnp.max(s, axis=0, keepdims=True)).astype(BF16)
            acc = _dot(vt_scr[h, :, ks], e)
            outs.append(acc[0:NA_HEAD_DIM, :] * (1.0 / acc[NA_HEAD_DIM:NA_HEAD_DIM + 1, :]))
        o = jnp.concatenate(outs, axis=0).T
        o = o * _silu(g_ref[0, qs, :].astype(F32))
        o_ref[0, qs, :] = o.astype(o_ref.dtype)

    for qb in range(NA_AHEAD - 1):
        scores(qb)
    for qb in range(NA_NBLK):
        if qb + NA_AHEAD - 1 < NA_NBLK:
            scores(qb + NA_AHEAD - 1)
        finish(qb)


def _na(u3, qk_w, bias, layer):
    b = u3.shape[0]
    L = SEQ
    base = COL_NA // LANES
    nb = NA_WIDTH // LANES

    def spec(part):
        return pl.BlockSpec((1, L, LANES), lambda hp, i: (i, 0, base + part * nb + hp))

    return pl.pallas_call(
        _na_kernel,
        grid=(NA_HEADS // 2, b),
        in_specs=[
            spec(0), spec(1), spec(2), spec(3),
            pl.BlockSpec((2, LANES), lambda hp, i: (0, 0)),
            pl.BlockSpec((None, 2, 3, NA_KBLK, NA_QBLK), lambda hp, i: (layer, hp, 0, 0, 0)),
        ],
        out_specs=pl.BlockSpec((1, L, LANES), lambda hp, i: (i, 0, hp)),
        out_shape=jax.ShapeDtypeStruct((b, L, NA_WIDTH), BF16),
        scratch_shapes=[
            pltpu.VMEM((L, LANES), BF16),
            pltpu.VMEM((2, L, LANES), BF16),
            pltpu.VMEM((2, NA_HEAD_DIM + ONES_ROWS, L), BF16),
            pltpu.VMEM((NA_AHEAD, 2, NA_KBLK, NA_QBLK), F32),
        ],
        compiler_params=pltpu.CompilerParams(
            dimension_semantics=("arbitrary", "arbitrary"), vmem_limit_bytes=VMEM_LIMIT),
        name="na_attn",
    )(u3, u3, u3, u3, qk_w, bias)


def _na_bias(rpb):
    lead = rpb.shape[:-2]
    n_dr = 2 * NA_KH - 1
    pad = GRID_W - NA_KW
    rev = jnp.pad(rpb.astype(F32)[..., ::-1] * LOG2E, [(0, 0)] * (rpb.ndim - 1) + [(pad, pad + 1)])
    rev = rev.reshape((-1, n_dr, LANES))
    out = pl.pallas_call(
        _na_bias_kernel,
        grid=(rev.shape[0],),
        in_specs=[pl.BlockSpec((1, n_dr, LANES), lambda n: (n, 0, 0))],
        out_specs=pl.BlockSpec((1, 3, NA_KBLK, NA_QBLK), lambda n: (n, 0, 0, 0)),
        out_shape=jax.ShapeDtypeStruct((rev.shape[0], 3, NA_KBLK, NA_QBLK), F32),
        scratch_shapes=[pltpu.VMEM((n_dr, GRID_W, LANES), F32)],
        compiler_params=pltpu.CompilerParams(
            dimension_semantics=("arbitrary",), vmem_limit_bytes=VMEM_LIMIT),
        name="na_bias",
    )(rev)
    return out.reshape(lead + (3, NA_KBLK, NA_QBLK))


def _na_bias_kernel(r_ref, o_ref, t_scr):
    k_i = lax.broadcasted_iota(jnp.int32, (GRID_W, LANES), 0)
    lane = lax.broadcasted_iota(jnp.int32, (GRID_W, LANES), 1)
    left = lane < GRID_W
    win_start = jnp.clip(lane % GRID_W - NA_KW // 2, 0, GRID_W - NA_KW)
    col_ok = (k_i >= win_start) & (k_i < win_start + NA_KW)
    for d in range(2 * NA_KH - 1):
        rows = jnp.broadcast_to(r_ref[0, d:d + 1, :], (GRID_W, LANES))
        skew = pltpu.roll(rows, GRID_W + 1, axis=1, stride=1, stride_axis=0)
        both = jnp.where(left, skew, pltpu.roll(skew, GRID_W, axis=1))
        t_scr[d] = jnp.where(col_ok, both, NEG)

    masked = jnp.full((GRID_W, LANES), NEG, F32)

    def tile(qb, j, i):
        r0, kb = _na_block_geometry(qb)
        r, rk = r0 + i, kb + j
        rs = min(max(r - NA_KH // 2, 0), ROWS - NA_KH)
        return t_scr[rk - r + NA_KH - 1] if rs <= rk < rs + NA_KH else masked

    for t, qb in enumerate((0, 1, NA_NBLK - 1)):
        for j in range(NA_BAND):
            for p in range(NA_QROWS // 2):
                o_ref[0, t, j * GRID_W:(j + 1) * GRID_W, p * LANES:(p + 1) * LANES] = jnp.where(
                    left, tile(qb, j, 2 * p), tile(qb, j, 2 * p + 1))


OUT_TM = 1024


def _outproj_kernel(x_ref, ys_ref, yd_ref, yn_ref, w_ref, o_ref):
    acc = _dot(ys_ref[...], w_ref[0:SSD_WIDTH, :])
    acc = acc + _dot(yd_ref[...], w_ref[SSD_WIDTH:SSD_WIDTH + DIFF_WIDTH, :])
    acc = acc + _dot(yn_ref[...], w_ref[SSD_WIDTH + DIFF_WIDTH:MIX_WIDTH, :])
    o_ref[...] = x_ref[...] + acc


def _outproj(x2, y_ssd, y_diff, y_na, w_out):
    m = x2.shape[0]
    return pl.pallas_call(
        _outproj_kernel,
        grid=(m // OUT_TM,),
        in_specs=[
            pl.BlockSpec((OUT_TM, D_MODEL), lambda i: (i, 0)),
            pl.BlockSpec((OUT_TM, SSD_WIDTH), lambda i: (i, 0)),
            pl.BlockSpec((OUT_TM, DIFF_WIDTH), lambda i: (i, 0)),
            pl.BlockSpec((OUT_TM, NA_WIDTH), lambda i: (i, 0)),
            pl.BlockSpec((MIX_WIDTH, D_MODEL), lambda i: (0, 0)),
        ],
        out_specs=pl.BlockSpec((OUT_TM, D_MODEL), lambda i: (i, 0)),
        out_shape=jax.ShapeDtypeStruct((m, D_MODEL), F32),
        compiler_params=pltpu.CompilerParams(
            dimension_semantics=("arbitrary",), vmem_limit_bytes=VMEM_LIMIT),
        name="outproj",
    )(x2, y_ssd, y_diff, y_na, w_out)


def _rope_tables():
    inv_freq = ROPE_THETA ** (-jnp.arange(0, DIFF_HEAD_DIM, 2, dtype=F32) / DIFF_HEAD_DIM)
    ang = jnp.arange(SEQ, dtype=F32)[:, None] * inv_freq[None, :]
    cos, sin = jnp.cos(ang), jnp.sin(ang)
    cos_t = jnp.concatenate([cos, cos, cos, cos], axis=1)
    sin_t = jnp.concatenate([-sin, sin, -sin, sin], axis=1)
    return cos_t, sin_t


def kernel(x, norm_w, w_in, conv_w, conv_b, a_log, dt_bias, d_skip, ssd_norm_w, diff_qk_norm,
           diff_lambda, diff_subln, na_qk_norm, na_rpb, w_out):
    b, L, d = x.shape
    assert (L, d) == (SEQ, D_MODEL)
    depth = w_in.shape[0]
    cos_t, sin_t = _rope_tables()
    na_bias = _na_bias(na_rpb)
    dt_lo = SSD_WIDTH + SSD_XBC
    dt_hi = dt_lo + 2 * SSD_HEADS
    w_t = jnp.swapaxes(w_in, 1, 2)
    w_main = jnp.concatenate([w_t[:, :dt_lo], w_t[:, dt_hi:]], axis=1).astype(BF16)
    w_dt = jnp.pad(w_t[:, dt_lo:dt_hi], ((0, 0), (0, DT_PAD - 2 * SSD_HEADS), (0, 0))).astype(BF16)
    x2 = x.reshape(b * L, d)
    for i in range(depth):
        lam_init = 0.8 - 0.6 * math.exp(-0.3 * i)
        u, dt = _inproj(x2, norm_w[i][None, :], w_main, w_dt, i)
        u3 = u.reshape(b, L, U_MAIN)
        dt_t = jnp.swapaxes(dt.reshape(b, L, DT_PAD)[:, :, :2 * SSD_HEADS], 1, 2)

        hp = jnp.stack([a_log[i][0], a_log[i][1], dt_bias[i][0], dt_bias[i][1]], axis=-1)
        hp = jnp.pad(hp, ((0, 0), (0, 4))).reshape(SSD_GROUPS, GROUP_HEADS, 8)
        dsk_row = jnp.repeat(d_skip[i][0] + d_skip[i][1], SSD_HEAD_DIM).reshape(SSD_GROUPS, 1, GROUP_WIDTH)
        y_ssd = _ssd(u3, dt_t, conv_w[i], conv_b[i][None, :], hp, dsk_row,
                     ssd_norm_w[i].reshape(SSD_GROUPS, 1, GROUP_WIDTH))

        half = DIFF_HEAD_DIM // 2
        qk_w = jnp.tile(diff_qk_norm[i], (1, 2))
        qk_w_sw = jnp.tile(jnp.concatenate([diff_qk_norm[i][:, half:], diff_qk_norm[i][:, :half]], axis=1), (1, 2))
        q_scale = (DIFF_HEAD_DIM ** -0.5) * LOG2E
        rope_tabs = (cos_t * (qk_w[0:1] * q_scale), sin_t * (qk_w_sw[0:1] * q_scale),
                     cos_t * qk_w[1:2], sin_t * qk_w_sw[1:2])
        y_diff = _diff(u3, rope_tabs, diff_lambda[i], diff_subln[i][None, :], lam_init)

        na_w = jnp.concatenate([na_qk_norm[i], na_qk_norm[i]], axis=1)
        y_na = _na(u3, na_w, na_bias, i)

        x2 = _outproj(x2, y_ssd.reshape(b * L, SSD_WIDTH), y_diff.reshape(b * L, DIFF_WIDTH),
                      y_na.reshape(b * L, NA_WIDTH), w_out[i].astype(BF16))
    return x2.reshape(b, L, d)
```

```python
import functools
import math

import numpy as np
import jax
import jax.numpy as jnp
from jax import lax
from jax.experimental import pallas as pl
from jax.experimental.pallas import tpu as pltpu

F32 = jnp.float32
BF16 = jnp.bfloat16

D_MODEL = 1024
SEQ = 2048
GRID_W = 64
ROWS = SEQ // GRID_W
SSD_WIDTH = 1024
SSD_HEAD_DIM = 64
SSD_HEADS = 16
SSD_GROUPS = 2
SSD_STATE = 128
SSD_CONV = 5
SSD_CHUNK = 128
SSD_XBC = SSD_WIDTH + 2 * SSD_GROUPS * SSD_STATE
GROUP_HEADS = SSD_HEADS // SSD_GROUPS
GROUP_WIDTH = SSD_WIDTH // SSD_GROUPS
N_CHUNKS = SEQ // SSD_CHUNK
DIFF_WIDTH = 512
DIFF_HEAD_DIM = 64
DIFF_HEADS = 4
NA_WIDTH = 512
NA_HEAD_DIM = 64
NA_HEADS = 8
NA_KH = 8
NA_KW = 16
MIX_WIDTH = SSD_WIDTH + DIFF_WIDTH + NA_WIDTH
ROPE_THETA = 10000.0
EPS = 1e-6
LANES = 128
CONV_HALO = 16
CONV_WIN = SSD_CHUNK + 2 * CONV_HALO

U_MAIN = SSD_WIDTH + SSD_XBC + 4 * DIFF_WIDTH + 4 * NA_WIDTH
COL_Z = 0
COL_X = SSD_WIDTH
COL_B = COL_X + SSD_WIDTH
COL_C = COL_B + SSD_GROUPS * SSD_STATE
COL_DIFF = COL_C + SSD_GROUPS * SSD_STATE
COL_NA = COL_DIFF + 4 * DIFF_WIDTH
DT_PAD = LANES

NA_QROWS = 4
NA_BAND = 12
NA_QBLK = NA_QROWS * GRID_W
NA_KBLK = NA_BAND * GRID_W
NA_NBLK = ROWS // NA_QROWS
NA_AHEAD = 2
NA_BATCH = 2
NEG = -1e30
LOG2E = math.log2(math.e)
ONES_ROWS = 16

VMEM_LIMIT = 56 * 1024 * 1024


def _silu(v):
    h = 0.5 * v
    return h * jnp.tanh(h) + h


def _dot(a, b):
    return jnp.dot(a, b, preferred_element_type=F32)


def _dot_nt(a, b):
    return lax.dot_general(a, b, (((1,), (1,)), ((), ())), preferred_element_type=F32)


def _dot_tn(a, b):
    return lax.dot_general(a, b, (((0,), (0,)), ((), ())), preferred_element_type=F32)


IN_TM = 1024
IN_TN = 3328


def _inproj_kernel(x_ref, nw_ref, w_ref, wdt_ref, u_ref, dt_ref, h_scr):
    @pl.when(pl.program_id(1) == 0)
    def _():
        x = x_ref[...]
        ms = jnp.mean(x * x, axis=-1, keepdims=True)
        h = (x * lax.rsqrt(ms + EPS) * nw_ref[...]).astype(BF16)
        h_scr[...] = h
        dt_ref[...] = _dot_nt(h, wdt_ref[...])

    u_ref[...] = _dot_nt(h_scr[...], w_ref[...]).astype(u_ref.dtype)


def _inproj(x2, norm_w, w_main, w_dt, layer):
    m = x2.shape[0]
    return pl.pallas_call(
        _inproj_kernel,
        grid=(m // IN_TM, U_MAIN // IN_TN),
        in_specs=[
            pl.BlockSpec((IN_TM, D_MODEL), lambda i, j: (i, 0)),
            pl.BlockSpec((1, D_MODEL), lambda i, j: (0, 0)),
            pl.BlockSpec((None, IN_TN, D_MODEL), lambda i, j: (layer, j, 0)),
            pl.BlockSpec((None, DT_PAD, D_MODEL), lambda i, j: (layer, 0, 0)),
        ],
        out_specs=[
            pl.BlockSpec((IN_TM, IN_TN), lambda i, j: (i, j)),
            pl.BlockSpec((IN_TM, DT_PAD), lambda i, j: (i, 0)),
        ],
        out_shape=[
            jax.ShapeDtypeStruct((m, U_MAIN), BF16),
            jax.ShapeDtypeStruct((m, DT_PAD), F32),
        ],
        scratch_shapes=[pltpu.VMEM((IN_TM, D_MODEL), BF16)],
        compiler_params=pltpu.CompilerParams(
            dimension_semantics=("arbitrary", "arbitrary"), vmem_limit_bytes=VMEM_LIMIT),
        name="inproj",
    )(x2, norm_w, w_main, w_dt)


R_WF, R_DF = 0, 1
R_WB, R_DB = 4, 5
R_COLF, R_COLB = 8, 9
R_EF, R_EB = 10, 11
R_ROWF, R_ROWB = 14, 15


def _softplus(v):
    return jnp.maximum(v, 0.0) + jnp.log1p(jnp.exp(-jnp.abs(v)))


def _chunk_scan(a, lane, reverse):
    n = a.shape[-1]
    out = a
    k = 1
    while k < SSD_CHUNK:
        if reverse:
            out = out + jnp.where(lane < SSD_CHUNK - k, pltpu.roll(out, n - k, axis=1), 0.0)
        else:
            out = out + jnp.where(lane >= k, pltpu.roll(out, k, axis=1), 0.0)
        k *= 2
    return out


def _ssd_kernel(z_ref, x_ref, b_ref, c_ref, dtf_ref, dtb_ref, cwx_ref, cwb_ref, cwc_ref,
                cbx_ref, cbb_ref, cbc_ref, hp_ref, dsk_ref, nw_ref, o_ref,
                shift_scr, xs_scr, bt_scr, cm_scr, y_scr, hm_scr, tm_scr, dec_scr, sf_scr, sb_scr):
    L = SEQ
    W = GROUP_WIDTH
    NS = SSD_STATE

    side = (SSD_CONV - 1) * SSD_CHUNK
    tap_t = lax.broadcasted_iota(jnp.int32, (side, CONV_WIN), 0)
    tap_j = lax.broadcasted_iota(jnp.int32, (side, CONV_WIN), 1)
    tap_k = tap_t // SSD_CHUNK
    tap_src = tap_t % SSD_CHUNK + jnp.where(tap_k >= SSD_CONV // 2, tap_k + 1, tap_k) - SSD_CONV // 2
    for variant in range(3):
        shift_scr[variant] = jnp.where(tap_j == tap_src + variant * CONV_HALO, 1.0, 0.0).astype(BF16)
    side_taps = [k for k in range(SSD_CONV) if k != SSD_CONV // 2]

    def conv(c, carry):
        r = pl.multiple_of(c * SSD_CHUNK, SSD_CHUNK)
        w0 = pl.multiple_of(jnp.clip(r - CONV_HALO, 0, L - CONV_WIN), CONV_HALO)
        variant = jnp.where(c == 0, 0, jnp.where(c == N_CHUNKS - 1, 2, 1))
        shift = shift_scr[variant]
        rows = pl.ds(r, SSD_CHUNK)

        def taps(win, centre, cw, cb):
            sh = _dot(shift, win)
            acc = cb + centre.astype(F32) * cw[SSD_CONV // 2:SSD_CONV // 2 + 1, :]
            for slot, k in enumerate(side_taps):
                acc = acc + sh[slot * SSD_CHUNK:(slot + 1) * SSD_CHUNK, :] * cw[k:k + 1, :]
            return _silu(acc)

        xs_scr[rows, :] = taps(x_ref[0, pl.ds(w0, CONV_WIN), :], x_ref[0, rows, :], cwx_ref[...], cbx_ref[...])
        win_bc = jnp.concatenate([b_ref[0, pl.ds(w0, CONV_WIN), :], c_ref[0, pl.ds(w0, CONV_WIN), :]], axis=1)
        mid_bc = jnp.concatenate([b_ref[0, rows, :], c_ref[0, rows, :]], axis=1)
        act_bc = taps(win_bc, mid_bc, jnp.concatenate([cwb_ref[...], cwc_ref[...]], axis=1),
                      jnp.concatenate([cbb_ref[...], cbc_ref[...]], axis=1))
        bt_scr[:, rows] = act_bc[:, 0:NS].T.astype(BF16)
        cm_scr[rows, :] = act_bc[:, NS:2 * NS].astype(BF16)
        return carry

    lax.fori_loop(0, N_CHUNKS, conv, 0, unroll=8)

    hp = hp_ref[0]
    a_f = -jnp.exp(hp[:, 0:1])
    a_b = -jnp.exp(hp[:, 1:2])
    dt_f = _softplus(dtf_ref[0] + hp[:, 2:3])
    dt_b = _softplus(dtb_ref[0] + hp[:, 3:4])
    da_f = dt_f * a_f
    da_b = dt_b * a_b
    lane = lax.broadcasted_iota(jnp.int32, (GROUP_HEADS, L), 1) % SSD_CHUNK
    cs_f = _chunk_scan(da_f, lane, False)
    rs_f = _chunk_scan(da_f, lane, True) - da_f
    cs_b = _chunk_scan(da_b, lane, False)
    ecs_b = cs_b - da_b
    rs_b = _chunk_scan(da_b, lane, True)

    def put(row, v):
        hm_scr[row * GROUP_HEADS:(row + 1) * GROUP_HEADS, :] = v

    def put_split(row_hi, v):
        hi = v.astype(BF16).astype(F32)
        put(row_hi, hi)
        put(row_hi + 2, (v - hi).astype(BF16).astype(F32))

    put_split(R_WF, jnp.exp(rs_f) * dt_f)
    put_split(R_DF, jnp.exp(cs_f))
    put_split(R_WB, jnp.exp(ecs_b) * dt_b)
    put_split(R_DB, jnp.exp(rs_b))
    put_split(R_EF, jnp.exp(cs_f + rs_f))
    put_split(R_EB, jnp.exp(ecs_b + rs_b))
    put(R_COLF, cs_f * LOG2E)
    put(R_COLB, ecs_b * LOG2E)
    put(R_ROWF, cs_f * LOG2E - jnp.log2(dt_f))
    put(R_ROWB, ecs_b * LOG2E + jnp.log2(dt_b))

    def to_time_major(c, carry):
        r = pl.multiple_of(c * SSD_CHUNK, SSD_CHUNK)
        tm_scr[pl.ds(r, SSD_CHUNK), :] = hm_scr[:, pl.ds(r, SSD_CHUNK)].T
        return carry

    lax.fori_loop(0, N_CHUNKS, to_time_major, 0, unroll=True)

    e_row = lax.broadcasted_iota(jnp.int32, (4 * GROUP_HEADS, 2 * W), 0) % (2 * GROUP_HEADS)
    e_col = lax.broadcasted_iota(jnp.int32, (4 * GROUP_HEADS, 2 * W), 1) // SSD_HEAD_DIM
    spread = jnp.where(e_row == e_col, 1.0, 0.0).astype(BF16)

    def expand(rows, first_row):
        cols = slice(first_row * GROUP_HEADS, (first_row + 4) * GROUP_HEADS)
        return _dot(tm_scr[rows, :][:, cols].astype(BF16), spread)

    dec_scr[...] = expand(pl.ds(0, N_CHUNKS, stride=SSD_CHUNK), R_EF)

    row_i = lax.broadcasted_iota(jnp.int32, (SSD_CHUNK, SSD_CHUNK), 0)
    col_i = lax.broadcasted_iota(jnp.int32, (SSD_CHUNK, SSD_CHUNK), 1)
    lower = col_i <= row_i
    upper = col_i >= row_i
    left = lax.broadcasted_iota(jnp.int32, (SSD_CHUNK, LANES), 1) < SSD_HEAD_DIM
    dsk = dsk_ref[0]
    nw = nw_ref[0]

    def hm_row(row, h, r):
        return hm_scr[row * GROUP_HEADS + h:row * GROUP_HEADS + h + 1, pl.ds(r, SSD_CHUNK)]

    def forward_part(c):
        r = pl.multiple_of(c * SSD_CHUNK, SSD_CHUNK)
        rows = pl.ds(r, SSD_CHUNK)
        xs_c = xs_scr[rows, :]
        bt_c = bt_scr[:, rows]
        cm_c = cm_scr[rows, :]
        g = _dot(cm_c, bt_c)
        xs_b = xs_c.astype(BF16)
        colf = tm_scr[rows, R_COLF * GROUP_HEADS:(R_COLF + 1) * GROUP_HEADS]
        colb = tm_scr[rows, R_COLB * GROUP_HEADS:(R_COLB + 1) * GROUP_HEADS]
        y_pairs = []
        for hp_i in range(GROUP_HEADS // 2):
            ms = []
            for h in (2 * hp_i, 2 * hp_i + 1):
                seg_f = jnp.where(lower, colf[:, h:h + 1] - hm_row(R_ROWF, h, r), NEG)
                seg_b = jnp.where(upper, hm_row(R_ROWB, h, r) - colb[:, h:h + 1], NEG)
                ms.append((g * (jnp.exp2(seg_f) + jnp.exp2(seg_b))).astype(BF16))
            xp = xs_b[:, hp_i * LANES:(hp_i + 1) * LANES]
            zero = jnp.zeros_like(xp)
            rhs = jnp.concatenate([jnp.where(left, xp, zero), jnp.where(left, zero, xp)], axis=0)
            y_pairs.append(_dot(jnp.concatenate(ms, axis=1), rhs))
        y_diag = jnp.concatenate(y_pairs, axis=1)
        ex = expand(rows, R_WF)
        s_f = sf_scr[...]
        y_off = _dot(cm_c, s_f.astype(BF16)) * ex[:, W:2 * W]
        xw = (xs_c * ex[:, 0:W]).astype(BF16)
        sf_scr[...] = s_f * dec_scr[pl.ds(c, 1), 0:W] + _dot(bt_c, xw)
        return y_diag + y_off

    def backward_part(c):
        r = pl.multiple_of(c * SSD_CHUNK, SSD_CHUNK)
        rows = pl.ds(r, SSD_CHUNK)
        ex = expand(rows, R_WB)
        s_b = sb_scr[...]
        y_off = _dot(cm_scr[rows, :], s_b.astype(BF16)) * ex[:, W:2 * W]
        xw = (xs_scr[rows, :] * ex[:, 0:W]).astype(BF16)
        sb_scr[...] = s_b * dec_scr[pl.ds(c, 1), W:2 * W] + _dot(bt_scr[:, rows], xw)
        return y_off

    def finalize(c, y):
        r = pl.multiple_of(c * SSD_CHUNK, SSD_CHUNK)
        rows = pl.ds(r, SSD_CHUNK)
        y = y + y_scr[rows, :] + dsk * xs_scr[rows, :]
        y = y * _silu(z_ref[0, rows, :].astype(F32))
        y = y * lax.rsqrt(jnp.mean(y * y, axis=-1, keepdims=True) + EPS) * nw
        o_ref[0, rows, :] = y.astype(o_ref.dtype)

    sf_scr[...] = jnp.zeros((NS, W), F32)
    sb_scr[...] = jnp.zeros((NS, W), F32)
    half = N_CHUNKS // 2

    def first_half(i, carry):
        cb = N_CHUNKS - 1 - i
        y_scr[pl.ds(pl.multiple_of(i * SSD_CHUNK, SSD_CHUNK), SSD_CHUNK), :] = forward_part(i)
        y_scr[pl.ds(pl.multiple_of(cb * SSD_CHUNK, SSD_CHUNK), SSD_CHUNK), :] = backward_part(cb)
        return carry

    def second_half(i, carry):
        cb = N_CHUNKS - 1 - i
        finalize(i, forward_part(i))
        finalize(cb, backward_part(cb))
        return carry

    lax.fori_loop(0, half, first_half, 0, unroll=True)
    lax.fori_loop(half, N_CHUNKS, second_half, 0, unroll=True)


def _ssd(u3, dt_t, conv_w, conv_b, head_params, dsk_row, norm_w):
    b = u3.shape[0]
    L = SEQ
    W = GROUP_WIDTH
    NS = SSD_STATE
    G = SSD_GROUPS
    xblk = COL_X // W
    bblk = COL_B // NS
    cblk = COL_C // NS
    in_specs = [
        pl.BlockSpec((1, L, W), lambda i, g: (i, 0, g)),
        pl.BlockSpec((1, L, W), lambda i, g: (i, 0, xblk + g)),
        pl.BlockSpec((1, L, NS), lambda i, g: (i, 0, bblk + g)),
        pl.BlockSpec((1, L, NS), lambda i, g: (i, 0, cblk + g)),
        pl.BlockSpec((1, GROUP_HEADS, L), lambda i, g: (i, g, 0)),
        pl.BlockSpec((1, GROUP_HEADS, L), lambda i, g: (i, G + g, 0)),
        pl.BlockSpec((SSD_CONV, W), lambda i, g: (0, g)),
        pl.BlockSpec((SSD_CONV, NS), lambda i, g: (0, SSD_WIDTH // NS + g)),
        pl.BlockSpec((SSD_CONV, NS), lambda i, g: (0, SSD_WIDTH // NS + G + g)),
        pl.BlockSpec((1, W), lambda i, g: (0, g)),
        pl.BlockSpec((1, NS), lambda i, g: (0, SSD_WIDTH // NS + g)),
        pl.BlockSpec((1, NS), lambda i, g: (0, SSD_WIDTH // NS + G + g)),
        pl.BlockSpec((1, GROUP_HEADS, 8), lambda i, g: (g, 0, 0)),
        pl.BlockSpec((1, 1, W), lambda i, g: (g, 0, 0)),
        pl.BlockSpec((1, 1, W), lambda i, g: (g, 0, 0)),
    ]
    return pl.pallas_call(
        _ssd_kernel,
        grid=(b, G),
        in_specs=in_specs,
        out_specs=pl.BlockSpec((1, L, W), lambda i, g: (i, 0, g)),
        out_shape=jax.ShapeDtypeStruct((b, L, SSD_WIDTH), BF16),
        scratch_shapes=[
            pltpu.VMEM((3, (SSD_CONV - 1) * SSD_CHUNK, CONV_WIN), BF16),
            pltpu.VMEM((L, W), F32),
            pltpu.VMEM((NS, L), BF16),
            pltpu.VMEM((L, NS), BF16),
            pltpu.VMEM((L, W), F32),
            pltpu.VMEM((LANES, L), F32),
            pltpu.VMEM((L, LANES), F32),
            pltpu.VMEM((N_CHUNKS, 2 * W), F32),
            pltpu.VMEM((NS, W), F32),
            pltpu.VMEM((NS, W), F32),
        ],
        compiler_params=pltpu.CompilerParams(
            dimension_semantics=("arbitrary", "arbitrary"), vmem_limit_bytes=VMEM_LIMIT),
        name="ssd",
    )(u3, u3, u3, u3, dt_t, dt_t, conv_w, conv_w, conv_w, conv_b, conv_b, conv_b,
      head_params, dsk_row, norm_w)


def _group_mean_sq(v, seg):
    return _dot((v * v).astype(BF16), seg) * (1.0 / 64.0)


def _seg_matrix():
    r = lax.broadcasted_iota(jnp.int32, (LANES, LANES), 0) // 64
    c = lax.broadcasted_iota(jnp.int32, (LANES, LANES), 1) // 64
    return jnp.where(r == c, 1.0, 0.0).astype(BF16)


DIFF_TQ = 256
DIFF_HEADS_PER_STEP = 2


def _diff_kernel(lam_init, q_ref, k_ref, v_ref, g_ref, qc_ref, qs_ref, kc_ref, ks_ref, lam_ref,
                 sub_ref, o_ref, q_scr, k_scr, vt_scr, sa_scr, sb_scr):
    L = SEQ
    seg = _seg_matrix()
    comp0 = lax.broadcasted_iota(jnp.int32, (DIFF_TQ, LANES), 1) < 64
    p_row = lax.broadcasted_iota(jnp.int32, (LANES, LANES), 0)
    p_col = lax.broadcasted_iota(jnp.int32, (LANES, LANES), 1)
    swap = jnp.where((p_row ^ 32) == p_col, 1.0, 0.0).astype(BF16)

    def norm_rope(ref, r, lanes, cw_ref, sw_ref):
        vb = ref[0, pl.ds(r, DIFF_TQ), lanes]
        v = vb.astype(F32)
        rinv = lax.rsqrt(_group_mean_sq(v, seg) + EPS)
        return rinv * (v * cw_ref[pl.ds(r, DIFF_TQ), :] + _dot(vb, swap) * sw_ref[pl.ds(r, DIFF_TQ), :])

    n_blk = L // DIFF_TQ
    n_items = DIFF_HEADS_PER_STEP * n_blk

    def item(j):
        hd = j // n_blk
        r = pl.multiple_of((j % n_blk) * DIFF_TQ, DIFF_TQ)
        return hd, r, pl.ds(pl.multiple_of(hd * LANES, LANES), LANES)

    for hd in range(DIFF_HEADS_PER_STEP):
        vt_scr[hd, LANES:LANES + ONES_ROWS, :] = jnp.ones((ONES_ROWS, L), BF16)

    def prep(j, carry):
        hd, r, lanes = item(j)
        q_scr[hd, pl.ds(r, DIFF_TQ), :] = norm_rope(q_ref, r, lanes, qc_ref, qs_ref).astype(BF16)
        kn = norm_rope(k_ref, r, lanes, kc_ref, ks_ref)
        k_scr[hd, 0, pl.ds(r, DIFF_TQ), :] = jnp.where(comp0, kn, 0.0).astype(BF16)
        k_scr[hd, 1, pl.ds(r, DIFF_TQ), :] = jnp.where(comp0, 0.0, kn).astype(BF16)
        vt_scr[hd, 0:LANES, pl.ds(r, DIFF_TQ)] = v_ref[0, pl.ds(r, DIFF_TQ), lanes].T
        return carry

    lax.fori_loop(0, n_items, prep, 0, unroll=True)

    lam = lam_ref[...]
    lam_full = (jnp.exp(jnp.sum(lam[0:1] * lam[1:2], axis=-1, keepdims=True))
                - jnp.exp(jnp.sum(lam[2:3] * lam[3:4], axis=-1, keepdims=True)) + lam_init)

    def scores(j, s_ref):
        hd, r, _ = item(j)
        qb = q_scr[hd, pl.ds(r, DIFF_TQ), :]
        for c in range(2):
            s_ref[c] = _dot_nt(k_scr[hd, c], qb)

    def finish(j, s_ref):
        hd, r, lanes = item(j)
        parts = []
        for c in range(2):
            s = s_ref[c]
            e = jnp.exp2(s - jnp.max(s, axis=0, keepdims=True)).astype(BF16)
            acc = _dot(vt_scr[hd], e)
            parts.append(acc[0:LANES, :] * (1.0 / acc[LANES:LANES + 1, :]))
        ot = parts[0] - lam_full * parts[1]
        o = ot.T
        o = o * lax.rsqrt(jnp.mean(o * o, axis=-1, keepdims=True) + EPS) * sub_ref[...]
        o = o * (1.0 - lam_init) * _silu(g_ref[0, pl.ds(r, DIFF_TQ), lanes].astype(F32))
        o_ref[0, pl.ds(r, DIFF_TQ), lanes] = o.astype(o_ref.dtype)

    scores(0, sa_scr)

    def pair(i, carry):
        scores(2 * i + 1, sb_scr)
        finish(2 * i, sa_scr)
        scores(2 * i + 2, sa_scr)
        finish(2 * i + 1, sb_scr)
        return carry

    lax.fori_loop(0, n_items // 2 - 1, pair, 0, unroll=2)
    scores(n_items - 1, sb_scr)
    finish(n_items - 2, sa_scr)
    finish(n_items - 1, sb_scr)


def _diff(u3, rope_tabs, lam, subln_w, lam_init):
    b = u3.shape[0]
    L = SEQ
    step_w = DIFF_HEADS_PER_STEP * LANES
    base = COL_DIFF // step_w
    nb = DIFF_WIDTH // step_w

    def spec(part):
        return pl.BlockSpec((1, L, step_w), lambda i, h: (i, 0, base + part * nb + h))

    return pl.pallas_call(
        functools.partial(_diff_kernel, lam_init),
        grid=(b, DIFF_HEADS // DIFF_HEADS_PER_STEP),
        in_specs=[
            spec(0), spec(1), spec(2), spec(3),
            pl.BlockSpec((L, LANES), lambda i, h: (0, 0)),
            pl.BlockSpec((L, LANES), lambda i, h: (0, 0)),
            pl.BlockSpec((L, LANES), lambda i, h: (0, 0)),
            pl.BlockSpec((L, LANES), lambda i, h: (0, 0)),
            pl.BlockSpec((4, DIFF_HEAD_DIM), lambda i, h: (0, 0)),
            pl.BlockSpec((1, LANES), lambda i, h: (0, 0)),
        ],
        out_specs=pl.BlockSpec((1, L, step_w), lambda i, h: (i, 0, h)),
        out_shape=jax.ShapeDtypeStruct((b, L, DIFF_WIDTH), BF16),
        scratch_shapes=[
            pltpu.VMEM((DIFF_HEADS_PER_STEP, L, LANES), BF16),
            pltpu.VMEM((DIFF_HEADS_PER_STEP, 2, L, LANES), BF16),
            pltpu.VMEM((DIFF_HEADS_PER_STEP, LANES + ONES_ROWS, L), BF16),
            pltpu.VMEM((2, L, DIFF_TQ), F32),
            pltpu.VMEM((2, L, DIFF_TQ), F32),
        ],
        compiler_params=pltpu.CompilerParams(
            dimension_semantics=("arbitrary", "arbitrary"), vmem_limit_bytes=VMEM_LIMIT),
        name="diff_attn",
    )(u3, u3, u3, u3, *rope_tabs, lam, subln_w)


def _na_block_geometry(qb):
    r0 = qb * NA_QROWS
    kb = min(max(r0 - NA_KH // 2, 0), ROWS - NA_BAND)
    return r0, kb


def _na_bias_type(qb):
    return 0 if qb == 0 else (2 if qb == NA_NBLK - 1 else 1)


def _na_kernel(q_ref, k_ref, v_ref, g_ref, qkw_ref, bias_ref, o_ref, q_scr, k_scr, vt_scr, s_scr):
    L = SEQ
    seg = _seg_matrix()
    head0_blk = lax.broadcasted_iota(jnp.int32, (NA_QBLK, LANES), 1) < 64
    q_scale = (NA_HEAD_DIM ** -0.5) * LOG2E
    for bb in range(NA_BATCH):
        for h in range(2):
            vt_scr[bb, h, NA_HEAD_DIM:NA_HEAD_DIM + ONES_ROWS, :] = jnp.ones((ONES_ROWS, L), BF16)

    def norm(ref, bb, r, w_row):
        v = ref[bb, r:r + NA_QBLK, :].astype(F32)
        return v * lax.rsqrt(_group_mean_sq(v, seg) + EPS) * w_row

    def prep(bb, i):
        r = i * NA_QBLK
        q_scr[bb, r:r + NA_QBLK, :] = (norm(q_ref, bb, r, qkw_ref[0:1, :]) * q_scale).astype(BF16)
        kn = norm(k_ref, bb, r, qkw_ref[1:2, :])
        k_scr[bb, 0, r:r + NA_QBLK, :] = jnp.where(head0_blk, kn, 0.0).astype(BF16)
        k_scr[bb, 1, r:r + NA_QBLK, :] = jnp.where(head0_blk, 0.0, kn).astype(BF16)
        vt = v_ref[bb, r:r + NA_QBLK, :].T
        for h in range(2):
            vt_scr[bb, h, 0:NA_HEAD_DIM, r:r + NA_QBLK] = vt[h * NA_HEAD_DIM:(h + 1) * NA_HEAD_DIM, :]

    def slices(qb):
        r0, kb = _na_block_geometry(qb)
        return (slice(r0 * GRID_W, r0 * GRID_W + NA_QBLK), slice(kb * GRID_W, kb * GRID_W + NA_KBLK))

    def scores(n):
        bb, qb = divmod(n, NA_NBLK)
        qs, ks = slices(qb)
        q_blk = q_scr[bb, qs, :]
        for h in range(2):
            s_scr[n % NA_AHEAD, h] = _dot_nt(k_scr[bb, h, ks, :], q_blk)

    def finish(n):
        bb, qb = divmod(n, NA_NBLK)
        qs, ks = slices(qb)
        t = _na_bias_type(qb)
        outs = []
        for h in range(2):
            s = s_scr[n % NA_AHEAD, h] + bias_ref[h, t]
            e = jnp.exp2(s - jnp.max(s, axis=0, keepdims=True)).astype(BF16)
            acc = _dot(vt_scr[bb, h, :, ks], e)
            outs.append(acc[0:NA_HEAD_DIM, :] * (1.0 / acc[NA_HEAD_DIM:NA_HEAD_DIM + 1, :]))
        o = jnp.concatenate(outs, axis=0).T
        o = o * _silu(g_ref[bb, qs, :].astype(F32))
        o_ref[bb, qs, :] = o.astype(o_ref.dtype)

    for bb in range(NA_BATCH):
        for i in range(L // NA_QBLK):
            prep(bb, i)
    n_items = NA_BATCH * NA_NBLK
    for n in range(NA_AHEAD - 1):
        scores(n)
    for n in range(n_items):
        if n + NA_AHEAD - 1 < n_items:
            scores(n + NA_AHEAD - 1)
        finish(n)


def _na(u3, qk_w, bias, layer):
    b = u3.shape[0]
    L = SEQ
    base = COL_NA // LANES
    nb = NA_WIDTH // LANES

    def spec(part):
        return pl.BlockSpec((NA_BATCH, L, LANES), lambda hp, i: (i, 0, base + part * nb + hp))

    assert b % NA_BATCH == 0
    return pl.pallas_call(
        _na_kernel,
        grid=(NA_HEADS // 2, b // NA_BATCH),
        in_specs=[
            spec(0), spec(1), spec(2), spec(3),
            pl.BlockSpec((2, LANES), lambda hp, i: (0, 0)),
            pl.BlockSpec((None, 2, 3, NA_KBLK, NA_QBLK), lambda hp, i: (layer, hp, 0, 0, 0)),
        ],
        out_specs=pl.BlockSpec((NA_BATCH, L, LANES), lambda hp, i: (i, 0, hp)),
        out_shape=jax.ShapeDtypeStruct((b, L, NA_WIDTH), BF16),
        scratch_shapes=[
            pltpu.VMEM((NA_BATCH, L, LANES), BF16),
            pltpu.VMEM((NA_BATCH, 2, L, LANES), BF16),
            pltpu.VMEM((NA_BATCH, 2, NA_HEAD_DIM + ONES_ROWS, L), BF16),
            pltpu.VMEM((NA_AHEAD, 2, NA_KBLK, NA_QBLK), F32),
        ],
        compiler_params=pltpu.CompilerParams(
            dimension_semantics=("arbitrary", "arbitrary"), vmem_limit_bytes=VMEM_LIMIT),
        name="na_attn",
    )(u3, u3, u3, u3, qk_w, bias)


def _na_bias(rpb):
    lead = rpb.shape[:-2]
    n_dr = 2 * NA_KH - 1
    pad = GRID_W - NA_KW
    rev = jnp.pad(rpb.astype(F32)[..., ::-1] * LOG2E, [(0, 0)] * (rpb.ndim - 1) + [(pad, pad + 1)])
    rev = rev.reshape((-1, n_dr, LANES))
    out = pl.pallas_call(
        _na_bias_kernel,
        grid=(rev.shape[0],),
        in_specs=[pl.BlockSpec((1, n_dr, LANES), lambda n: (n, 0, 0))],
        out_specs=pl.BlockSpec((1, 3, NA_KBLK, NA_QBLK), lambda n: (n, 0, 0, 0)),
        out_shape=jax.ShapeDtypeStruct((rev.shape[0], 3, NA_KBLK, NA_QBLK), F32),
        scratch_shapes=[pltpu.VMEM((n_dr, GRID_W, LANES), F32)],
        compiler_params=pltpu.CompilerParams(
            dimension_semantics=("arbitrary",), vmem_limit_bytes=VMEM_LIMIT),
        name="na_bias",
    )(rev)
    return out.reshape(lead + (3, NA_KBLK, NA_QBLK))


def _na_bias_kernel(r_ref, o_ref, t_scr):
    k_i = lax.broadcasted_iota(jnp.int32, (GRID_W, LANES), 0)
    lane = lax.broadcasted_iota(jnp.int32, (GRID_W, LANES), 1)
    left = lane < GRID_W
    win_start = jnp.clip(lane % GRID_W - NA_KW // 2, 0, GRID_W - NA_KW)
    col_ok = (k_i >= win_start) & (k_i < win_start + NA_KW)
    for d in range(2 * NA_KH - 1):
        rows = jnp.broadcast_to(r_ref[0, d:d + 1, :], (GRID_W, LANES))
        skew = pltpu.roll(rows, GRID_W + 1, axis=1, stride=1, stride_axis=0)
        both = jnp.where(left, skew, pltpu.roll(skew, GRID_W, axis=1))
        t_scr[d] = jnp.where(col_ok, both, NEG)

    masked = jnp.full((GRID_W, LANES), NEG, F32)

    def tile(qb, j, i):
        r0, kb = _na_block_geometry(qb)
        r, rk = r0 + i, kb + j
        rs = min(max(r - NA_KH // 2, 0), ROWS - NA_KH)
        return t_scr[rk - r + NA_KH - 1] if rs <= rk < rs + NA_KH else masked

    for t, qb in enumerate((0, 1, NA_NBLK - 1)):
        for j in range(NA_BAND):
            for p in range(NA_QROWS // 2):
                o_ref[0, t, j * GRID_W:(j + 1) * GRID_W, p * LANES:(p + 1) * LANES] = jnp.where(
                    left, tile(qb, j, 2 * p), tile(qb, j, 2 * p + 1))


OUT_TM = 1024


def _outproj_kernel(x_ref, ys_ref, yd_ref, yn_ref, w_ref, o_ref):
    acc = _dot(ys_ref[...], w_ref[0:SSD_WIDTH, :])
    acc = acc + _dot(yd_ref[...], w_ref[SSD_WIDTH:SSD_WIDTH + DIFF_WIDTH, :])
    acc = acc + _dot(yn_ref[...], w_ref[SSD_WIDTH + DIFF_WIDTH:MIX_WIDTH, :])
    o_ref[...] = x_ref[...] + acc


def _outproj(x2, y_ssd, y_diff, y_na, w_out):
    m = x2.shape[0]
    return pl.pallas_call(
        _outproj_kernel,
        grid=(m // OUT_TM,),
        in_specs=[
            pl.BlockSpec((OUT_TM, D_MODEL), lambda i: (i, 0)),
            pl.BlockSpec((OUT_TM, SSD_WIDTH), lambda i: (i, 0)),
            pl.BlockSpec((OUT_TM, DIFF_WIDTH), lambda i: (i, 0)),
            pl.BlockSpec((OUT_TM, NA_WIDTH), lambda i: (i, 0)),
            pl.BlockSpec((MIX_WIDTH, D_MODEL), lambda i: (0, 0)),
        ],
        out_specs=pl.BlockSpec((OUT_TM, D_MODEL), lambda i: (i, 0)),
        out_shape=jax.ShapeDtypeStruct((m, D_MODEL), F32),
        compiler_params=pltpu.CompilerParams(
            dimension_semantics=("arbitrary",), vmem_limit_bytes=VMEM_LIMIT),
        name="outproj",
    )(x2, y_ssd, y_diff, y_na, w_out)


def _rope_tables():
    inv_freq = ROPE_THETA ** (-jnp.arange(0, DIFF_HEAD_DIM, 2, dtype=F32) / DIFF_HEAD_DIM)
    ang = jnp.arange(SEQ, dtype=F32)[:, None] * inv_freq[None, :]
    cos, sin = jnp.cos(ang), jnp.sin(ang)
    cos_t = jnp.concatenate([cos, cos, cos, cos], axis=1)
    sin_t = jnp.concatenate([-sin, sin, -sin, sin], axis=1)
    return cos_t, sin_t


def kernel(x, norm_w, w_in, conv_w, conv_b, a_log, dt_bias, d_skip, ssd_norm_w, diff_qk_norm,
           diff_lambda, diff_subln, na_qk_norm, na_rpb, w_out):
    b, L, d = x.shape
    assert (L, d) == (SEQ, D_MODEL)
    depth = w_in.shape[0]
    cos_t, sin_t = _rope_tables()
    na_bias = _na_bias(na_rpb)
    dt_lo = SSD_WIDTH + SSD_XBC
    dt_hi = dt_lo + 2 * SSD_HEADS
    w_t = jnp.swapaxes(w_in, 1, 2)
    w_main = jnp.concatenate([w_t[:, :dt_lo], w_t[:, dt_hi:]], axis=1).astype(BF16)
    w_dt = jnp.pad(w_t[:, dt_lo:dt_hi], ((0, 0), (0, DT_PAD - 2 * SSD_HEADS), (0, 0))).astype(BF16)
    x2 = x.reshape(b * L, d)
    for i in range(depth):
        lam_init = 0.8 - 0.6 * math.exp(-0.3 * i)
        u, dt = _inproj(x2, norm_w[i][None, :], w_main, w_dt, i)
        u3 = u.reshape(b, L, U_MAIN)
        dt_t = jnp.swapaxes(dt.reshape(b, L, DT_PAD)[:, :, :2 * SSD_HEADS], 1, 2)

        hp = jnp.stack([a_log[i][0], a_log[i][1], dt_bias[i][0], dt_bias[i][1]], axis=-1)
        hp = jnp.pad(hp, ((0, 0), (0, 4))).reshape(SSD_GROUPS, GROUP_HEADS, 8)
        dsk_row = jnp.repeat(d_skip[i][0] + d_skip[i][1], SSD_HEAD_DIM).reshape(SSD_GROUPS, 1, GROUP_WIDTH)
        y_ssd = _ssd(u3, dt_t, conv_w[i], conv_b[i][None, :], hp, dsk_row,
                     ssd_norm_w[i].reshape(SSD_GROUPS, 1, GROUP_WIDTH))

        half = DIFF_HEAD_DIM // 2
        qk_w = jnp.tile(diff_qk_norm[i], (1, 2))
        qk_w_sw = jnp.tile(jnp.concatenate([diff_qk_norm[i][:, half:], diff_qk_norm[i][:, :half]], axis=1), (1, 2))
        q_scale = (DIFF_HEAD_DIM ** -0.5) * LOG2E
        rope_tabs = (cos_t * (qk_w[0:1] * q_scale), sin_t * (qk_w_sw[0:1] * q_scale),
                     cos_t * qk_w[1:2], sin_t * qk_w_sw[1:2])
        y_diff = _diff(u3, rope_tabs, diff_lambda[i], diff_subln[i][None, :], lam_init)

        na_w = jnp.concatenate([na_qk_norm[i], na_qk_norm[i]], axis=1)
        y_na = _na(u3, na_w, na_bias, i)

        x2 = _outproj(x2, y_ssd.reshape(b * L, SSD_WIDTH), y_diff.reshape(b * L, DIFF_WIDTH),
                      y_na.reshape(b * L, NA_WIDTH), w_out[i].astype(BF16))
    return x2.reshape(b, L, d)
```

```python
import functools
import math

import numpy as np
import jax
import jax.numpy as jnp
from jax import lax
from jax.experimental import pallas as pl
from jax.experimental.pallas import tpu as pltpu

F32 = jnp.float32
BF16 = jnp.bfloat16

D_MODEL = 1024
SEQ = 2048
GRID_W = 64
ROWS = SEQ // GRID_W
SSD_WIDTH = 1024
SSD_HEAD_DIM = 64
SSD_HEADS = 16
SSD_GROUPS = 2
SSD_STATE = 128
SSD_CONV = 5
SSD_CHUNK = 128
SSD_XBC = SSD_WIDTH + 2 * SSD_GROUPS * SSD_STATE
GROUP_HEADS = SSD_HEADS // SSD_GROUPS
GROUP_WIDTH = SSD_WIDTH // SSD_GROUPS
N_CHUNKS = SEQ // SSD_CHUNK
DIFF_WIDTH = 512
DIFF_HEAD_DIM = 64
DIFF_HEADS = 4
NA_WIDTH = 512
NA_HEAD_DIM = 64
NA_HEADS = 8
NA_KH = 8
NA_KW = 16
MIX_WIDTH = SSD_WIDTH + DIFF_WIDTH + NA_WIDTH
ROPE_THETA = 10000.0
EPS = 1e-6
LANES = 128
CONV_HALO = 16
CONV_WIN = SSD_CHUNK + 2 * CONV_HALO

U_MAIN = SSD_WIDTH + SSD_XBC + 4 * DIFF_WIDTH + 4 * NA_WIDTH
COL_Z = 0
COL_X = SSD_WIDTH
COL_B = COL_X + SSD_WIDTH
COL_C = COL_B + SSD_GROUPS * SSD_STATE
COL_DIFF = COL_C + SSD_GROUPS * SSD_STATE
COL_NA = COL_DIFF + 4 * DIFF_WIDTH
DT_PAD = LANES

NA_QROWS = 4
NA_BAND = 12
NA_QBLK = NA_QROWS * GRID_W
NA_KBLK = NA_BAND * GRID_W
NA_NBLK = ROWS // NA_QROWS
NA_AHEAD = 2
NEG = -1e30
LOG2E = math.log2(math.e)
ONES_ROWS = 16

VMEM_LIMIT = 56 * 1024 * 1024


def _silu(v):
    h = 0.5 * v
    return h * jnp.tanh(h) + h


def _dot(a, b):
    return jnp.dot(a, b, preferred_element_type=F32)


def _dot_nt(a, b):
    return lax.dot_general(a, b, (((1,), (1,)), ((), ())), preferred_element_type=F32)


def _dot_tn(a, b):
    return lax.dot_general(a, b, (((0,), (0,)), ((), ())), preferred_element_type=F32)


IN_TM = 1024
IN_TN = 3328


def _inproj_kernel(x_ref, nw_ref, w_ref, wdt_ref, u_ref, dt_ref, h_scr):
    @pl.when(pl.program_id(1) == 0)
    def _():
        x = x_ref[...]
        ms = jnp.mean(x * x, axis=-1, keepdims=True)
        h = (x * lax.rsqrt(ms + EPS) * nw_ref[...]).astype(BF16)
        h_scr[...] = h
        dt_ref[...] = _dot_nt(h, wdt_ref[...])

    u_ref[...] = _dot_nt(h_scr[...], w_ref[...]).astype(u_ref.dtype)


def _inproj(x2, norm_w, w_main, w_dt, layer):
    m = x2.shape[0]
    return pl.pallas_call(
        _inproj_kernel,
        grid=(m // IN_TM, U_MAIN // IN_TN),
        in_specs=[
            pl.BlockSpec((IN_TM, D_MODEL), lambda i, j: (i, 0)),
            pl.BlockSpec((1, D_MODEL), lambda i, j: (0, 0)),
            pl.BlockSpec((None, IN_TN, D_MODEL), lambda i, j: (layer, j, 0)),
            pl.BlockSpec((None, DT_PAD, D_MODEL), lambda i, j: (layer, 0, 0)),
        ],
        out_specs=[
            pl.BlockSpec((IN_TM, IN_TN), lambda i, j: (i, j)),
            pl.BlockSpec((IN_TM, DT_PAD), lambda i, j: (i, 0)),
        ],
        out_shape=[
            jax.ShapeDtypeStruct((m, U_MAIN), BF16),
            jax.ShapeDtypeStruct((m, DT_PAD), F32),
        ],
        scratch_shapes=[pltpu.VMEM((IN_TM, D_MODEL), BF16)],
        compiler_params=pltpu.CompilerParams(
            dimension_semantics=("arbitrary", "arbitrary"), vmem_limit_bytes=VMEM_LIMIT),
        name="inproj",
    )(x2, norm_w, w_main, w_dt)


R_WF, R_DF = 0, 1
R_WB, R_DB = 4, 5
R_COLF, R_COLB = 8, 9
R_EF, R_EB = 10, 11
R_ROWF, R_ROWB = 14, 15


def _softplus(v):
    return jnp.maximum(v, 0.0) + jnp.log1p(jnp.exp(-jnp.abs(v)))


def _chunk_scan(a, lane, reverse):
    n = a.shape[-1]
    out = a
    k = 1
    while k < SSD_CHUNK:
        if reverse:
            out = out + jnp.where(lane < SSD_CHUNK - k, pltpu.roll(out, n - k, axis=1), 0.0)
        else:
            out = out + jnp.where(lane >= k, pltpu.roll(out, k, axis=1), 0.0)
        k *= 2
    return out


def _ssd_kernel(z_ref, x_ref, b_ref, c_ref, dtf_ref, dtb_ref, cwx_ref, cwb_ref, cwc_ref,
                cbx_ref, cbb_ref, cbc_ref, hp_ref, dsk_ref, nw_ref, o_ref,
                shift_scr, xs_scr, bt_scr, cm_scr, y_scr, hm_scr, tm_scr, dec_scr, sf_scr, sb_scr):
    L = SEQ
    W = GROUP_WIDTH
    NS = SSD_STATE

    side = (SSD_CONV - 1) * SSD_CHUNK
    tap_t = lax.broadcasted_iota(jnp.int32, (side, CONV_WIN), 0)
    tap_j = lax.broadcasted_iota(jnp.int32, (side, CONV_WIN), 1)
    tap_k = tap_t // SSD_CHUNK
    tap_src = tap_t % SSD_CHUNK + jnp.where(tap_k >= SSD_CONV // 2, tap_k + 1, tap_k) - SSD_CONV // 2
    for variant in range(3):
        shift_scr[variant] = jnp.where(tap_j == tap_src + variant * CONV_HALO, 1.0, 0.0).astype(BF16)
    side_taps = [k for k in range(SSD_CONV) if k != SSD_CONV // 2]

    def conv(c, carry):
        r = pl.multiple_of(c * SSD_CHUNK, SSD_CHUNK)
        w0 = pl.multiple_of(jnp.clip(r - CONV_HALO, 0, L - CONV_WIN), CONV_HALO)
        variant = jnp.where(c == 0, 0, jnp.where(c == N_CHUNKS - 1, 2, 1))
        shift = shift_scr[variant]
        rows = pl.ds(r, SSD_CHUNK)

        def taps(win, centre, cw, cb):
            sh = _dot(shift, win)
            acc = cb + centre.astype(F32) * cw[SSD_CONV // 2:SSD_CONV // 2 + 1, :]
            for slot, k in enumerate(side_taps):
                acc = acc + sh[slot * SSD_CHUNK:(slot + 1) * SSD_CHUNK, :] * cw[k:k + 1, :]
            return _silu(acc)

        xs_scr[rows, :] = taps(x_ref[0, pl.ds(w0, CONV_WIN), :], x_ref[0, rows, :], cwx_ref[...], cbx_ref[...])
        win_bc = jnp.concatenate([b_ref[0, pl.ds(w0, CONV_WIN), :], c_ref[0, pl.ds(w0, CONV_WIN), :]], axis=1)
        mid_bc = jnp.concatenate([b_ref[0, rows, :], c_ref[0, rows, :]], axis=1)
        act_bc = taps(win_bc, mid_bc, jnp.concatenate([cwb_ref[...], cwc_ref[...]], axis=1),
                      jnp.concatenate([cbb_ref[...], cbc_ref[...]], axis=1))
        bt_scr[:, rows] = act_bc[:, 0:NS].T.astype(BF16)
        cm_scr[rows, :] = act_bc[:, NS:2 * NS].astype(BF16)
        return carry

    lax.fori_loop(0, N_CHUNKS, conv, 0, unroll=True)

    hp = hp_ref[0]
    a_f = -jnp.exp(hp[:, 0:1])
    a_b = -jnp.exp(hp[:, 1:2])
    dt_f = _softplus(dtf_ref[0] + hp[:, 2:3])
    dt_b = _softplus(dtb_ref[0] + hp[:, 3:4])
    da_f = dt_f * a_f
    da_b = dt_b * a_b
    lane = lax.broadcasted_iota(jnp.int32, (GROUP_HEADS, L), 1) % SSD_CHUNK
    cs_f = _chunk_scan(da_f, lane, False)
    rs_f = _chunk_scan(da_f, lane, True) - da_f
    cs_b = _chunk_scan(da_b, lane, False)
    ecs_b = cs_b - da_b
    rs_b = _chunk_scan(da_b, lane, True)

    def put(row, v):
        hm_scr[row * GROUP_HEADS:(row + 1) * GROUP_HEADS, :] = v

    def put_split(row_hi, v):
        hi = v.astype(BF16).astype(F32)
        put(row_hi, hi)
        put(row_hi + 2, (v - hi).astype(BF16).astype(F32))

    put_split(R_WF, jnp.exp(rs_f) * dt_f)
    put_split(R_DF, jnp.exp(cs_f))
    put_split(R_WB, jnp.exp(ecs_b) * dt_b)
    put_split(R_DB, jnp.exp(rs_b))
    put_split(R_EF, jnp.exp(cs_f + rs_f))
    put_split(R_EB, jnp.exp(ecs_b + rs_b))
    put(R_COLF, cs_f * LOG2E)
    put(R_COLB, ecs_b * LOG2E)
    put(R_ROWF, cs_f * LOG2E - jnp.log2(dt_f))
    put(R_ROWB, ecs_b * LOG2E + jnp.log2(dt_b))

    def to_time_major(c, carry):
        r = pl.multiple_of(c * SSD_CHUNK, SSD_CHUNK)
        tm_scr[pl.ds(r, SSD_CHUNK), :] = hm_scr[:, pl.ds(r, SSD_CHUNK)].T
        return carry

    lax.fori_loop(0, N_CHUNKS, to_time_major, 0, unroll=True)

    e_row = lax.broadcasted_iota(jnp.int32, (4 * GROUP_HEADS, 2 * W), 0) % (2 * GROUP_HEADS)
    e_col = lax.broadcasted_iota(jnp.int32, (4 * GROUP_HEADS, 2 * W), 1) // SSD_HEAD_DIM
    spread = jnp.where(e_row == e_col, 1.0, 0.0).astype(BF16)

    def expand(rows, first_row):
        cols = slice(first_row * GROUP_HEADS, (first_row + 4) * GROUP_HEADS)
        return _dot(tm_scr[rows, :][:, cols].astype(BF16), spread)

    dec_scr[...] = expand(pl.ds(0, N_CHUNKS, stride=SSD_CHUNK), R_EF)

    row_i = lax.broadcasted_iota(jnp.int32, (SSD_CHUNK, SSD_CHUNK), 0)
    col_i = lax.broadcasted_iota(jnp.int32, (SSD_CHUNK, SSD_CHUNK), 1)
    lower = col_i <= row_i
    upper = col_i >= row_i
    left = lax.broadcasted_iota(jnp.int32, (SSD_CHUNK, LANES), 1) < SSD_HEAD_DIM
    dsk = dsk_ref[0]
    nw = nw_ref[0]

    def hm_row(row, h, r):
        return hm_scr[row * GROUP_HEADS + h:row * GROUP_HEADS + h + 1, pl.ds(r, SSD_CHUNK)]

    def forward_part(c):
        r = pl.multiple_of(c * SSD_CHUNK, SSD_CHUNK)
        rows = pl.ds(r, SSD_CHUNK)
        xs_c = xs_scr[rows, :]
        bt_c = bt_scr[:, rows]
        cm_c = cm_scr[rows, :]
        g = _dot(cm_c, bt_c)
        xs_b = xs_c.astype(BF16)
        colf = tm_scr[rows, R_COLF * GROUP_HEADS:(R_COLF + 1) * GROUP_HEADS]
        colb = tm_scr[rows, R_COLB * GROUP_HEADS:(R_COLB + 1) * GROUP_HEADS]
        y_pairs = []
        for hp_i in range(GROUP_HEADS // 2):
            ms = []
            for h in (2 * hp_i, 2 * hp_i + 1):
                seg_f = jnp.where(lower, colf[:, h:h + 1] - hm_row(R_ROWF, h, r), NEG)
                seg_b = jnp.where(upper, hm_row(R_ROWB, h, r) - colb[:, h:h + 1], NEG)
                ms.append((g * (jnp.exp2(seg_f) + jnp.exp2(seg_b))).astype(BF16))
            xp = xs_b[:, hp_i * LANES:(hp_i + 1) * LANES]
            zero = jnp.zeros_like(xp)
            rhs = jnp.concatenate([jnp.where(left, xp, zero), jnp.where(left, zero, xp)], axis=0)
            y_pairs.append(_dot(jnp.concatenate(ms, axis=1), rhs))
        y_diag = jnp.concatenate(y_pairs, axis=1)
        ex = expand(rows, R_WF)
        s_f = sf_scr[...]
        y_off = _dot(cm_c, s_f.astype(BF16)) * ex[:, W:2 * W]
        xw = (xs_c * ex[:, 0:W]).astype(BF16)
        sf_scr[...] = s_f * dec_scr[pl.ds(c, 1), 0:W] + _dot(bt_c, xw)
        return y_diag + y_off

    def backward_part(c):
        r = pl.multiple_of(c * SSD_CHUNK, SSD_CHUNK)
        rows = pl.ds(r, SSD_CHUNK)
        ex = expand(rows, R_WB)
        s_b = sb_scr[...]
        y_off = _dot(cm_scr[rows, :], s_b.astype(BF16)) * ex[:, W:2 * W]
        xw = (xs_scr[rows, :] * ex[:, 0:W]).astype(BF16)
        sb_scr[...] = s_b * dec_scr[pl.ds(c, 1), W:2 * W] + _dot(bt_scr[:, rows], xw)
        return y_off

    def finalize(c, y):
        r = pl.multiple_of(c * SSD_CHUNK, SSD_CHUNK)
        rows = pl.ds(r, SSD_CHUNK)
        y = y + y_scr[rows, :] + dsk * xs_scr[rows, :]
        y = y * _silu(z_ref[0, rows, :].astype(F32))
        y = y * lax.rsqrt(jnp.mean(y * y, axis=-1, keepdims=True) + EPS) * nw
        o_ref[0, rows, :] = y.astype(o_ref.dtype)

    sf_scr[...] = jnp.zeros((NS, W), F32)
    sb_scr[...] = jnp.zeros((NS, W), F32)
    half = N_CHUNKS // 2

    def first_half(i, carry):
        cb = N_CHUNKS - 1 - i
        y_scr[pl.ds(pl.multiple_of(i * SSD_CHUNK, SSD_CHUNK), SSD_CHUNK), :] = forward_part(i)
        y_scr[pl.ds(pl.multiple_of(cb * SSD_CHUNK, SSD_CHUNK), SSD_CHUNK), :] = backward_part(cb)
        return carry

    def second_half(i, carry):
        cb = N_CHUNKS - 1 - i
        finalize(i, forward_part(i))
        finalize(cb, backward_part(cb))
        return carry

    lax.fori_loop(0, half, first_half, 0, unroll=True)
    lax.fori_loop(half, N_CHUNKS, second_half, 0, unroll=True)


def _ssd(u3, dt_t, conv_w, conv_b, head_params, dsk_row, norm_w):
    b = u3.shape[0]
    L = SEQ
    W = GROUP_WIDTH
    NS = SSD_STATE
    G = SSD_GROUPS
    xblk = COL_X // W
    bblk = COL_B // NS
    cblk = COL_C // NS
    in_specs = [
        pl.BlockSpec((1, L, W), lambda i, g: (i, 0, g)),
        pl.BlockSpec((1, L, W), lambda i, g: (i, 0, xblk + g)),
        pl.BlockSpec((1, L, NS), lambda i, g: (i, 0, bblk + g)),
        pl.BlockSpec((1, L, NS), lambda i, g: (i, 0, cblk + g)),
        pl.BlockSpec((1, GROUP_HEADS, L), lambda i, g: (i, g, 0)),
        pl.BlockSpec((1, GROUP_HEADS, L), lambda i, g: (i, G + g, 0)),
        pl.BlockSpec((SSD_CONV, W), lambda i, g: (0, g)),
        pl.BlockSpec((SSD_CONV, NS), lambda i, g: (0, SSD_WIDTH // NS + g)),
        pl.BlockSpec((SSD_CONV, NS), lambda i, g: (0, SSD_WIDTH // NS + G + g)),
        pl.BlockSpec((1, W), lambda i, g: (0, g)),
        pl.BlockSpec((1, NS), lambda i, g: (0, SSD_WIDTH // NS + g)),
        pl.BlockSpec((1, NS), lambda i, g: (0, SSD_WIDTH // NS + G + g)),
        pl.BlockSpec((1, GROUP_HEADS, 8), lambda i, g: (g, 0, 0)),
        pl.BlockSpec((1, 1, W), lambda i, g: (g, 0, 0)),
        pl.BlockSpec((1, 1, W), lambda i, g: (g, 0, 0)),
    ]
    return pl.pallas_call(
        _ssd_kernel,
        grid=(b, G),
        in_specs=in_specs,
        out_specs=pl.BlockSpec((1, L, W), lambda i, g: (i, 0, g)),
        out_shape=jax.ShapeDtypeStruct((b, L, SSD_WIDTH), BF16),
        scratch_shapes=[
            pltpu.VMEM((3, (SSD_CONV - 1) * SSD_CHUNK, CONV_WIN), BF16),
            pltpu.VMEM((L, W), F32),
            pltpu.VMEM((NS, L), BF16),
            pltpu.VMEM((L, NS), BF16),
            pltpu.VMEM((L, W), F32),
            pltpu.VMEM((LANES, L), F32),
            pltpu.VMEM((L, LANES), F32),
            pltpu.VMEM((N_CHUNKS, 2 * W), F32),
            pltpu.VMEM((NS, W), F32),
            pltpu.VMEM((NS, W), F32),
        ],
        compiler_params=pltpu.CompilerParams(
            dimension_semantics=("arbitrary", "arbitrary"), vmem_limit_bytes=VMEM_LIMIT),
        name="ssd",
    )(u3, u3, u3, u3, dt_t, dt_t, conv_w, conv_w, conv_w, conv_b, conv_b, conv_b,
      head_params, dsk_row, norm_w)


def _group_mean_sq(v, seg):
    return _dot((v * v).astype(BF16), seg) * (1.0 / 64.0)


def _seg_matrix():
    r = lax.broadcasted_iota(jnp.int32, (LANES, LANES), 0) // 64
    c = lax.broadcasted_iota(jnp.int32, (LANES, LANES), 1) // 64
    return jnp.where(r == c, 1.0, 0.0).astype(BF16)


DIFF_TQ = 256
DIFF_HEADS_PER_STEP = 2


def _diff_kernel(lam_init, q_ref, k_ref, v_ref, g_ref, qc_ref, qs_ref, kc_ref, ks_ref, lam_ref,
                 sub_ref, o_ref, q_scr, k_scr, vt_scr, sa_scr, sb_scr):
    L = SEQ
    seg = _seg_matrix()
    comp0 = lax.broadcasted_iota(jnp.int32, (DIFF_TQ, LANES), 1) < 64
    p_row = lax.broadcasted_iota(jnp.int32, (LANES, LANES), 0)
    p_col = lax.broadcasted_iota(jnp.int32, (LANES, LANES), 1)
    swap = jnp.where((p_row ^ 32) == p_col, 1.0, 0.0).astype(BF16)

    def norm_rope(ref, r, lanes, cw_ref, sw_ref):
        vb = ref[0, pl.ds(r, DIFF_TQ), lanes]
        v = vb.astype(F32)
        rinv = lax.rsqrt(_group_mean_sq(v, seg) + EPS)
        return rinv * (v * cw_ref[pl.ds(r, DIFF_TQ), :] + _dot(vb, swap) * sw_ref[pl.ds(r, DIFF_TQ), :])

    n_blk = L // DIFF_TQ
    n_items = DIFF_HEADS_PER_STEP * n_blk

    def item(j):
        hd = j // n_blk
        r = pl.multiple_of((j % n_blk) * DIFF_TQ, DIFF_TQ)
        return hd, r, pl.ds(pl.multiple_of(hd * LANES, LANES), LANES)

    for hd in range(DIFF_HEADS_PER_STEP):
        vt_scr[hd, LANES:LANES + ONES_ROWS, :] = jnp.ones((ONES_ROWS, L), BF16)

    def prep(j, carry):
        hd, r, lanes = item(j)
        q_scr[hd, pl.ds(r, DIFF_TQ), :] = norm_rope(q_ref, r, lanes, qc_ref, qs_ref).astype(BF16)
        kn = norm_rope(k_ref, r, lanes, kc_ref, ks_ref)
        k_scr[hd, 0, pl.ds(r, DIFF_TQ), :] = jnp.where(comp0, kn, 0.0).astype(BF16)
        k_scr[hd, 1, pl.ds(r, DIFF_TQ), :] = jnp.where(comp0, 0.0, kn).astype(BF16)
        vt_scr[hd, 0:LANES, pl.ds(r, DIFF_TQ)] = v_ref[0, pl.ds(r, DIFF_TQ), lanes].T
        return carry

    lax.fori_loop(0, n_items, prep, 0, unroll=True)

    lam = lam_ref[...]
    lam_full = (jnp.exp(jnp.sum(lam[0:1] * lam[1:2], axis=-1, keepdims=True))
                - jnp.exp(jnp.sum(lam[2:3] * lam[3:4], axis=-1, keepdims=True)) + lam_init)

    def scores(j, s_ref):
        hd, r, _ = item(j)
        qb = q_scr[hd, pl.ds(r, DIFF_TQ), :]
        for c in range(2):
            s_ref[c] = _dot_nt(k_scr[hd, c], qb)

    def finish(j, s_ref):
        hd, r, lanes = item(j)
        parts = []
        for c in range(2):
            s = s_ref[c]
            e = jnp.exp2(s - jnp.max(s, axis=0, keepdims=True)).astype(BF16)
            acc = _dot(vt_scr[hd], e)
            parts.append(acc[0:LANES, :] * (1.0 / acc[LANES:LANES + 1, :]))
        ot = parts[0] - lam_full * parts[1]
        o = ot.T
        o = o * lax.rsqrt(jnp.mean(o * o, axis=-1, keepdims=True) + EPS) * sub_ref[...]
        o = o * (1.0 - lam_init) * _silu(g_ref[0, pl.ds(r, DIFF_TQ), lanes].astype(F32))
        o_ref[0, pl.ds(r, DIFF_TQ), lanes] = o.astype(o_ref.dtype)

    scores(0, sa_scr)

    def pair(i, carry):
        scores(2 * i + 1, sb_scr)
        finish(2 * i, sa_scr)
        scores(2 * i + 2, sa_scr)
        finish(2 * i + 1, sb_scr)
        return carry

    lax.fori_loop(0, n_items // 2 - 1, pair, 0, unroll=2)
    scores(n_items - 1, sb_scr)
    finish(n_items - 2, sa_scr)
    finish(n_items - 1, sb_scr)


def _diff(u3, rope_tabs, lam, subln_w, lam_init):
    b = u3.shape[0]
    L = SEQ
    step_w = DIFF_HEADS_PER_STEP * LANES
    base = COL_DIFF // step_w
    nb = DIFF_WIDTH // step_w

    def spec(part):
        return pl.BlockSpec((1, L, step_w), lambda i, h: (i, 0, base + part * nb + h))

    return pl.pallas_call(
        functools.partial(_diff_kernel, lam_init),
        grid=(b, DIFF_HEADS // DIFF_HEADS_PER_STEP),
        in_specs=[
            spec(0), spec(1), spec(2), spec(3),
            pl.BlockSpec((L, LANES), lambda i, h: (0, 0)),
            pl.BlockSpec((L, LANES), lambda i, h: (0, 0)),
            pl.BlockSpec((L, LANES), lambda i, h: (0, 0)),
            pl.BlockSpec((L, LANES), lambda i, h: (0, 0)),
            pl.BlockSpec((4, DIFF_HEAD_DIM), lambda i, h: (0, 0)),
            pl.BlockSpec((1, LANES), lambda i, h: (0, 0)),
        ],
        out_specs=pl.BlockSpec((1, L, step_w), lambda i, h: (i, 0, h)),
        out_shape=jax.ShapeDtypeStruct((b, L, DIFF_WIDTH), BF16),
        scratch_shapes=[
            pltpu.VMEM((DIFF_HEADS_PER_STEP, L, LANES), BF16),
            pltpu.VMEM((DIFF_HEADS_PER_STEP, 2, L, LANES), BF16),
            pltpu.VMEM((DIFF_HEADS_PER_STEP, LANES + ONES_ROWS, L), BF16),
            pltpu.VMEM((2, L, DIFF_TQ), F32),
            pltpu.VMEM((2, L, DIFF_TQ), F32),
        ],
        compiler_params=pltpu.CompilerParams(
            dimension_semantics=("arbitrary", "arbitrary"), vmem_limit_bytes=VMEM_LIMIT),
        name="diff_attn",
    )(u3, u3, u3, u3, *rope_tabs, lam, subln_w)


def _na_block_geometry(qb):
    r0 = qb * NA_QROWS
    kb = min(max(r0 - NA_KH // 2, 0), ROWS - NA_BAND)
    return r0, kb


def _na_bias_type(qb):
    return 0 if qb == 0 else (2 if qb == NA_NBLK - 1 else 1)


def _na_kernel(q_ref, k_ref, v_ref, g_ref, qkw_ref, bias_ref, o_ref, q_scr, k_scr, vt_scr, s_scr):
    L = SEQ
    seg = _seg_matrix()
    head0_blk = lax.broadcasted_iota(jnp.int32, (NA_QBLK, LANES), 1) < 64
    q_scale = (NA_HEAD_DIM ** -0.5) * LOG2E
    for h in range(2):
        vt_scr[h, NA_HEAD_DIM:NA_HEAD_DIM + ONES_ROWS, :] = jnp.ones((ONES_ROWS, L), BF16)

    def norm(ref, r, w_row):
        v = ref[0, pl.ds(r, NA_QBLK), :].astype(F32)
        return v * lax.rsqrt(_group_mean_sq(v, seg) + EPS) * w_row

    def prep(i, carry):
        r = pl.multiple_of(i * NA_QBLK, NA_QBLK)
        q_scr[pl.ds(r, NA_QBLK), :] = (norm(q_ref, r, qkw_ref[0:1, :]) * q_scale).astype(BF16)
        kn = norm(k_ref, r, qkw_ref[1:2, :])
        k_scr[0, pl.ds(r, NA_QBLK), :] = jnp.where(head0_blk, kn, 0.0).astype(BF16)
        k_scr[1, pl.ds(r, NA_QBLK), :] = jnp.where(head0_blk, 0.0, kn).astype(BF16)
        vt = v_ref[0, pl.ds(r, NA_QBLK), :].T
        for h in range(2):
            vt_scr[h, 0:NA_HEAD_DIM, pl.ds(r, NA_QBLK)] = vt[h * NA_HEAD_DIM:(h + 1) * NA_HEAD_DIM, :]
        return carry

    lax.fori_loop(0, L // NA_QBLK, prep, 0, unroll=True)

    def slices(qb):
        r0, kb = _na_block_geometry(qb)
        return (slice(r0 * GRID_W, r0 * GRID_W + NA_QBLK), slice(kb * GRID_W, kb * GRID_W + NA_KBLK))

    def scores(qb):
        qs, ks = slices(qb)
        q_blk = q_scr[qs, :]
        for h in range(2):
            s_scr[qb % NA_AHEAD, h] = _dot_nt(k_scr[h, ks, :], q_blk)

    def finish(qb):
        qs, ks = slices(qb)
        t = _na_bias_type(qb)
        outs = []
        for h in range(2):
            s = s_scr[qb % NA_AHEAD, h] + bias_ref[h, t]
            e = jnp.exp2(s - jnp.max(s, axis=0, keepdims=True)).astype(BF16)
            acc = _dot(vt_scr[h, :, ks], e)
            outs.append(acc[0:NA_HEAD_DIM, :] * (1.0 / acc[NA_HEAD_DIM:NA_HEAD_DIM + 1, :]))
        o = jnp.concatenate(outs, axis=0).T
        o = o * _silu(g_ref[0, qs, :].astype(F32))
        o_ref[0, qs, :] = o.astype(o_ref.dtype)

    for qb in range(NA_AHEAD - 1):
        scores(qb)
    for qb in range(NA_NBLK):
        if qb + NA_AHEAD - 1 < NA_NBLK:
            scores(qb + NA_AHEAD - 1)
        finish(qb)


def _na(u3, qk_w, bias, layer):
    b = u3.shape[0]
    L = SEQ
    base = COL_NA // LANES
    nb = NA_WIDTH // LANES

    def spec(part):
        return pl.BlockSpec((1, L, LANES), lambda hp, i: (i, 0, base + part * nb + hp))

    return pl.pallas_call(
        _na_kernel,
        grid=(NA_HEADS // 2, b),
        in_specs=[
            spec(0), spec(1), spec(2), spec(3),
            pl.BlockSpec((2, LANES), lambda hp, i: (0, 0)),
            pl.BlockSpec((None, 2, 3, NA_KBLK, NA_QBLK), lambda hp, i: (layer, hp, 0, 0, 0)),
        ],
        out_specs=pl.BlockSpec((1, L, LANES), lambda hp, i: (i, 0, hp)),
        out_shape=jax.ShapeDtypeStruct((b, L, NA_WIDTH), BF16),
        scratch_shapes=[
            pltpu.VMEM((L, LANES), BF16),
            pltpu.VMEM((2, L, LANES), BF16),
            pltpu.VMEM((2, NA_HEAD_DIM + ONES_ROWS, L), BF16),
            pltpu.VMEM((NA_AHEAD, 2, NA_KBLK, NA_QBLK), F32),
        ],
        compiler_params=pltpu.CompilerParams(
            dimension_semantics=("arbitrary", "arbitrary"), vmem_limit_bytes=VMEM_LIMIT),
        name="na_attn",
    )(u3, u3, u3, u3, qk_w, bias)


def _na_bias(rpb):
    lead = rpb.shape[:-2]
    n_dr = 2 * NA_KH - 1
    pad = GRID_W - NA_KW
    rev = jnp.pad(rpb.astype(F32)[..., ::-1] * LOG2E, [(0, 0)] * (rpb.ndim - 1) + [(pad, pad + 1)])
    rev = rev.reshape((-1, n_dr, LANES))
    out = pl.pallas_call(
        _na_bias_kernel,
        grid=(rev.shape[0],),
        in_specs=[pl.BlockSpec((1, n_dr, LANES), lambda n: (n, 0, 0))],
        out_specs=pl.BlockSpec((1, 3, NA_KBLK, NA_QBLK), lambda n: (n, 0, 0, 0)),
        out_shape=jax.ShapeDtypeStruct((rev.shape[0], 3, NA_KBLK, NA_QBLK), F32),
        scratch_shapes=[pltpu.VMEM((n_dr, GRID_W, LANES), F32)],
        compiler_params=pltpu.CompilerParams(
            dimension_semantics=("arbitrary",), vmem_limit_bytes=VMEM_LIMIT),
        name="na_bias",
    )(rev)
    return out.reshape(lead + (3, NA_KBLK, NA_QBLK))


def _na_bias_kernel(r_ref, o_ref, t_scr):
    k_i = lax.broadcasted_iota(jnp.int32, (GRID_W, LANES), 0)
    lane = lax.broadcasted_iota(jnp.int32, (GRID_W, LANES), 1)
    left = lane < GRID_W
    win_start = jnp.clip(lane % GRID_W - NA_KW // 2, 0, GRID_W - NA_KW)
    col_ok = (k_i >= win_start) & (k_i < win_start + NA_KW)
    for d in range(2 * NA_KH - 1):
        rows = jnp.broadcast_to(r_ref[0, d:d + 1, :], (GRID_W, LANES))
        skew = pltpu.roll(rows, GRID_W + 1, axis=1, stride=1, stride_axis=0)
        both = jnp.where(left, skew, pltpu.roll(skew, GRID_W, axis=1))
        t_scr[d] = jnp.where(col_ok, both, NEG)

    masked = jnp.full((GRID_W, LANES), NEG, F32)

    def tile(qb, j, i):
        r0, kb = _na_block_geometry(qb)
        r, rk = r0 + i, kb + j
        rs = min(max(r - NA_KH // 2, 0), ROWS - NA_KH)
        return t_scr[rk - r + NA_KH - 1] if rs <= rk < rs + NA_KH else masked

    for t, qb in enumerate((0, 1, NA_NBLK - 1)):
        for j in range(NA_BAND):
            for p in range(NA_QROWS // 2):
                o_ref[0, t, j * GRID_W:(j + 1) * GRID_W, p * LANES:(p + 1) * LANES] = jnp.where(
                    left, tile(qb, j, 2 * p), tile(qb, j, 2 * p + 1))


OUT_TM = 1024


def _outproj_kernel(x_ref, ys_ref, yd_ref, yn_ref, w_ref, o_ref):
    acc = _dot(ys_ref[...], w_ref[0:SSD_WIDTH, :])
    acc = acc + _dot(yd_ref[...], w_ref[SSD_WIDTH:SSD_WIDTH + DIFF_WIDTH, :])
    acc = acc + _dot(yn_ref[...], w_ref[SSD_WIDTH + DIFF_WIDTH:MIX_WIDTH, :])
    o_ref[...] = x_ref[...] + acc


def _outproj(x2, y_ssd, y_diff, y_na, w_out):
    m = x2.shape[0]
    return pl.pallas_call(
        _outproj_kernel,
        grid=(m // OUT_TM,),
        in_specs=[
            pl.BlockSpec((OUT_TM, D_MODEL), lambda i: (i, 0)),
            pl.BlockSpec((OUT_TM, SSD_WIDTH), lambda i: (i, 0)),
            pl.BlockSpec((OUT_TM, DIFF_WIDTH), lambda i: (i, 0)),
            pl.BlockSpec((OUT_TM, NA_WIDTH), lambda i: (i, 0)),
            pl.BlockSpec((MIX_WIDTH, D_MODEL), lambda i: (0, 0)),
        ],
        out_specs=pl.BlockSpec((OUT_TM, D_MODEL), lambda i: (i, 0)),
        out_shape=jax.ShapeDtypeStruct((m, D_MODEL), F32),
        compiler_params=pltpu.CompilerParams(
            dimension_semantics=("arbitrary",), vmem_limit_bytes=VMEM_LIMIT),
        name="outproj",
    )(x2, y_ssd, y_diff, y_na, w_out)


def _rope_tables():
    inv_freq = ROPE_THETA ** (-jnp.arange(0, DIFF_HEAD_DIM, 2, dtype=F32) / DIFF_HEAD_DIM)
    ang = jnp.arange(SEQ, dtype=F32)[:, None] * inv_freq[None, :]
    cos, sin = jnp.cos(ang), jnp.sin(ang)
    cos_t = jnp.concatenate([cos, cos, cos, cos], axis=1)
    sin_t = jnp.concatenate([-sin, sin, -sin, sin], axis=1)
    return cos_t, sin_t


def kernel(x, norm_w, w_in, conv_w, conv_b, a_log, dt_bias, d_skip, ssd_norm_w, diff_qk_norm,
           diff_lambda, diff_subln, na_qk_norm, na_rpb, w_out):
    b, L, d = x.shape
    assert (L, d) == (SEQ, D_MODEL)
    depth = w_in.shape[0]
    cos_t, sin_t = _rope_tables()
    na_bias = _na_bias(na_rpb)
    dt_lo = SSD_WIDTH + SSD_XBC
    dt_hi = dt_lo + 2 * SSD_HEADS
    w_t = jnp.swapaxes(w_in, 1, 2)
    w_main = jnp.concatenate([w_t[:, :dt_lo], w_t[:, dt_hi:]], axis=1).astype(BF16)
    w_dt = jnp.pad(w_t[:, dt_lo:dt_hi], ((0, 0), (0, DT_PAD - 2 * SSD_HEADS), (0, 0))).astype(BF16)
    x2 = x.reshape(b * L, d)
    for i in range(depth):
        lam_init = 0.8 - 0.6 * math.exp(-0.3 * i)
        u, dt = _inproj(x2, norm_w[i][None, :], w_main, w_dt, i)
        u3 = u.reshape(b, L, U_MAIN)
        dt_t = jnp.swapaxes(dt.reshape(b, L, DT_PAD)[:, :, :2 * SSD_HEADS], 1, 2)

        hp = jnp.stack([a_log[i][0], a_log[i][1], dt_bias[i][0], dt_bias[i][1]], axis=-1)
        hp = jnp.pad(hp, ((0, 0), (0, 4))).reshape(SSD_GROUPS, GROUP_HEADS, 8)
        dsk_row = jnp.repeat(d_skip[i][0] + d_skip[i][1], SSD_HEAD_DIM).reshape(SSD_GROUPS, 1, GROUP_WIDTH)
        y_ssd = _ssd(u3, dt_t, conv_w[i], conv_b[i][None, :], hp, dsk_row,
                     ssd_norm_w[i].reshape(SSD_GROUPS, 1, GROUP_WIDTH))

        half = DIFF_HEAD_DIM // 2
        qk_w = jnp.tile(diff_qk_norm[i], (1, 2))
        qk_w_sw = jnp.tile(jnp.concatenate([diff_qk_norm[i][:, half:], diff_qk_norm[i][:, :half]], axis=1), (1, 2))
        q_scale = (DIFF_HEAD_DIM ** -0.5) * LOG2E
        rope_tabs = (cos_t * (qk_w[0:1] * q_scale), sin_t * (qk_w_sw[0:1] * q_scale),
                     cos_t * qk_w[1:2], sin_t * qk_w_sw[1:2])
        y_diff = _diff(u3, rope_tabs, diff_lambda[i], diff_subln[i][None, :], lam_init)

        na_w = jnp.concatenate([na_qk_norm[i], na_qk_norm[i]], axis=1)
        y_na = _na(u3, na_w, na_bias, i)

        x2 = _outproj(x2, y_ssd.reshape(b * L, SSD_WIDTH), y_diff.reshape(b * L, DIFF_WIDTH),
                      y_na.reshape(b * L, NA_WIDTH), w_out[i].astype(BF16))
    return x2.reshape(b, L, d)
```

```python
import functools
import math

import numpy as np
import jax
import jax.numpy as jnp
from jax import lax
from jax.experimental import pallas as pl
from jax.experimental.pallas import tpu as pltpu

F32 = jnp.float32
BF16 = jnp.bfloat16

D_MODEL = 1024
SEQ = 2048
GRID_W = 64
ROWS = SEQ // GRID_W
SSD_WIDTH = 1024
SSD_HEAD_DIM = 64
SSD_HEADS = 16
SSD_GROUPS = 2
SSD_STATE = 128
SSD_CONV = 5
SSD_CHUNK = 128
SSD_XBC = SSD_WIDTH + 2 * SSD_GROUPS * SSD_STATE
GROUP_HEADS = SSD_HEADS // SSD_GROUPS
GROUP_WIDTH = SSD_WIDTH // SSD_GROUPS
N_CHUNKS = SEQ // SSD_CHUNK
DIFF_WIDTH = 512
DIFF_HEAD_DIM = 64
DIFF_HEADS = 4
NA_WIDTH = 512
NA_HEAD_DIM = 64
NA_HEADS = 8
NA_KH = 8
NA_KW = 16
MIX_WIDTH = SSD_WIDTH + DIFF_WIDTH + NA_WIDTH
ROPE_THETA = 10000.0
EPS = 1e-6
LANES = 128
CONV_HALO = 16
CONV_WIN = SSD_CHUNK + 2 * CONV_HALO

U_MAIN = SSD_WIDTH + SSD_XBC + 4 * DIFF_WIDTH + 4 * NA_WIDTH
COL_Z = 0
COL_X = SSD_WIDTH
COL_B = COL_X + SSD_WIDTH
COL_C = COL_B + SSD_GROUPS * SSD_STATE
COL_DIFF = COL_C + SSD_GROUPS * SSD_STATE
COL_NA = COL_DIFF + 4 * DIFF_WIDTH
DT_PAD = LANES

NA_QROWS = 4
NA_BAND = 12
NA_QBLK = NA_QROWS * GRID_W
NA_KBLK = NA_BAND * GRID_W
NA_NBLK = ROWS // NA_QROWS
NA_AHEAD = 2
NEG = -1e30
LOG2E = math.log2(math.e)
ONES_ROWS = 16

VMEM_LIMIT = 56 * 1024 * 1024


def _silu(v):
    h = 0.5 * v
    return h * jnp.tanh(h) + h


def _dot(a, b):
    return jnp.dot(a, b, preferred_element_type=F32)


def _dot_nt(a, b):
    return lax.dot_general(a, b, (((1,), (1,)), ((), ())), preferred_element_type=F32)


def _dot_tn(a, b):
    return lax.dot_general(a, b, (((0,), (0,)), ((), ())), preferred_element_type=F32)


IN_TM = 1024
IN_TN = 3328


def _inproj_kernel(x_ref, nw_ref, w_ref, wdt_ref, u_ref, dt_ref, h_scr):
    @pl.when(pl.program_id(1) == 0)
    def _():
        x = x_ref[...]
        ms = jnp.mean(x * x, axis=-1, keepdims=True)
        h = (x * lax.rsqrt(ms + EPS) * nw_ref[...]).astype(BF16)
        h_scr[...] = h
        dt_ref[...] = _dot_nt(h, wdt_ref[...])

    u_ref[...] = _dot_nt(h_scr[...], w_ref[...]).astype(u_ref.dtype)


def _inproj(x2, norm_w, w_main, w_dt, layer):
    m = x2.shape[0]
    return pl.pallas_call(
        _inproj_kernel,
        grid=(m // IN_TM, U_MAIN // IN_TN),
        in_specs=[
            pl.BlockSpec((IN_TM, D_MODEL), lambda i, j: (i, 0)),
            pl.BlockSpec((1, D_MODEL), lambda i, j: (0, 0)),
            pl.BlockSpec((None, IN_TN, D_MODEL), lambda i, j: (layer, j, 0)),
            pl.BlockSpec((None, DT_PAD, D_MODEL), lambda i, j: (layer, 0, 0)),
        ],
        out_specs=[
            pl.BlockSpec((IN_TM, IN_TN), lambda i, j: (i, j)),
            pl.BlockSpec((IN_TM, DT_PAD), lambda i, j: (i, 0)),
        ],
        out_shape=[
            jax.ShapeDtypeStruct((m, U_MAIN), BF16),
            jax.ShapeDtypeStruct((m, DT_PAD), F32),
        ],
        scratch_shapes=[pltpu.VMEM((IN_TM, D_MODEL), BF16)],
        compiler_params=pltpu.CompilerParams(
            dimension_semantics=("arbitrary", "arbitrary"), vmem_limit_bytes=VMEM_LIMIT),
        name="inproj",
    )(x2, norm_w, w_main, w_dt)


R_WF, R_DF = 0, 1
R_WB, R_DB = 4, 5
R_COLF, R_COLB = 8, 9
R_EF, R_EB = 10, 11
R_ROWF, R_ROWB = 14, 15


def _softplus(v):
    return jnp.maximum(v, 0.0) + jnp.log1p(jnp.exp(-jnp.abs(v)))


def _chunk_scan(a, lane, reverse):
    n = a.shape[-1]
    out = a
    k = 1
    while k < SSD_CHUNK:
        if reverse:
            out = out + jnp.where(lane < SSD_CHUNK - k, pltpu.roll(out, n - k, axis=1), 0.0)
        else:
            out = out + jnp.where(lane >= k, pltpu.roll(out, k, axis=1), 0.0)
        k *= 2
    return out


def _ssd_kernel(z_ref, x_ref, b_ref, c_ref, dtf_ref, dtb_ref, cwx_ref, cwb_ref, cwc_ref,
                cbx_ref, cbb_ref, cbc_ref, hp_ref, dsk_ref, nw_ref, o_ref,
                shift_scr, xs_scr, bt_scr, cm_scr, y_scr, hm_scr, tm_scr, dec_scr, sf_scr, sb_scr):
    L = SEQ
    W = GROUP_WIDTH
    NS = SSD_STATE

    side = (SSD_CONV - 1) * SSD_CHUNK
    tap_t = lax.broadcasted_iota(jnp.int32, (side, CONV_WIN), 0)
    tap_j = lax.broadcasted_iota(jnp.int32, (side, CONV_WIN), 1)
    tap_k = tap_t // SSD_CHUNK
    tap_src = tap_t % SSD_CHUNK + jnp.where(tap_k >= SSD_CONV // 2, tap_k + 1, tap_k) - SSD_CONV // 2
    for variant in range(3):
        shift_scr[variant] = jnp.where(tap_j == tap_src + variant * CONV_HALO, 1.0, 0.0).astype(BF16)
    side_taps = [k for k in range(SSD_CONV) if k != SSD_CONV // 2]

    def conv(c, carry):
        r = pl.multiple_of(c * SSD_CHUNK, SSD_CHUNK)
        w0 = pl.multiple_of(jnp.clip(r - CONV_HALO, 0, L - CONV_WIN), CONV_HALO)
        variant = jnp.where(c == 0, 0, jnp.where(c == N_CHUNKS - 1, 2, 1))
        shift = shift_scr[variant]
        rows = pl.ds(r, SSD_CHUNK)

        def taps(win, centre, cw, cb):
            sh = _dot(shift, win)
            acc = cb + centre.astype(F32) * cw[SSD_CONV // 2:SSD_CONV // 2 + 1, :]
            for slot, k in enumerate(side_taps):
                acc = acc + sh[slot * SSD_CHUNK:(slot + 1) * SSD_CHUNK, :] * cw[k:k + 1, :]
            return _silu(acc)

        xs_scr[rows, :] = taps(x_ref[0, pl.ds(w0, CONV_WIN), :], x_ref[0, rows, :], cwx_ref[...], cbx_ref[...])
        win_bc = jnp.concatenate([b_ref[0, pl.ds(w0, CONV_WIN), :], c_ref[0, pl.ds(w0, CONV_WIN), :]], axis=1)
        mid_bc = jnp.concatenate([b_ref[0, rows, :], c_ref[0, rows, :]], axis=1)
        act_bc = taps(win_bc, mid_bc, jnp.concatenate([cwb_ref[...], cwc_ref[...]], axis=1),
                      jnp.concatenate([cbb_ref[...], cbc_ref[...]], axis=1))
        bt_scr[:, rows] = act_bc[:, 0:NS].T.astype(BF16)
        cm_scr[rows, :] = act_bc[:, NS:2 * NS].astype(BF16)
        return carry

    lax.fori_loop(0, N_CHUNKS, conv, 0, unroll=8)

    hp = hp_ref[0]
    a_f = -jnp.exp(hp[:, 0:1])
    a_b = -jnp.exp(hp[:, 1:2])
    dt_f = _softplus(dtf_ref[0] + hp[:, 2:3])
    dt_b = _softplus(dtb_ref[0] + hp[:, 3:4])
    da_f = dt_f * a_f
    da_b = dt_b * a_b
    lane = lax.broadcasted_iota(jnp.int32, (GROUP_HEADS, L), 1) % SSD_CHUNK
    cs_f = _chunk_scan(da_f, lane, False)
    rs_f = _chunk_scan(da_f, lane, True) - da_f
    cs_b = _chunk_scan(da_b, lane, False)
    ecs_b = cs_b - da_b
    rs_b = _chunk_scan(da_b, lane, True)

    def put(row, v):
        hm_scr[row * GROUP_HEADS:(row + 1) * GROUP_HEADS, :] = v

    def put_split(row_hi, v):
        hi = v.astype(BF16).astype(F32)
        put(row_hi, hi)
        put(row_hi + 2, (v - hi).astype(BF16).astype(F32))

    put_split(R_WF, jnp.exp(rs_f) * dt_f)
    put_split(R_DF, jnp.exp(cs_f))
    put_split(R_WB, jnp.exp(ecs_b) * dt_b)
    put_split(R_DB, jnp.exp(rs_b))
    put_split(R_EF, jnp.exp(cs_f + rs_f))
    put_split(R_EB, jnp.exp(ecs_b + rs_b))
    put(R_COLF, cs_f * LOG2E)
    put(R_COLB, ecs_b * LOG2E)
    put(R_ROWF, cs_f * LOG2E - jnp.log2(dt_f))
    put(R_ROWB, ecs_b * LOG2E + jnp.log2(dt_b))

    def to_time_major(c, carry):
        r = pl.multiple_of(c * SSD_CHUNK, SSD_CHUNK)
        tm_scr[pl.ds(r, SSD_CHUNK), :] = hm_scr[:, pl.ds(r, SSD_CHUNK)].T
        return carry

    lax.fori_loop(0, N_CHUNKS, to_time_major, 0, unroll=True)

    e_row = lax.broadcasted_iota(jnp.int32, (4 * GROUP_HEADS, 2 * W), 0) % (2 * GROUP_HEADS)
    e_col = lax.broadcasted_iota(jnp.int32, (4 * GROUP_HEADS, 2 * W), 1) // SSD_HEAD_DIM
    spread = jnp.where(e_row == e_col, 1.0, 0.0).astype(BF16)

    def expand(rows, first_row):
        cols = slice(first_row * GROUP_HEADS, (first_row + 4) * GROUP_HEADS)
        return _dot(tm_scr[rows, :][:, cols].astype(BF16), spread)

    dec_scr[...] = expand(pl.ds(0, N_CHUNKS, stride=SSD_CHUNK), R_EF)

    row_i = lax.broadcasted_iota(jnp.int32, (SSD_CHUNK, SSD_CHUNK), 0)
    col_i = lax.broadcasted_iota(jnp.int32, (SSD_CHUNK, SSD_CHUNK), 1)
    lower = col_i <= row_i
    upper = col_i >= row_i
    left = lax.broadcasted_iota(jnp.int32, (SSD_CHUNK, LANES), 1) < SSD_HEAD_DIM
    dsk = dsk_ref[0]
    nw = nw_ref[0]

    def hm_row(row, h, r):
        return hm_scr[row * GROUP_HEADS + h:row * GROUP_HEADS + h + 1, pl.ds(r, SSD_CHUNK)]

    def forward_part(c):
        r = pl.multiple_of(c * SSD_CHUNK, SSD_CHUNK)
        rows = pl.ds(r, SSD_CHUNK)
        xs_c = xs_scr[rows, :]
        bt_c = bt_scr[:, rows]
        cm_c = cm_scr[rows, :]
        g = _dot(cm_c, bt_c)
        xs_b = xs_c.astype(BF16)
        colf = tm_scr[rows, R_COLF * GROUP_HEADS:(R_COLF + 1) * GROUP_HEADS]
        colb = tm_scr[rows, R_COLB * GROUP_HEADS:(R_COLB + 1) * GROUP_HEADS]
        y_pairs = []
        for hp_i in range(GROUP_HEADS // 2):
            ms = []
            for h in (2 * hp_i, 2 * hp_i + 1):
                seg_f = jnp.where(lower, colf[:, h:h + 1] - hm_row(R_ROWF, h, r), NEG)
                seg_b = jnp.where(upper, hm_row(R_ROWB, h, r) - colb[:, h:h + 1], NEG)
                ms.append((g * (jnp.exp2(seg_f) + jnp.exp2(seg_b))).astype(BF16))
            xp = xs_b[:, hp_i * LANES:(hp_i + 1) * LANES]
            zero = jnp.zeros_like(xp)
            rhs = jnp.concatenate([jnp.where(left, xp, zero), jnp.where(left, zero, xp)], axis=0)
            y_pairs.append(_dot(jnp.concatenate(ms, axis=1), rhs))
        y_diag = jnp.concatenate(y_pairs, axis=1)
        ex = expand(rows, R_WF)
        s_f = sf_scr[...]
        y_off = _dot(cm_c, s_f.astype(BF16)) * ex[:, W:2 * W]
        xw = (xs_c * ex[:, 0:W]).astype(BF16)
        sf_scr[...] = s_f * dec_scr[pl.ds(c, 1), 0:W] + _dot(bt_c, xw)
        return y_diag + y_off

    def backward_part(c):
        r = pl.multiple_of(c * SSD_CHUNK, SSD_CHUNK)
        rows = pl.ds(r, SSD_CHUNK)
        ex = expand(rows, R_WB)
        s_b = sb_scr[...]
        y_off = _dot(cm_scr[rows, :], s_b.astype(BF16)) * ex[:, W:2 * W]
        xw = (xs_scr[rows, :] * ex[:, 0:W]).astype(BF16)
        sb_scr[...] = s_b * dec_scr[pl.ds(c, 1), W:2 * W] + _dot(bt_scr[:, rows], xw)
        return y_off

    def finalize(c, y):
        r = pl.multiple_of(c * SSD_CHUNK, SSD_CHUNK)
        rows = pl.ds(r, SSD_CHUNK)
        y = y + y_scr[rows, :] + dsk * xs_scr[rows, :]
        y = y * _silu(z_ref[0, rows, :].astype(F32))
        y = y * lax.rsqrt(jnp.mean(y * y, axis=-1, keepdims=True) + EPS) * nw
        o_ref[0, rows, :] = y.astype(o_ref.dtype)

    sf_scr[...] = jnp.zeros((NS, W), F32)
    sb_scr[...] = jnp.zeros((NS, W), F32)
    half = N_CHUNKS // 2

    def first_half(i, carry):
        cb = N_CHUNKS - 1 - i
        y_scr[pl.ds(pl.multiple_of(i * SSD_CHUNK, SSD_CHUNK), SSD_CHUNK), :] = forward_part(i)
        y_scr[pl.ds(pl.multiple_of(cb * SSD_CHUNK, SSD_CHUNK), SSD_CHUNK), :] = backward_part(cb)
        return carry

    def second_half(i, carry):
        cb = N_CHUNKS - 1 - i
        finalize(i, forward_part(i))
        finalize(cb, backward_part(cb))
        return carry

    lax.fori_loop(0, half, first_half, 0, unroll=True)
    lax.fori_loop(half, N_CHUNKS, second_half, 0, unroll=True)


def _ssd(u3, dt_t, conv_w, conv_b, head_params, dsk_row, norm_w):
    b = u3.shape[0]
    L = SEQ
    W = GROUP_WIDTH
    NS = SSD_STATE
    G = SSD_GROUPS
    xblk = COL_X // W
    bblk = COL_B // NS
    cblk = COL_C // NS
    in_specs = [
        pl.BlockSpec((1, L, W), lambda i, g: (i, 0, g)),
        pl.BlockSpec((1, L, W), lambda i, g: (i, 0, xblk + g)),
        pl.BlockSpec((1, L, NS), lambda i, g: (i, 0, bblk + g)),
        pl.BlockSpec((1, L, NS), lambda i, g: (i, 0, cblk + g)),
        pl.BlockSpec((1, GROUP_HEADS, L), lambda i, g: (i, g, 0)),
        pl.BlockSpec((1, GROUP_HEADS, L), lambda i, g: (i, G + g, 0)),
        pl.BlockSpec((SSD_CONV, W), lambda i, g: (0, g)),
        pl.BlockSpec((SSD_CONV, NS), lambda i, g: (0, SSD_WIDTH // NS + g)),
        pl.BlockSpec((SSD_CONV, NS), lambda i, g: (0, SSD_WIDTH // NS + G + g)),
        pl.BlockSpec((1, W), lambda i, g: (0, g)),
        pl.BlockSpec((1, NS), lambda i, g: (0, SSD_WIDTH // NS + g)),
        pl.BlockSpec((1, NS), lambda i, g: (0, SSD_WIDTH // NS + G + g)),
        pl.BlockSpec((1, GROUP_HEADS, 8), lambda i, g: (g, 0, 0)),
        pl.BlockSpec((1, 1, W), lambda i, g: (g, 0, 0)),
        pl.BlockSpec((1, 1, W), lambda i, g: (g, 0, 0)),
    ]
    return pl.pallas_call(
        _ssd_kernel,
        grid=(b, G),
        in_specs=in_specs,
        out_specs=pl.BlockSpec((1, L, W), lambda i, g: (i, 0, g)),
        out_shape=jax.ShapeDtypeStruct((b, L, SSD_WIDTH), BF16),
        scratch_shapes=[
            pltpu.VMEM((3, (SSD_CONV - 1) * SSD_CHUNK, CONV_WIN), BF16),
            pltpu.VMEM((L, W), F32),
            pltpu.VMEM((NS, L), BF16),
            pltpu.VMEM((L, NS), BF16),
            pltpu.VMEM((L, W), F32),
            pltpu.VMEM((LANES, L), F32),
            pltpu.VMEM((L, LANES), F32),
            pltpu.VMEM((N_CHUNKS, 2 * W), F32),
            pltpu.VMEM((NS, W), F32),
            pltpu.VMEM((NS, W), F32),
        ],
        compiler_params=pltpu.CompilerParams(
            dimension_semantics=("arbitrary", "arbitrary"), vmem_limit_bytes=VMEM_LIMIT),
        name="ssd",
    )(u3, u3, u3, u3, dt_t, dt_t, conv_w, conv_w, conv_w, conv_b, conv_b, conv_b,
      head_params, dsk_row, norm_w)


def _group_mean_sq(v, seg):
    return _dot((v * v).astype(BF16), seg) * (1.0 / 64.0)


def _seg_matrix():
    r = lax.broadcasted_iota(jnp.int32, (LANES, LANES), 0) // 64
    c = lax.broadcasted_iota(jnp.int32, (LANES, LANES), 1) // 64
    return jnp.where(r == c, 1.0, 0.0).astype(BF16)


DIFF_TQ = 256
DIFF_HEADS_PER_STEP = 4


def _diff_kernel(lam_init, q_ref, k_ref, v_ref, g_ref, qc_ref, qs_ref, kc_ref, ks_ref, lam_ref,
                 sub_ref, o_ref, q_scr, k_scr, vt_scr, sa_scr, sb_scr):
    L = SEQ
    seg = _seg_matrix()
    comp0 = lax.broadcasted_iota(jnp.int32, (DIFF_TQ, LANES), 1) < 64
    p_row = lax.broadcasted_iota(jnp.int32, (LANES, LANES), 0)
    p_col = lax.broadcasted_iota(jnp.int32, (LANES, LANES), 1)
    swap = jnp.where((p_row ^ 32) == p_col, 1.0, 0.0).astype(BF16)

    def norm_rope(ref, r, lanes, cw_ref, sw_ref):
        vb = ref[0, pl.ds(r, DIFF_TQ), lanes]
        v = vb.astype(F32)
        rinv = lax.rsqrt(_group_mean_sq(v, seg) + EPS)
        return rinv * (v * cw_ref[pl.ds(r, DIFF_TQ), :] + _dot(vb, swap) * sw_ref[pl.ds(r, DIFF_TQ), :])

    n_blk = L // DIFF_TQ
    n_items = DIFF_HEADS_PER_STEP * n_blk

    def item(j):
        hd = j // n_blk
        r = pl.multiple_of((j % n_blk) * DIFF_TQ, DIFF_TQ)
        return hd, r, pl.ds(pl.multiple_of(hd * LANES, LANES), LANES)

    for hd in range(DIFF_HEADS_PER_STEP):
        vt_scr[hd, LANES:LANES + ONES_ROWS, :] = jnp.ones((ONES_ROWS, L), BF16)

    def prep(j, carry):
        hd, r, lanes = item(j)
        q_scr[hd, pl.ds(r, DIFF_TQ), :] = norm_rope(q_ref, r, lanes, qc_ref, qs_ref).astype(BF16)
        kn = norm_rope(k_ref, r, lanes, kc_ref, ks_ref)
        k_scr[hd, 0, pl.ds(r, DIFF_TQ), :] = jnp.where(comp0, kn, 0.0).astype(BF16)
        k_scr[hd, 1, pl.ds(r, DIFF_TQ), :] = jnp.where(comp0, 0.0, kn).astype(BF16)
        vt_scr[hd, 0:LANES, pl.ds(r, DIFF_TQ)] = v_ref[0, pl.ds(r, DIFF_TQ), lanes].T
        return carry

    lax.fori_loop(0, n_items, prep, 0, unroll=True)

    lam = lam_ref[...]
    lam_full = (jnp.exp(jnp.sum(lam[0:1] * lam[1:2], axis=-1, keepdims=True))
                - jnp.exp(jnp.sum(lam[2:3] * lam[3:4], axis=-1, keepdims=True)) + lam_init)

    def scores(j, s_ref):
        hd, r, _ = item(j)
        qb = q_scr[hd, pl.ds(r, DIFF_TQ), :]
        for c in range(2):
            s_ref[c] = _dot_nt(k_scr[hd, c], qb)

    def finish(j, s_ref):
        hd, r, lanes = item(j)
        parts = []
        for c in range(2):
            s = s_ref[c]
            e = jnp.exp2(s - jnp.max(s, axis=0, keepdims=True)).astype(BF16)
            acc = _dot(vt_scr[hd], e)
            parts.append(acc[0:LANES, :] * (1.0 / acc[LANES:LANES + 1, :]))
        ot = parts[0] - lam_full * parts[1]
        o = ot.T
        o = o * lax.rsqrt(jnp.mean(o * o, axis=-1, keepdims=True) + EPS) * sub_ref[...]
        o = o * (1.0 - lam_init) * _silu(g_ref[0, pl.ds(r, DIFF_TQ), lanes].astype(F32))
        o_ref[0, pl.ds(r, DIFF_TQ), lanes] = o.astype(o_ref.dtype)

    scores(0, sa_scr)

    def pair(i, carry):
        scores(2 * i + 1, sb_scr)
        finish(2 * i, sa_scr)
        scores(2 * i + 2, sa_scr)
        finish(2 * i + 1, sb_scr)
        return carry

    lax.fori_loop(0, n_items // 2 - 1, pair, 0, unroll=2)
    scores(n_items - 1, sb_scr)
    finish(n_items - 2, sa_scr)
    finish(n_items - 1, sb_scr)


def _diff(u3, rope_tabs, lam, subln_w, lam_init):
    b = u3.shape[0]
    L = SEQ
    step_w = DIFF_HEADS_PER_STEP * LANES
    base = COL_DIFF // step_w
    nb = DIFF_WIDTH // step_w

    def spec(part):
        return pl.BlockSpec((1, L, step_w), lambda i, h: (i, 0, base + part * nb + h))

    return pl.pallas_call(
        functools.partial(_diff_kernel, lam_init),
        grid=(b, DIFF_HEADS // DIFF_HEADS_PER_STEP),
        in_specs=[
            spec(0), spec(1), spec(2), spec(3),
            pl.BlockSpec((L, LANES), lambda i, h: (0, 0)),
            pl.BlockSpec((L, LANES), lambda i, h: (0, 0)),
            pl.BlockSpec((L, LANES), lambda i, h: (0, 0)),
            pl.BlockSpec((L, LANES), lambda i, h: (0, 0)),
            pl.BlockSpec((4, DIFF_HEAD_DIM), lambda i, h: (0, 0)),
            pl.BlockSpec((1, LANES), lambda i, h: (0, 0)),
        ],
        out_specs=pl.BlockSpec((1, L, step_w), lambda i, h: (i, 0, h)),
        out_shape=jax.ShapeDtypeStruct((b, L, DIFF_WIDTH), BF16),
        scratch_shapes=[
            pltpu.VMEM((DIFF_HEADS_PER_STEP, L, LANES), BF16),
            pltpu.VMEM((DIFF_HEADS_PER_STEP, 2, L, LANES), BF16),
            pltpu.VMEM((DIFF_HEADS_PER_STEP, LANES + ONES_ROWS, L), BF16),
            pltpu.VMEM((2, L, DIFF_TQ), F32),
            pltpu.VMEM((2, L, DIFF_TQ), F32),
        ],
        compiler_params=pltpu.CompilerParams(
            dimension_semantics=("arbitrary", "arbitrary"), vmem_limit_bytes=VMEM_LIMIT),
        name="diff_attn",
    )(u3, u3, u3, u3, *rope_tabs, lam, subln_w)


def _na_block_geometry(qb):
    r0 = qb * NA_QROWS
    kb = min(max(r0 - NA_KH // 2, 0), ROWS - NA_BAND)
    return r0, kb


def _na_bias_type(qb):
    return 0 if qb == 0 else (2 if qb == NA_NBLK - 1 else 1)


def _na_kernel(q_ref, k_ref, v_ref, g_ref, qkw_ref, bias_ref, o_ref, q_scr, k_scr, vt_scr, s_scr):
    L = SEQ
    seg = _seg_matrix()
    head0_blk = lax.broadcasted_iota(jnp.int32, (NA_QBLK, LANES), 1) < 64
    q_scale = (NA_HEAD_DIM ** -0.5) * LOG2E
    for h in range(2):
        vt_scr[h, NA_HEAD_DIM:NA_HEAD_DIM + ONES_ROWS, :] = jnp.ones((ONES_ROWS, L), BF16)

    def norm(ref, r, w_row):
        v = ref[0, pl.ds(r, NA_QBLK), :].astype(F32)
        return v * lax.rsqrt(_group_mean_sq(v, seg) + EPS) * w_row

    def prep(i, carry):
        r = pl.multiple_of(i * NA_QBLK, NA_QBLK)
        q_scr[pl.ds(r, NA_QBLK), :] = (norm(q_ref, r, qkw_ref[0:1, :]) * q_scale).astype(BF16)
        kn = norm(k_ref, r, qkw_ref[1:2, :])
        k_scr[0, pl.ds(r, NA_QBLK), :] = jnp.where(head0_blk, kn, 0.0).astype(BF16)
        k_scr[1, pl.ds(r, NA_QBLK), :] = jnp.where(head0_blk, 0.0, kn).astype(BF16)
        vt = v_ref[0, pl.ds(r, NA_QBLK), :].T
        for h in range(2):
            vt_scr[h, 0:NA_HEAD_DIM, pl.ds(r, NA_QBLK)] = vt[h * NA_HEAD_DIM:(h + 1) * NA_HEAD_DIM, :]
        return carry

    lax.fori_loop(0, L // NA_QBLK, prep, 0, unroll=True)

    def slices(qb):
        r0, kb = _na_block_geometry(qb)
        return (slice(r0 * GRID_W, r0 * GRID_W + NA_QBLK), slice(kb * GRID_W, kb * GRID_W + NA_KBLK))

    def scores(qb):
        qs, ks = slices(qb)
        q_blk = q_scr[qs, :]
        for h in range(2):
            s_scr[qb % NA_AHEAD, h] = _dot_nt(k_scr[h, ks, :], q_blk)

    def finish(qb):
        qs, ks = slices(qb)
        t = _na_bias_type(qb)
        outs = []
        for h in range(2):
            s = s_scr[qb % NA_AHEAD, h] + bias_ref[h, t]
            e = jnp.exp2(s - jnp.max(s, axis=0, keepdims=True)).astype(BF16)
            acc = _dot(vt_scr[h, :, ks], e)
            outs.append(acc[0:NA_HEAD_DIM, :] * (1.0 / acc[NA_HEAD_DIM:NA_HEAD_DIM + 1, :]))
        o = jnp.concatenate(outs, axis=0).T
        o = o * _silu(g_ref[0, qs, :].astype(F32))
        o_ref[0, qs, :] = o.astype(o_ref.dtype)

    for qb in range(NA_AHEAD - 1):
        scores(qb)
    for qb in range(NA_NBLK):
        if qb + NA_AHEAD - 1 < NA_NBLK:
            scores(qb + NA_AHEAD - 1)
        finish(qb)


def _na(u3, qk_w, bias, layer):
    b = u3.shape[0]
    L = SEQ
    base = COL_NA // LANES
    nb = NA_WIDTH // LANES

    def spec(part):
        return pl.BlockSpec((1, L, LANES), lambda hp, i: (i, 0, base + part * nb + hp))

    return pl.pallas_call(
        _na_kernel,
        grid=(NA_HEADS // 2, b),
        in_specs=[
            spec(0), spec(1), spec(2), spec(3),
            pl.BlockSpec((2, LANES), lambda hp, i: (0, 0)),
            pl.BlockSpec((None, 2, 3, NA_KBLK, NA_QBLK), lambda hp, i: (layer, hp, 0, 0, 0)),
        ],
        out_specs=pl.BlockSpec((1, L, LANES), lambda hp, i: (i, 0, hp)),
        out_shape=jax.ShapeDtypeStruct((b, L, NA_WIDTH), BF16),
        scratch_shapes=[
            pltpu.VMEM((L, LANES), BF16),
            pltpu.VMEM((2, L, LANES), BF16),
            pltpu.VMEM((2, NA_HEAD_DIM + ONES_ROWS, L), BF16),
            pltpu.VMEM((NA_AHEAD, 2, NA_KBLK, NA_QBLK), F32),
        ],
        compiler_params=pltpu.CompilerParams(
            dimension_semantics=("arbitrary", "arbitrary"), vmem_limit_bytes=VMEM_LIMIT),
        name="na_attn",
    )(u3, u3, u3, u3, qk_w, bias)


def _na_bias(rpb):
    lead = rpb.shape[:-2]
    n_dr = 2 * NA_KH - 1
    pad = GRID_W - NA_KW
    rev = jnp.pad(rpb.astype(F32)[..., ::-1] * LOG2E, [(0, 0)] * (rpb.ndim - 1) + [(pad, pad + 1)])
    rev = rev.reshape((-1, n_dr, LANES))
    out = pl.pallas_call(
        _na_bias_kernel,
        grid=(rev.shape[0],),
        in_specs=[pl.BlockSpec((1, n_dr, LANES), lambda n: (n, 0, 0))],
        out_specs=pl.BlockSpec((1, 3, NA_KBLK, NA_QBLK), lambda n: (n, 0, 0, 0)),
        out_shape=jax.ShapeDtypeStruct((rev.shape[0], 3, NA_KBLK, NA_QBLK), F32),
        scratch_shapes=[pltpu.VMEM((n_dr, GRID_W, LANES), F32)],
        compiler_params=pltpu.CompilerParams(
            dimension_semantics=("arbitrary",), vmem_limit_bytes=VMEM_LIMIT),
        name="na_bias",
    )(rev)
    return out.reshape(lead + (3, NA_KBLK, NA_QBLK))


def _na_bias_kernel(r_ref, o_ref, t_scr):
    k_i = lax.broadcasted_iota(jnp.int32, (GRID_W, LANES), 0)
    lane = lax.broadcasted_iota(jnp.int32, (GRID_W, LANES), 1)
    left = lane < GRID_W
    win_start = jnp.clip(lane % GRID_W - NA_KW // 2, 0, GRID_W - NA_KW)
    col_ok = (k_i >= win_start) & (k_i < win_start + NA_KW)
    for d in range(2 * NA_KH - 1):
        rows = jnp.broadcast_to(r_ref[0, d:d + 1, :], (GRID_W, LANES))
        skew = pltpu.roll(rows, GRID_W + 1, axis=1, stride=1, stride_axis=0)
        both = jnp.where(left, skew, pltpu.roll(skew, GRID_W, axis=1))
        t_scr[d] = jnp.where(col_ok, both, NEG)

    masked = jnp.full((GRID_W, LANES), NEG, F32)

    def tile(qb, j, i):
        r0, kb = _na_block_geometry(qb)
        r, rk = r0 + i, kb + j
        rs = min(max(r - NA_KH // 2, 0), ROWS - NA_KH)
        return t_scr[rk - r + NA_KH - 1] if rs <= rk < rs + NA_KH else masked

    for t, qb in enumerate((0, 1, NA_NBLK - 1)):
        for j in range(NA_BAND):
            for p in range(NA_QROWS // 2):
                o_ref[0, t, j * GRID_W:(j + 1) * GRID_W, p * LANES:(p + 1) * LANES] = jnp.where(
                    left, tile(qb, j, 2 * p), tile(qb, j, 2 * p + 1))


OUT_TM = 1024


def _outproj_kernel(x_ref, ys_ref, yd_ref, yn_ref, w_ref, o_ref):
    acc = _dot(ys_ref[...], w_ref[0:SSD_WIDTH, :])
    acc = acc + _dot(yd_ref[...], w_ref[SSD_WIDTH:SSD_WIDTH + DIFF_WIDTH, :])
    acc = acc + _dot(yn_ref[...], w_ref[SSD_WIDTH + DIFF_WIDTH:MIX_WIDTH, :])
    o_ref[...] = x_ref[...] + acc


def _outproj(x2, y_ssd, y_diff, y_na, w_out):
    m = x2.shape[0]
    return pl.pallas_call(
        _outproj_kernel,
        grid=(m // OUT_TM,),
        in_specs=[
            pl.BlockSpec((OUT_TM, D_MODEL), lambda i: (i, 0)),
            pl.BlockSpec((OUT_TM, SSD_WIDTH), lambda i: (i, 0)),
            pl.BlockSpec((OUT_TM, DIFF_WIDTH), lambda i: (i, 0)),
            pl.BlockSpec((OUT_TM, NA_WIDTH), lambda i: (i, 0)),
            pl.BlockSpec((MIX_WIDTH, D_MODEL), lambda i: (0, 0)),
        ],
        out_specs=pl.BlockSpec((OUT_TM, D_MODEL), lambda i: (i, 0)),
        out_shape=jax.ShapeDtypeStruct((m, D_MODEL), F32),
        compiler_params=pltpu.CompilerParams(
            dimension_semantics=("arbitrary",), vmem_limit_bytes=VMEM_LIMIT),
        name="outproj",
    )(x2, y_ssd, y_diff, y_na, w_out)


def _rope_tables():
    inv_freq = ROPE_THETA ** (-jnp.arange(0, DIFF_HEAD_DIM, 2, dtype=F32) / DIFF_HEAD_DIM)
    ang = jnp.arange(SEQ, dtype=F32)[:, None] * inv_freq[None, :]
    cos, sin = jnp.cos(ang), jnp.sin(ang)
    cos_t = jnp.concatenate([cos, cos, cos, cos], axis=1)
    sin_t = jnp.concatenate([-sin, sin, -sin, sin], axis=1)
    return cos_t, sin_t


def kernel(x, norm_w, w_in, conv_w, conv_b, a_log, dt_bias, d_skip, ssd_norm_w, diff_qk_norm,
           diff_lambda, diff_subln, na_qk_norm, na_rpb, w_out):
    b, L, d = x.shape
    assert (L, d) == (SEQ, D_MODEL)
    depth = w_in.shape[0]
    cos_t, sin_t = _rope_tables()
    na_bias = _na_bias(na_rpb)
    dt_lo = SSD_WIDTH + SSD_XBC
    dt_hi = dt_lo + 2 * SSD_HEADS
    w_t = jnp.swapaxes(w_in, 1, 2)
    w_main = jnp.concatenate([w_t[:, :dt_lo], w_t[:, dt_hi:]], axis=1).astype(BF16)
    w_dt = jnp.pad(w_t[:, dt_lo:dt_hi], ((0, 0), (0, DT_PAD - 2 * SSD_HEADS), (0, 0))).astype(BF16)
    x2 = x.reshape(b * L, d)
    for i in range(depth):
        lam_init = 0.8 - 0.6 * math.exp(-0.3 * i)
        u, dt = _inproj(x2, norm_w[i][None, :], w_main, w_dt, i)
        u3 = u.reshape(b, L, U_MAIN)
        dt_t = jnp.swapaxes(dt.reshape(b, L, DT_PAD)[:, :, :2 * SSD_HEADS], 1, 2)

        hp = jnp.stack([a_log[i][0], a_log[i][1], dt_bias[i][0], dt_bias[i][1]], axis=-1)
        hp = jnp.pad(hp, ((0, 0), (0, 4))).reshape(SSD_GROUPS, GROUP_HEADS, 8)
        dsk_row = jnp.repeat(d_skip[i][0] + d_skip[i][1], SSD_HEAD_DIM).reshape(SSD_GROUPS, 1, GROUP_WIDTH)
        y_ssd = _ssd(u3, dt_t, conv_w[i], conv_b[i][None, :], hp, dsk_row,
                     ssd_norm_w[i].reshape(SSD_GROUPS, 1, GROUP_WIDTH))

        half = DIFF_HEAD_DIM // 2
        qk_w = jnp.tile(diff_qk_norm[i], (1, 2))
        qk_w_sw = jnp.tile(jnp.concatenate([diff_qk_norm[i][:, half:], diff_qk_norm[i][:, :half]], axis=1), (1, 2))
        q_scale = (DIFF_HEAD_DIM ** -0.5) * LOG2E
        rope_tabs = (cos_t * (qk_w[0:1] * q_scale), sin_t * (qk_w_sw[0:1] * q_scale),
                     cos_t * qk_w[1:2], sin_t * qk_w_sw[1:2])
        y_diff = _diff(u3, rope_tabs, diff_lambda[i], diff_subln[i][None, :], lam_init)

        na_w = jnp.concatenate([na_qk_norm[i], na_qk_norm[i]], axis=1)
        y_na = _na(u3, na_w, na_bias, i)

        x2 = _outproj(x2, y_ssd.reshape(b * L, SSD_WIDTH), y_diff.reshape(b * L, DIFF_WIDTH),
                      y_na.reshape(b * L, NA_WIDTH), w_out[i].astype(BF16))
    return x2.reshape(b, L, d)
```
